```python
import math
import jax
import jax.numpy as jnp
from jax import lax
import numpy as np

D_MODEL = 1024
BATCH = 16
SEQ = 2048
DEPTH = 2

HEAD_DIM = 64
GRID_W = 64
Q_BLOCK = 128
ROPE_THETA = 10000.0
NORM_EPS = 1e-6
NEG_INF = -1e30

NUM_BUCKETS = 32
REL_MAX_DISTANCE = 1024

MLA_HEADS = 4
MLA_NOPE_DIM = 64
MLA_ROPE_DIM = 32
MLA_QK_DIM = MLA_NOPE_DIM + MLA_ROPE_DIM
MLA_V_DIM = 64
MLA_Q_RANK = 256
MLA_KV_RANK = 128

DIL_HEADS = 4
DIL_PATTERNS = ((128, 1), (512, 4), (2048, 16))
DIL_BLOCK = max(w // (2 * d) for w, d in DIL_PATTERNS)

GQA_HEADS = 4
GQA_KV_HEADS = 2
GQA_GROUP = GQA_HEADS // GQA_KV_HEADS
AXIAL_DIM = HEAD_DIM // 2

DIFF_HEADS = 4
DIFF_QK_DIM = HEAD_DIM // 2
DIFF_V_DIM = HEAD_DIM

NUM_BIAS_HEADS = DIL_HEADS + DIFF_HEADS

D_MIX = MLA_HEADS * MLA_V_DIM + DIL_HEADS * HEAD_DIM + GQA_HEADS * HEAD_DIM + DIFF_HEADS * DIFF_V_DIM

IN_SIZES = (
    MLA_Q_RANK, MLA_KV_RANK, MLA_ROPE_DIM,
    DIL_HEADS * HEAD_DIM, DIL_HEADS * HEAD_DIM, DIL_HEADS * HEAD_DIM,
    GQA_HEADS * HEAD_DIM, GQA_KV_HEADS * HEAD_DIM, GQA_KV_HEADS * HEAD_DIM,
    DIFF_HEADS * 2 * DIFF_QK_DIM, DIFF_HEADS * 2 * DIFF_QK_DIM, DIFF_HEADS * DIFF_V_DIM,
)
IN_COLS = sum(IN_SIZES)

N_GROUPS = 4
EXPERTS_PER_GROUP = 4
N_EXPERTS = N_GROUPS * EXPERTS_PER_GROUP
TOP_K = 2
D_FF_EXPERT = 512

kernel_name = 'hybrid_parallel_head_encoder'


def rms_norm(x, gain=None):
    xf = x.astype(jnp.float32)
    y = xf * lax.rsqrt(jnp.mean(xf * xf, axis=-1, keepdims=True) + NORM_EPS)
    if gain is not None:
        y = y * gain.astype(jnp.float32)
    return y.astype(x.dtype)


def rope_cos_sin(pos, dim):
    inv = 1.0 / (ROPE_THETA ** (jnp.arange(0, dim, 2, dtype=jnp.float32) / dim))
    ang = pos.astype(jnp.float32)[:, None] * inv[None, :]
    ang = jnp.concatenate([ang, ang], axis=-1)
    return jnp.cos(ang), jnp.sin(ang)


def apply_rope(x, cos, sin):
    half = x.shape[-1] // 2
    xf = x.astype(jnp.float32)
    rot = jnp.concatenate([-xf[..., half:], xf[..., :half]], axis=-1)
    return (xf * cos[None, :, None, :] + rot * sin[None, :, None, :]).astype(x.dtype)


def axial_rope(x, row_cos, row_sin, col_cos, col_sin):
    return jnp.concatenate([apply_rope(x[..., :AXIAL_DIM], row_cos, row_sin),
                            apply_rope(x[..., AXIAL_DIM:], col_cos, col_sin)], axis=-1)


def rel_bucket(rel):
    half = NUM_BUCKETS // 2
    max_exact = half // 2
    n = jnp.abs(rel)
    nf = jnp.maximum(n, 1).astype(jnp.float32)
    log_ratio = jnp.log(nf / max_exact) / math.log(REL_MAX_DISTANCE / max_exact)
    large = jnp.minimum(max_exact + (log_ratio * (half - max_exact)).astype(jnp.int32), half - 1)
    return jnp.where(rel > 0, half, 0) + jnp.where(n < max_exact, n, large)


def sweep_query_blocks(block_fn, q):
    b, s = q.shape[:2]
    nb = s // Q_BLOCK
    qb = jnp.moveaxis(q.reshape((b, nb, Q_BLOCK) + q.shape[2:]), 1, 0)
    starts = jnp.arange(nb, dtype=jnp.int32) * Q_BLOCK
    out = lax.map(lambda a: block_fn(a[0], a[1]), (starts, qb))
    out = jnp.moveaxis(out, 0, 1)
    return out.reshape((b, s) + out.shape[3:])


def mla_mixer(c_q, c_kv, k_rope, q_norm_g, kv_norm_g, w_uq, w_ukv, qk_g, cos, sin):
    b, s, _ = c_q.shape
    q = jnp.einsum('bsr,rc->bsc', rms_norm(c_q, q_norm_g), w_uq).reshape(b, s, MLA_HEADS, MLA_QK_DIM)
    kv = jnp.einsum('bsr,rc->bsc', rms_norm(c_kv, kv_norm_g), w_ukv)
    kv = kv.reshape(b, s, MLA_HEADS, MLA_NOPE_DIM + MLA_V_DIM)
    k_nope, v = kv[..., :MLA_NOPE_DIM], kv[..., MLA_NOPE_DIM:]
    k_pe = jnp.broadcast_to(k_rope[:, :, None, :], (b, s, MLA_HEADS, MLA_ROPE_DIM))
    k = jnp.concatenate([k_nope, k_pe], axis=-1)
    q = rms_norm(q, qk_g[0])
    k = rms_norm(k, qk_g[1])
    q = jnp.concatenate([q[..., :MLA_NOPE_DIM], apply_rope(q[..., MLA_NOPE_DIM:], cos, sin)], axis=-1)
    k = jnp.concatenate([k[..., :MLA_NOPE_DIM], apply_rope(k[..., MLA_NOPE_DIM:], cos, sin)], axis=-1)
    scale = MLA_QK_DIM ** -0.5

    def block(start, qb):
        sc = jnp.einsum('bqhe,bkhe->bhqk', qb, k).astype(jnp.float32) * scale
        p = jax.nn.softmax(sc, axis=-1).astype(v.dtype)
        return jnp.einsum('bhqk,bkhe->bqhe', p, v)

    return sweep_query_blocks(block, q).reshape(b, s, MLA_HEADS * MLA_V_DIM)


def dilated_branch(q, k, v, bias_table, dilation, half_steps):
    b, s, h, e = q.shape
    n_sub = s // dilation
    nb = -(-n_sub // DIL_BLOCK)
    lp = nb * DIL_BLOCK

    def to_sub(a):
        a = a.reshape(b, n_sub, dilation, h, e).transpose(0, 2, 3, 1, 4)
        return jnp.pad(a, ((0, 0), (0, 0), (0, 0), (0, lp - n_sub), (0, 0)))

    def band(a):
        a = jnp.pad(a, ((0, 0), (0, 0), (0, 0), (DIL_BLOCK, DIL_BLOCK), (0, 0)))
        a = a.reshape(b, dilation, h, nb + 2, DIL_BLOCK, e)
        return jnp.concatenate([a[:, :, :, :-2], a[:, :, :, 1:-1], a[:, :, :, 2:]], axis=4)

    qb = to_sub(q).reshape(b, dilation, h, nb, DIL_BLOCK, e)
    kb = band(to_sub(k))
    vb = band(to_sub(v))
    scores = jnp.einsum('brhnqe,brhnke->brhnqk', qb, kb).astype(jnp.float32) * (e ** -0.5)
    qi = jnp.arange(DIL_BLOCK, dtype=jnp.int32)
    kj = jnp.arange(3 * DIL_BLOCK, dtype=jnp.int32) - DIL_BLOCK
    rel = kj[None, :] - qi[:, None]
    k_sub = (jnp.arange(nb, dtype=jnp.int32) * DIL_BLOCK)[:, None] + kj[None, :]
    valid = (jnp.abs(rel) <= half_steps)[None] & ((k_sub >= 0) & (k_sub < n_sub))[:, None, :]
    bias = jnp.transpose(bias_table[rel_bucket(rel * dilation)], (2, 0, 1))
    scores = jnp.where(valid, scores + bias[:, None], NEG_INF)
    lse = jax.nn.logsumexp(scores, axis=-1)
    p = jnp.exp(scores - lse[..., None])
    o = jnp.einsum('brhnqk,brhnke->brhnqe', p.astype(v.dtype), vb)
    o = o.reshape(b, dilation, h, lp, e)[:, :, :, :n_sub].transpose(0, 3, 1, 2, 4).reshape(b, s, h, e)
    lse = lse.reshape(b, dilation, h, lp)[..., :n_sub].transpose(0, 3, 1, 2).reshape(b, s, h)
    return o, lse


def dilated_mixer(q, k, v, qk_g, bias_table):
    b, s, _ = q.shape
    q = rms_norm(q.reshape(b, s, DIL_HEADS, HEAD_DIM), qk_g[0])
    k = rms_norm(k.reshape(b, s, DIL_HEADS, HEAD_DIM), qk_g[1])
    v = v.reshape(b, s, DIL_HEADS, HEAD_DIM)
    outs, lses = [], []
    for window, dilation in DIL_PATTERNS:
        o, l = dilated_branch(q, k, v, bias_table, dilation, window // (2 * dilation))
        outs.append(o)
        lses.append(l)
    wts = jax.nn.softmax(jnp.stack(lses, axis=0), axis=0).astype(q.dtype)
    o = jnp.sum(wts[..., None] * jnp.stack(outs, axis=0), axis=0)
    return o.reshape(b, s, DIL_HEADS * HEAD_DIM)


def gqa_mixer(q, k, v, qk_g, row_cos, row_sin, col_cos, col_sin):
    b, s, _ = q.shape
    q = rms_norm(q.reshape(b, s, GQA_HEADS, HEAD_DIM), qk_g[0])
    k = rms_norm(k.reshape(b, s, GQA_KV_HEADS, HEAD_DIM), qk_g[1])
    v = v.reshape(b, s, GQA_KV_HEADS, HEAD_DIM)
    q = axial_rope(q, row_cos, row_sin, col_cos, col_sin).reshape(b, s, GQA_KV_HEADS, GQA_GROUP, HEAD_DIM)
    k = axial_rope(k, row_cos, row_sin, col_cos, col_sin)
    scale = HEAD_DIM ** -0.5

    def block(start, qb):
        sc = jnp.einsum('bqhge,bkhe->bhgqk', qb, k).astype(jnp.float32) * scale
        p = jax.nn.softmax(sc, axis=-1).astype(v.dtype)
        return jnp.einsum('bhgqk,bkhe->bqhge', p, v)

    return sweep_query_blocks(block, q).reshape(b, s, GQA_HEADS * HEAD_DIM)


def diff_mixer(q, k, v, qk_g, lam_vecs, subln_g, bias_table, lambda_init):
    b, s, _ = q.shape
    q = rms_norm(q.reshape(b, s, DIFF_HEADS, 2, DIFF_QK_DIM), qk_g[0])
    k = rms_norm(k.reshape(b, s, DIFF_HEADS, 2, DIFF_QK_DIM), qk_g[1])
    v = v.reshape(b, s, DIFF_HEADS, DIFF_V_DIM)
    lv = lam_vecs.astype(jnp.float32)
    lam = jnp.exp(jnp.sum(lv[0] * lv[1])) - jnp.exp(jnp.sum(lv[2] * lv[3])) + lambda_init
    kpos = jnp.arange(s, dtype=jnp.int32)
    scale = DIFF_QK_DIM ** -0.5

    def block(start, qb):
        sc = jnp.einsum('bqhce,bkhce->bchqk', qb, k).astype(jnp.float32) * scale
        qpos = start + jnp.arange(Q_BLOCK, dtype=jnp.int32)
        bias = jnp.transpose(bias_table[rel_bucket(kpos[None, :] - qpos[:, None])], (2, 0, 1))
        p = jax.nn.softmax(sc + bias[None, None], axis=-1)
        attn = p[:, 0] - lam * p[:, 1]
        return jnp.einsum('bhqk,bkhe->bqhe', attn.astype(v.dtype), v)

    o = sweep_query_blocks(block, q)
    o = rms_norm(o, subln_g) * (1.0 - lambda_init)
    return o.reshape(b, s, DIFF_HEADS * DIFF_V_DIM)


def hier_moe(h, wg, bg, we, be, w_gate, w_up, w_down):
    b, s, d = h.shape
    t = h.reshape(b * s, d)
    n_tok = t.shape[0]
    g_prob = jax.nn.softmax(jnp.einsum('td,dg->tg', t, wg).astype(jnp.float32) + bg.astype(jnp.float32), axis=-1)
    g_w, g_idx = lax.top_k(g_prob, 1)
    e_logits = jnp.einsum('td,de->te', t, we).astype(jnp.float32) + be.astype(jnp.float32)
    e_logits = e_logits.reshape(n_tok, N_GROUPS, EXPERTS_PER_GROUP)
    sel = jnp.broadcast_to(g_idx[:, :, None], (n_tok, 1, EXPERTS_PER_GROUP))
    e_logits = jnp.take_along_axis(e_logits, sel, axis=1)[:, 0]
    e_w, e_idx = lax.top_k(jax.nn.softmax(e_logits, axis=-1), TOP_K)
    e_w = e_w / jnp.sum(e_w, axis=-1, keepdims=True)
    expert_id = g_idx * EXPERTS_PER_GROUP + e_idx
    gate = jnp.sum(jax.nn.one_hot(expert_id, N_EXPERTS, dtype=jnp.float32) * (g_w * e_w)[..., None], axis=1)
    y = jnp.zeros_like(t)
    for e in range(N_EXPERTS):
        hid = jax.nn.silu(t @ w_gate[e]) * (t @ w_up[e])
        y = y + gate[:, e:e + 1].astype(t.dtype) * (hid @ w_down[e])
    return y.reshape(b, s, d)


def setup_inputs(seed: int = 0) -> dict:
    key = jax.random.key(seed)
    ks = jax.random.split(key, 24)
    f32 = jnp.float32

    def nrm(k, shape, scale):
        return jax.random.normal(k, shape, f32) * scale

    def gain(k, shape):
        return 1.0 + 0.02 * jax.random.normal(k, shape, f32)

    return {
        'x': nrm(ks[0], (BATCH, SEQ, D_MODEL), 1.0),
        'rel_bias': nrm(ks[1], (NUM_BUCKETS, NUM_BIAS_HEADS), 0.5),
        'norm1_g': gain(ks[2], (DEPTH, D_MODEL)),
        'w_in': nrm(ks[3], (DEPTH, D_MODEL, IN_COLS), D_MODEL ** -0.5),
        'mla_q_norm_g': gain(ks[4], (DEPTH, MLA_Q_RANK)),
        'mla_kv_norm_g': gain(ks[5], (DEPTH, MLA_KV_RANK)),
        'mla_w_uq': nrm(ks[6], (DEPTH, MLA_Q_RANK, MLA_HEADS * MLA_QK_DIM), MLA_Q_RANK ** -0.5),
        'mla_w_ukv': nrm(ks[7], (DEPTH, MLA_KV_RANK, MLA_HEADS * (MLA_NOPE_DIM + MLA_V_DIM)), MLA_KV_RANK ** -0.5),
        'mla_qk_g': gain(ks[8], (DEPTH, 2, MLA_QK_DIM)),
        'dil_qk_g': gain(ks[9], (DEPTH, 2, HEAD_DIM)),
        'gqa_qk_g': gain(ks[10], (DEPTH, 2, HEAD_DIM)),
        'diff_qk_g': gain(ks[11], (DEPTH, 2, DIFF_QK_DIM)),
        'diff_lambda': nrm(ks[12], (DEPTH, 4, DIFF_QK_DIM), 0.1),
        'diff_subln_g': gain(ks[13], (DEPTH, DIFF_V_DIM)),
        'mix_beta': gain(ks[14], (DEPTH, D_MIX)),
        'w_out': nrm(ks[15], (DEPTH, D_MIX, D_MODEL), D_MIX ** -0.5),
        'norm2_g': gain(ks[16], (DEPTH, D_MODEL)),
        'router_group_w': nrm(ks[17], (DEPTH, D_MODEL, N_GROUPS), D_MODEL ** -0.5),
        'router_group_b': nrm(ks[18], (DEPTH, N_GROUPS), 0.01),
        'router_expert_w': nrm(ks[19], (DEPTH, D_MODEL, N_EXPERTS), D_MODEL ** -0.5),
        'router_expert_b': nrm(ks[20], (DEPTH, N_EXPERTS), 0.01),
        'expert_w_gate': nrm(ks[21], (DEPTH, N_EXPERTS, D_MODEL, D_FF_EXPERT), D_MODEL ** -0.5),
        'expert_w_up': nrm(ks[22], (DEPTH, N_EXPERTS, D_MODEL, D_FF_EXPERT), D_MODEL ** -0.5),
        'expert_w_down': nrm(ks[23], (DEPTH, N_EXPERTS, D_FF_EXPERT, D_MODEL), D_FF_EXPERT ** -0.5),
    }


def reference(x, rel_bias, norm1_g, w_in, mla_q_norm_g, mla_kv_norm_g, mla_w_uq, mla_w_ukv, mla_qk_g,
              dil_qk_g, gqa_qk_g, diff_qk_g, diff_lambda, diff_subln_g, mix_beta, w_out, norm2_g,
              router_group_w, router_group_b, router_expert_w, router_expert_b,
              expert_w_gate, expert_w_up, expert_w_down):
    b, s, _ = x.shape
    rows = s // GRID_W
    pos = jnp.arange(s, dtype=jnp.int32)
    row_pos = jnp.repeat(jnp.arange(rows, dtype=jnp.int32), GRID_W)
    col_pos = pos - row_pos * GRID_W
    mla_cos, mla_sin = rope_cos_sin(pos, MLA_ROPE_DIM)
    row_cos, row_sin = rope_cos_sin(row_pos, AXIAL_DIM)
    col_cos, col_sin = rope_cos_sin(col_pos, AXIAL_DIM)
    split_points = np.cumsum(IN_SIZES)[:-1].tolist()
    bias_b = rel_bias[:, :DIL_HEADS]
    bias_d = rel_bias[:, DIL_HEADS:]

    for layer in range(DEPTH):
        h = rms_norm(x, norm1_g[layer])
        proj = jnp.einsum('bsd,dc->bsc', h, w_in[layer])
        (a_cq, a_ckv, a_kr, b_q, b_k, b_v, c_q, c_k, c_v, d_q, d_k, d_v) = jnp.split(proj, split_points, axis=-1)
        y_a = mla_mixer(a_cq, a_ckv, a_kr, mla_q_norm_g[layer], mla_kv_norm_g[layer], mla_w_uq[layer],
                        mla_w_ukv[layer], mla_qk_g[layer], mla_cos, mla_sin)
        y_b = dilated_mixer(b_q, b_k, b_v, dil_qk_g[layer], bias_b)
        y_c = gqa_mixer(c_q, c_k, c_v, gqa_qk_g[layer], row_cos, row_sin, col_cos, col_sin)
        lambda_init = 0.8 - 0.6 * math.exp(-0.3 * layer)
        y_d = diff_mixer(d_q, d_k, d_v, diff_qk_g[layer], diff_lambda[layer], diff_subln_g[layer], bias_d, lambda_init)
        mixed = jnp.concatenate([rms_norm(y_a), rms_norm(y_b), rms_norm(y_c), y_d], axis=-1) * mix_beta[layer]
        x = x + jnp.einsum('bsm,md->bsd', mixed, w_out[layer])
        x = x + hier_moe(rms_norm(x, norm2_g[layer]), router_group_w[layer], router_group_b[layer],
                         router_expert_w[layer], router_expert_b[layer], expert_w_gate[layer],
                         expert_w_up[layer], expert_w_down[layer])
    return x
```

```python
import functools
import math

import jax
import jax.numpy as jnp
import numpy as np
from jax import lax
from jax.experimental import pallas as pl
from jax.experimental.pallas import tpu as pltpu

F32 = jnp.float32
BF16 = jnp.bfloat16

D_MODEL = 1024
SEQ = 2048
HEAD_DIM = 64
GRID_W = 64
ROPE_THETA = 10000.0
NORM_EPS = 1e-6
NEG_INF = -1e30
NUM_BUCKETS = 32
REL_MAX_DISTANCE = 1024

N_HEADS = 4
MLA_NOPE_DIM = 64
MLA_ROPE_DIM = 32
MLA_QK_DIM = MLA_NOPE_DIM + MLA_ROPE_DIM
MLA_Q_RANK = 256
MLA_KV_RANK = 128
DIL_PATTERNS = ((128, 1), (512, 4), (2048, 16))
DIL_HALF = 64
DIFF_QK_DIM = 32
N_EXPERTS = 16
EXPERTS_PER_GROUP = 4
D_FF = 512

LANES = 128
MXU_DIM = 256
VMEM_LIMIT_BYTES = 56 * 1024 * 1024

P_CQ, P_CKV, P_KR = 0, 256, 384
P_BQ, P_BK, P_BV = 512, 768, 1024
P_CQ2, P_CK2, P_CV2 = 1280, 1536, 1664
P_DQ, P_DK, P_DV = 1792, 2048, 2304
PROJ_COLS = 2560

A_QA, A_KA, A_VA = 0, 512, 1024
A_QB, A_KB, A_VB = 1280, 1536, 1792
A_QC, A_KC, A_VC = 2048, 2304, 2432
A_QD0, A_QD1, A_KD, A_VD = 2560, 2816, 3072, 3328
ACT_COLS = 3584

(G_NORM1, G_MLA_QN, G_MLA_KVN, G_MLA_Q, G_MLA_K, G_DIL_Q, G_DIL_K, G_GQA_Q, G_GQA_K,
 G_DIFF_Q0, G_DIFF_Q1, G_DIFF_K) = range(12)
GAIN_ROWS = 16

TM_PROJ = 256
TQ = 256
TM_OUT = 256
TM_MOE = 512
DIL_QBLK = 128


def _bf(x):
    return x.astype(BF16)


def _dot(a, b):
    return jnp.dot(a, b, preferred_element_type=F32)


def _dot_nt(a, b):
    return lax.dot_general(a, b, (((1,), (1,)), ((), ())), preferred_element_type=F32)


def _rms(x, width):
    return x * lax.rsqrt(jnp.sum(x * x, axis=-1, keepdims=True) * (1.0 / width) + NORM_EPS)


def _group_sumsq(x, g):
    x2 = x * x
    hi = _bf(x2)
    lo = _bf(x2 - hi.astype(F32))
    w = g.shape[0]
    outs = []
    for c in range(x.shape[1] // w):
        sl = slice(w * c, w * (c + 1))
        outs.append(_dot(hi[:, sl], g) + _dot(lo[:, sl], g))
    return outs[0] if len(outs) == 1 else jnp.concatenate(outs, axis=1)


def _group_rms(x, g, group):
    return x * lax.rsqrt(_group_sumsq(x, g) * (1.0 / group) + NORM_EPS)


def _rope(x, c, s_next, s_prev, half):
    n = x.shape[1]
    return x * c + pltpu.roll(x, n - half, axis=1) * s_next + pltpu.roll(x, half, axis=1) * s_prev


def _inproj_kernel(x_ref, gv_ref, w_ref, wuq_ref, wuk_ref, wuv_ref, rope_ref, g32_ref, g64_ref, g128_ref, o_ref):
    def gain(row, width):
        return gv_ref[row:row + 1, 0:width]

    x = x_ref[...]
    h = _rms(x, D_MODEL) * gain(G_NORM1, D_MODEL)
    proj = _dot(_bf(h), w_ref[...])

    g32 = g32_ref[...]
    g64 = g64_ref[...]
    g128 = g128_ref[...]
    m_c = rope_ref[:, 0:128]
    m_sn = rope_ref[:, 128:256]
    m_sp = rope_ref[:, 256:384]
    a_c = rope_ref[:, 384:512]
    a_sn = rope_ref[:, 512:640]
    a_sp = rope_ref[:, 640:768]

    cq = _rms(proj[:, P_CQ:P_CQ + MLA_Q_RANK], MLA_Q_RANK) * gain(G_MLA_QN, MLA_Q_RANK)
    ckv = _rms(proj[:, P_CKV:P_CKV + MLA_KV_RANK], MLA_KV_RANK) * gain(G_MLA_KVN, MLA_KV_RANK)
    q = _dot(_bf(cq), wuq_ref[...])
    k_nope = _dot(_bf(ckv), wuk_ref[...])
    v = _dot(_bf(ckv), wuv_ref[...])
    k_rope = pltpu.roll(proj[:, P_KR:P_KR + LANES], MLA_NOPE_DIM, axis=1)
    k = k_nope + jnp.concatenate([k_rope] * N_HEADS, axis=1)
    qn = _group_rms(q, g128, MLA_QK_DIM) * gain(G_MLA_Q, 512)
    kn = _group_rms(k, g128, MLA_QK_DIM) * gain(G_MLA_K, 512)
    for g in range(N_HEADS):
        sl = slice(LANES * g, LANES * (g + 1))
        o_ref[:, A_QA + LANES * g:A_QA + LANES * (g + 1)] = _bf(_rope(qn[:, sl], m_c, m_sn, m_sp, MLA_ROPE_DIM // 2))
        o_ref[:, A_KA + LANES * g:A_KA + LANES * (g + 1)] = _bf(_rope(kn[:, sl], m_c, m_sn, m_sp, MLA_ROPE_DIM // 2))
    o_ref[:, A_VA:A_VA + 256] = _bf(v)

    o_ref[:, A_QB:A_QB + 256] = _bf(_group_rms(proj[:, P_BQ:P_BQ + 256], g64, HEAD_DIM) * gain(G_DIL_Q, 256))
    o_ref[:, A_KB:A_KB + 256] = _bf(_group_rms(proj[:, P_BK:P_BK + 256], g64, HEAD_DIM) * gain(G_DIL_K, 256))
    o_ref[:, A_VB:A_VB + 256] = _bf(proj[:, P_BV:P_BV + 256])

    qc = _group_rms(proj[:, P_CQ2:P_CQ2 + 256], g64, HEAD_DIM) * gain(G_GQA_Q, 256)
    for g in range(2):
        sl = slice(LANES * g, LANES * (g + 1))
        o_ref[:, A_QC + LANES * g:A_QC + LANES * (g + 1)] = _bf(_rope(qc[:, sl], a_c, a_sn, a_sp, HEAD_DIM // 4))
    kc = _group_rms(proj[:, P_CK2:P_CK2 + 128], g64[0:128, 0:128], HEAD_DIM) * gain(G_GQA_K, 128)
    o_ref[:, A_KC:A_KC + 128] = _bf(_rope(kc, a_c, a_sn, a_sp, HEAD_DIM // 4))
    o_ref[:, A_VC:A_VC + 128] = _bf(proj[:, P_CV2:P_CV2 + 128])

    dqn = _group_rms(proj[:, P_DQ:P_DQ + 256], g32, DIFF_QK_DIM)
    o_ref[:, A_QD0:A_QD0 + 256] = _bf(dqn * gain(G_DIFF_Q0, 256))
    o_ref[:, A_QD1:A_QD1 + 256] = _bf(dqn * gain(G_DIFF_Q1, 256))
    o_ref[:, A_KD:A_KD + 256] = _bf(_group_rms(proj[:, P_DK:P_DK + 256], g32, DIFF_QK_DIM) * gain(G_DIFF_K, 256))
    o_ref[:, A_VD:A_VD + 256] = _bf(proj[:, P_DV:P_DV + 256])


def _inproj(x2d, gv, w_in, wuq, wuk, wuv, rope, g32, g64, g128):
    t = x2d.shape[0]
    tm = TM_PROJ
    n_pos = SEQ // tm
    const = lambda i: (0, 0)
    return pl.pallas_call(
        _inproj_kernel,
        grid=(t // tm,),
        in_specs=[
            pl.BlockSpec((tm, D_MODEL), lambda i: (i, 0)),
            pl.BlockSpec((GAIN_ROWS, D_MODEL), const),
            pl.BlockSpec((D_MODEL, PROJ_COLS), const),
            pl.BlockSpec((MLA_Q_RANK, 512), const),
            pl.BlockSpec((MLA_KV_RANK, 512), const),
            pl.BlockSpec((MLA_KV_RANK, 256), const),
            pl.BlockSpec((tm, 768), lambda i: (i % n_pos, 0)),
            pl.BlockSpec((MXU_DIM, MXU_DIM), const),
            pl.BlockSpec((MXU_DIM, MXU_DIM), const),
            pl.BlockSpec((MXU_DIM, MXU_DIM), const),
        ],
        out_specs=pl.BlockSpec((tm, ACT_COLS), lambda i: (i, 0)),
        out_shape=jax.ShapeDtypeStruct((t, ACT_COLS), BF16),
        compiler_params=pltpu.CompilerParams(dimension_semantics=("parallel",), vmem_limit_bytes=VMEM_LIMIT_BYTES),
        name="inproj_prep",
    )(x2d, gv, w_in, wuq, wuk, wuv, rope, g32, g64, g128)


def _softmax_pv(s, v):
    m = jnp.max(s, axis=-1, keepdims=True)
    e = jnp.exp(s - m)
    l = jnp.sum(e, axis=-1, keepdims=True)
    return _dot(_bf(e), v) / l


def _attn_mla_kernel(q_ref, k_ref, v_ref, o_ref):
    for h in range(N_HEADS):
        sl = slice(LANES * h, LANES * (h + 1))
        s = _dot_nt(q_ref[:, sl], k_ref[:, sl])
        o_ref[:, HEAD_DIM * h:HEAD_DIM * (h + 1)] = _softmax_pv(s, v_ref[:, HEAD_DIM * h:HEAD_DIM * (h + 1)])


def _attn_gqa_kernel(q_ref, k_ref, v_ref, o_ref):
    for h in range(N_HEADS):
        g = h // 2
        hs = slice(HEAD_DIM * h, HEAD_DIM * (h + 1))
        gs = slice(HEAD_DIM * g, HEAD_DIM * (g + 1))
        s = _dot_nt(q_ref[:, hs], k_ref[:, gs])
        o_ref[:, hs] = _softmax_pv(s, v_ref[:, gs])


def _attn_full(act, batch, kernel, q_col, q_w, k_col, k_w, v_col, v_w, name):
    t = act.shape[0]
    nq = SEQ // TQ
    return pl.pallas_call(
        kernel,
        grid=(batch, nq),
        in_specs=[
            pl.BlockSpec((TQ, q_w), lambda b, i: (b * nq + i, q_col // q_w)),
            pl.BlockSpec((SEQ, k_w), lambda b, i: (b, k_col // k_w)),
            pl.BlockSpec((SEQ, v_w), lambda b, i: (b, v_col // v_w)),
        ],
        out_specs=pl.BlockSpec((TQ, 256), lambda b, i: (b * nq + i, 0)),
        out_shape=jax.ShapeDtypeStruct((t, 256), F32),
        compiler_params=pltpu.CompilerParams(
            dimension_semantics=("parallel", "parallel"), vmem_limit_bytes=VMEM_LIMIT_BYTES),
        name=name,
    )(act, act, act)


def _attn_diff_kernel(lam_ref, q0_ref, q1_ref, k_ref, v_ref, bias_ref, sg_ref, o_ref):
    lam = lam_ref[0]
    for h in range(N_HEADS):
        hs = slice(HEAD_DIM * h, HEAD_DIM * (h + 1))
        k = k_ref[:, hs]
        bias = bias_ref[h]
        s0 = _dot_nt(q0_ref[:, hs], k) + bias
        s1 = _dot_nt(q1_ref[:, hs], k) + bias
        e0 = jnp.exp(s0 - jnp.max(s0, axis=-1, keepdims=True))
        e1 = jnp.exp(s1 - jnp.max(s1, axis=-1, keepdims=True))
        r0 = 1.0 / jnp.sum(e0, axis=-1, keepdims=True)
        r1 = lam / jnp.sum(e1, axis=-1, keepdims=True)
        attn = e0 * r0 - e1 * r1
        o = _dot(_bf(attn), v_ref[:, hs])
        o_ref[:, hs] = _rms(o, HEAD_DIM) * sg_ref[:, hs]


def _attn_diff(act, batch, lam, bias, sub_gain):
    t = act.shape[0]
    nq = SEQ // TQ
    return pl.pallas_call(
        _attn_diff_kernel,
        grid=(nq, batch),
        in_specs=[
            pl.BlockSpec(memory_space=pltpu.SMEM),
            pl.BlockSpec((TQ, 256), lambda i, b: (b * nq + i, A_QD0 // 256)),
            pl.BlockSpec((TQ, 256), lambda i, b: (b * nq + i, A_QD1 // 256)),
            pl.BlockSpec((SEQ, 256), lambda i, b: (b, A_KD // 256)),
            pl.BlockSpec((SEQ, 256), lambda i, b: (b, A_VD // 256)),
            pl.BlockSpec((N_HEADS, TQ, SEQ), lambda i, b: (0, i, 0)),
            pl.BlockSpec((1, 256), lambda i, b: (0, 0)),
        ],
        out_specs=pl.BlockSpec((TQ, 256), lambda i, b: (b * nq + i, 0)),
        out_shape=jax.ShapeDtypeStruct((t, 256), F32),
        compiler_params=pltpu.CompilerParams(
            dimension_semantics=("parallel", "parallel"), vmem_limit_bytes=VMEM_LIMIT_BYTES),
        name="attn_diff",
    )(lam, act, act, act, act, bias, sub_gain)


def _dil_kernel(n_sub, final, q_ref, k_ref, v_ref, bias_ref, *rest):
    if final:
        o1_ref, l1_ref, o4_ref, l4_ref, o_ref, kpad, vpad = rest
    else:
        o_ref, lse_ref, kpad, vpad = rest
    zeros = jnp.zeros((DIL_HALF, 256), BF16)
    kpad[0:DIL_HALF, :] = zeros
    vpad[0:DIL_HALF, :] = zeros
    kpad[DIL_HALF:DIL_HALF + n_sub, :] = k_ref[...]
    vpad[DIL_HALF:DIL_HALF + n_sub, :] = v_ref[...]
    kpad[DIL_HALF + n_sub:2 * DIL_HALF + n_sub, :] = zeros
    vpad[DIL_HALF + n_sub:2 * DIL_HALF + n_sub, :] = zeros
    col = lax.broadcasted_iota(jnp.int32, (1, 2 * DIL_QBLK), 1) - DIL_HALF

    def block(j):
        r0 = pl.multiple_of(j * DIL_QBLK, DIL_QBLK)
        rows = pl.ds(r0, DIL_QBLK)
        win = pl.ds(r0, 2 * DIL_QBLK)
        key_pos = col + j * DIL_QBLK
        in_seq = jnp.logical_and(key_pos >= 0, key_pos < n_sub)
        for h in range(N_HEADS):
            hs = slice(HEAD_DIM * h, HEAD_DIM * (h + 1))
            s = _dot_nt(q_ref[rows, hs], kpad[win, hs]) + bias_ref[h]
            s = jnp.where(in_seq, s, NEG_INF)
            m = jnp.max(s, axis=-1, keepdims=True)
            e = jnp.exp(s - m)
            l = jnp.sum(e, axis=-1, keepdims=True)
            o = _dot(_bf(e), vpad[win, hs]) / l
            lse = jnp.broadcast_to(m + jnp.log(l), (DIL_QBLK, HEAD_DIM))
            if final:
                la, lb = l1_ref[rows, hs], l4_ref[rows, hs]
                mx = jnp.maximum(jnp.maximum(la, lb), lse)
                wa, wb, wc = jnp.exp(la - mx), jnp.exp(lb - mx), jnp.exp(lse - mx)
                den = wa + wb + wc
                o_ref[rows, hs] = (wa / den) * o1_ref[rows, hs] + (wb / den) * o4_ref[rows, hs] + (wc / den) * o
            else:
                o_ref[rows, hs] = o
                lse_ref[rows, hs] = lse

    n_blk = n_sub // DIL_QBLK
    if n_blk == 1:
        block(0)
    else:
        def body(j, carry):
            block(j)
            return carry
        lax.fori_loop(0, n_blk, body, 0)


def _attn_dilated(act, batch, bias_tabs):
    t = act.shape[0]
    prev = []
    for (_, d), bias in zip(DIL_PATTERNS, bias_tabs):
        n_sub = SEQ // d
        final = d == DIL_PATTERNS[-1][1]
        act_v = act.reshape(t // d, d * ACT_COLS)
        cpr = ACT_COLS // 256
        blk = lambda c: pl.BlockSpec((n_sub, 256), lambda b, r, c=c: (b, r * cpr + c))
        oblk = pl.BlockSpec((n_sub, 256), lambda b, r: (b, r))
        in_specs = [blk(A_QB // 256), blk(A_KB // 256), blk(A_VB // 256),
                    pl.BlockSpec((N_HEADS, DIL_QBLK, 2 * DIL_QBLK), lambda b, r: (0, 0, 0))]
        args = [act_v, act_v, act_v, bias]
        o_sds = jax.ShapeDtypeStruct((t // d, d * 256), F32)
        if final:
            in_specs += [oblk] * 4
            args += [a.reshape(t // d, d * 256) for a in prev]
            out_specs, out_shape = oblk, o_sds
        else:
            out_specs, out_shape = [oblk, oblk], [o_sds, o_sds]
        res = pl.pallas_call(
            functools.partial(_dil_kernel, n_sub, final),
            grid=(batch, d),
            in_specs=in_specs,
            out_specs=out_specs,
            out_shape=out_shape,
            scratch_shapes=[pltpu.VMEM((n_sub + 2 * DIL_HALF, 256), BF16)] * 2,
            compiler_params=pltpu.CompilerParams(
                dimension_semantics=("parallel", "parallel"), vmem_limit_bytes=VMEM_LIMIT_BYTES),
            name=f"attn_dil{d}",
        )(*args)
        if final:
            return res.reshape(t, 256)
        prev += [res[0].reshape(t, 256), res[1].reshape(t, 256)]


def _route(lt):
    g = [lt[i:i + 1, :] for i in range(4)]
    gmax = jnp.maximum(jnp.maximum(g[0], g[1]), jnp.maximum(g[2], g[3]))
    gsum = sum(jnp.exp(gi - gmax) for gi in g)
    g_w = 1.0 / gsum
    gidx = jnp.where(g[0] == gmax, 0, jnp.where(g[1] == gmax, 1, jnp.where(g[2] == gmax, 2, 3)))
    el = []
    for j in range(EXPERTS_PER_GROUP):
        acc = jnp.zeros_like(g[0])
        for i in range(4):
            r = 4 + EXPERTS_PER_GROUP * i + j
            acc = jnp.where(gidx == i, lt[r:r + 1, :], acc)
        el.append(acc)
    emax = jnp.maximum(jnp.maximum(el[0], el[1]), jnp.maximum(el[2], el[3]))
    ee = [jnp.exp(e - emax) for e in el]
    esum = ee[0] + ee[1] + ee[2] + ee[3]
    p = [e / esum for e in ee]
    p1 = jnp.maximum(jnp.maximum(p[0], p[1]), jnp.maximum(p[2], p[3]))
    i1 = jnp.where(p[0] == p1, 0, jnp.where(p[1] == p1, 1, jnp.where(p[2] == p1, 2, 3)))
    pm = [jnp.where(i1 == j, -1.0, p[j]) for j in range(4)]
    p2 = jnp.maximum(jnp.maximum(pm[0], pm[1]), jnp.maximum(pm[2], pm[3]))
    i2 = jnp.where(pm[0] == p2, 0, jnp.where(pm[1] == p2, 1, jnp.where(pm[2] == p2, 2, 3)))
    den = p1 + p2
    return gidx * EXPERTS_PER_GROUP + i1, gidx * EXPERTS_PER_GROUP + i2, g_w * (p1 / den), g_w * (p2 / den)


def _outproj_kernel(ya_ref, yb_ref, yc_ref, yd_ref, x_ref, cv_ref, w_ref, wrh_ref, wrl_ref, rb_ref,
                    x1_ref, h2_ref, gate_ref):
    mixed = jnp.concatenate(
        [_rms(ya_ref[...], 256), _rms(yb_ref[...], 256), _rms(yc_ref[...], 256), yd_ref[...]], axis=1)
    mixed = mixed * cv_ref[0:1, :]
    x1 = x_ref[...] + _dot(_bf(mixed), w_ref[...])
    x1_ref[...] = x1
    h2 = _rms(x1, D_MODEL) * cv_ref[1:2, :]
    h_hi = _bf(h2)
    h2_ref[...] = h_hi
    h_lo = _bf(h2 - h_hi.astype(F32))
    logits = _dot(h_hi, wrh_ref[...]) + _dot(h_hi, wrl_ref[...]) + _dot(h_lo, wrh_ref[...]) + rb_ref[...]
    id1, id2, w1, w2 = _route(logits.T)
    rows = lax.broadcasted_iota(jnp.int32, (LANES, logits.shape[0]), 0)
    gate_t = jnp.where(rows == id1, w1, 0.0) + jnp.where(rows == id2, w2, 0.0)
    gate_ref[...] = gate_t.T


def _outproj(ya, yb, yc, yd, x2d, cv, w_out, wr_hi, wr_lo, rb):
    t = x2d.shape[0]
    tm = TM_OUT
    row = lambda w: pl.BlockSpec((tm, w), lambda i: (i, 0))
    const = lambda shape: pl.BlockSpec(shape, lambda i: (0, 0))
    return pl.pallas_call(
        _outproj_kernel,
        grid=(t // tm,),
        in_specs=[row(256), row(256), row(256), row(256), row(D_MODEL), const((8, D_MODEL)),
                  const((D_MODEL, D_MODEL)), const((D_MODEL, LANES)), const((D_MODEL, LANES)), const((1, LANES))],
        out_specs=[row(D_MODEL), row(D_MODEL), row(LANES)],
        out_shape=[jax.ShapeDtypeStruct((t, D_MODEL), F32), jax.ShapeDtypeStruct((t, D_MODEL), BF16),
                   jax.ShapeDtypeStruct((t, LANES), F32)],
        compiler_params=pltpu.CompilerParams(dimension_semantics=("parallel",), vmem_limit_bytes=VMEM_LIMIT_BYTES),
        name="outproj_router",
    )(ya, yb, yc, yd, x2d, cv, w_out, wr_hi, wr_lo, rb)


def _moe_dense_kernel(h_ref, gate_ref, x1_ref, wg_ref, wu_ref, wd_ref, o_ref):
    e = pl.program_id(1)

    @pl.when(e == 0)
    def _():
        o_ref[...] = x1_ref[...]

    gate = gate_ref[...]
    lane = lax.broadcasted_iota(jnp.int32, gate.shape, 1)
    g = jnp.sum(jnp.where(lane == e, gate, 0.0), axis=-1, keepdims=True)
    h = h_ref[...]
    a = _dot(h, wg_ref[0])
    u = _dot(h, wu_ref[0])
    hid = a * (1.0 / (1.0 + jnp.exp(-a))) * u
    o_ref[...] += g * _dot(_bf(hid), wd_ref[0])


def _moe_dense(h2, gate, x1, wg, wu, wd):
    t = h2.shape[0]
    tm = TM_MOE
    return pl.pallas_call(
        _moe_dense_kernel,
        grid=(t // tm, N_EXPERTS),
        in_specs=[
            pl.BlockSpec((tm, D_MODEL), lambda i, e: (i, 0)),
            pl.BlockSpec((tm, LANES), lambda i, e: (i, 0)),
            pl.BlockSpec((tm, D_MODEL), lambda i, e: (i, 0)),
            pl.BlockSpec((1, D_MODEL, D_FF), lambda i, e: (e, 0, 0)),
            pl.BlockSpec((1, D_MODEL, D_FF), lambda i, e: (e, 0, 0)),
            pl.BlockSpec((1, D_FF, D_MODEL), lambda i, e: (e, 0, 0)),
        ],
        out_specs=pl.BlockSpec((tm, D_MODEL), lambda i, e: (i, 0)),
        out_shape=jax.ShapeDtypeStruct((t, D_MODEL), F32),
        compiler_params=pltpu.CompilerParams(
            dimension_semantics=("parallel", "arbitrary"), vmem_limit_bytes=VMEM_LIMIT_BYTES),
        name="moe_dense",
    )(h2, gate, x1, wg, wu, wd)


def _rope_angles(pos, dim):
    inv = 1.0 / (ROPE_THETA ** (jnp.arange(0, dim, 2, dtype=F32) / dim))
    return pos.astype(F32)[:, None] * inv[None, :]


def _rope_tables():
    pos = jnp.arange(SEQ, dtype=jnp.int32)
    z = lambda w: jnp.zeros((SEQ, w), F32)
    ang = _rope_angles(pos, MLA_ROPE_DIM)
    c, s = jnp.cos(ang), jnp.sin(ang)
    m_c = jnp.concatenate([jnp.ones((SEQ, 64), F32), c, c, z(32)], axis=1)
    m_sn = jnp.concatenate([z(64), -s, z(48)], axis=1)
    m_sp = jnp.concatenate([z(80), s, z(32)], axis=1)
    row_pos = pos // GRID_W
    col_pos = pos - row_pos * GRID_W
    ra, ca = _rope_angles(row_pos, HEAD_DIM // 2), _rope_angles(col_pos, HEAD_DIM // 2)
    rc, rs, cc, cs = jnp.cos(ra), jnp.sin(ra), jnp.cos(ca), jnp.sin(ca)
    a_c = jnp.concatenate([rc, rc, cc, cc] * 2, axis=1)
    a_sn = jnp.concatenate([-rs, z(16), -cs, z(16)] * 2, axis=1)
    a_sp = jnp.concatenate([z(16), rs, z(16), cs] * 2, axis=1)
    return jnp.concatenate([m_c, m_sn, m_sp, a_c, a_sn, a_sp], axis=1)


def _rel_bucket(rel):
    half = NUM_BUCKETS // 2
    max_exact = half // 2
    n = jnp.abs(rel)
    nf = jnp.maximum(n, 1).astype(F32)
    log_ratio = jnp.log(nf / max_exact) / math.log(REL_MAX_DISTANCE / max_exact)
    large = jnp.minimum(max_exact + (log_ratio * (half - max_exact)).astype(jnp.int32), half - 1)
    return jnp.where(rel > 0, half, 0) + jnp.where(n < max_exact, n, large)


def _dil_bias(table, dilation):
    qi = jnp.arange(DIL_QBLK, dtype=jnp.int32)
    kj = jnp.arange(2 * DIL_QBLK, dtype=jnp.int32) - DIL_HALF
    rel = kj[None, :] - qi[:, None]
    bias = jnp.transpose(table[_rel_bucket(rel * dilation)], (2, 0, 1))
    return jnp.where((jnp.abs(rel) <= DIL_HALF)[None], bias, NEG_INF)


def _diff_bias(table):
    pos = jnp.arange(SEQ, dtype=jnp.int32)
    return jnp.transpose(table[_rel_bucket(pos[None, :] - pos[:, None])], (2, 0, 1))


def _block_ones(group):
    idx = np.arange(MXU_DIM) // group
    return jnp.asarray(idx[:, None] == idx[None, :], dtype=BF16)


def _pad_row(v):
    return jnp.pad(v.astype(F32), (0, D_MODEL - v.shape[0]))


def _layer_params(layer, norm1_g, w_in, mla_q_norm_g, mla_kv_norm_g, mla_w_uq, mla_w_ukv, mla_qk_g, dil_qk_g,
                  gqa_qk_g, diff_qk_g, diff_lambda, diff_subln_g, mix_beta, w_out, norm2_g, router_group_w,
                  router_group_b, router_expert_w, router_expert_b):
    w = w_in[layer]
    w_p = _bf(jnp.concatenate([w[:, :416], jnp.zeros((D_MODEL, 96), F32), w[:, 416:]], axis=1))
    uq = mla_w_uq[layer].reshape(MLA_Q_RANK, N_HEADS, MLA_QK_DIM)
    wuq = _bf(jnp.pad(uq, ((0, 0), (0, 0), (0, LANES - MLA_QK_DIM))).reshape(MLA_Q_RANK, 512))
    ukv = mla_w_ukv[layer].reshape(MLA_KV_RANK, N_HEADS, 2 * MLA_NOPE_DIM)
    wuk = _bf(jnp.pad(ukv[:, :, :MLA_NOPE_DIM], ((0, 0), (0, 0), (0, LANES - MLA_NOPE_DIM))).reshape(MLA_KV_RANK, 512))
    wuv = _bf(ukv[:, :, MLA_NOPE_DIM:].reshape(MLA_KV_RANK, 256))

    pad96 = lambda g: jnp.tile(jnp.pad(g, (0, LANES - MLA_QK_DIM)), N_HEADS)
    m0 = jnp.tile(jnp.concatenate([jnp.ones(32, F32), jnp.zeros(32, F32)]), N_HEADS)
    dq = jnp.tile(diff_qk_g[layer, 0], 2 * N_HEADS) * DIFF_QK_DIM ** -0.5
    rows = [
        norm1_g[layer], mla_q_norm_g[layer], mla_kv_norm_g[layer],
        pad96(mla_qk_g[layer, 0]) * MLA_QK_DIM ** -0.5, pad96(mla_qk_g[layer, 1]),
        jnp.tile(dil_qk_g[layer, 0], N_HEADS) * HEAD_DIM ** -0.5, jnp.tile(dil_qk_g[layer, 1], N_HEADS),
        jnp.tile(gqa_qk_g[layer, 0], N_HEADS) * HEAD_DIM ** -0.5, jnp.tile(gqa_qk_g[layer, 1], 2),
        dq * m0, dq * (1.0 - m0), jnp.tile(diff_qk_g[layer, 1], 2 * N_HEADS),
    ]
    gv = jnp.stack([_pad_row(r) for r in rows] + [jnp.zeros(D_MODEL, F32)] * (GAIN_ROWS - len(rows)))

    lambda_init = 0.8 - 0.6 * math.exp(-0.3 * layer)
    lv = diff_lambda[layer].astype(F32)
    lam = (jnp.exp(jnp.sum(lv[0] * lv[1])) - jnp.exp(jnp.sum(lv[2] * lv[3])) + lambda_init).reshape(1)
    sub_gain = (jnp.tile(diff_subln_g[layer], N_HEADS) * (1.0 - lambda_init)).reshape(1, 256)

    cv = jnp.stack([mix_beta[layer], norm2_g[layer]] + [jnp.zeros(D_MODEL, F32)] * 6)
    wr = jnp.concatenate([router_group_w[layer], router_expert_w[layer],
                          jnp.zeros((D_MODEL, LANES - 4 - N_EXPERTS), F32)], axis=1)
    wr_hi = _bf(wr)
    wr_lo = _bf(wr - wr_hi.astype(F32))
    rb = jnp.concatenate([router_group_b[layer], router_expert_b[layer],
                          jnp.zeros(LANES - 4 - N_EXPERTS, F32)]).reshape(1, LANES)
    return dict(w_in=w_p, wuq=wuq, wuk=wuk, wuv=wuv, gv=gv, lam=lam, sub_gain=sub_gain, cv=cv,
                w_out=_bf(w_out[layer]), wr_hi=wr_hi, wr_lo=wr_lo, rb=rb)


def kernel(x, rel_bias, norm1_g, w_in, mla_q_norm_g, mla_kv_norm_g, mla_w_uq, mla_w_ukv, mla_qk_g, dil_qk_g, gqa_qk_g, diff_qk_g, diff_lambda, diff_subln_g, mix_beta, w_out, norm2_g, router_group_w, router_group_b, router_expert_w, router_expert_b, expert_w_gate, expert_w_up, expert_w_down):
    batch, seq, d_model = x.shape
    assert seq == SEQ and d_model == D_MODEL
    depth = w_in.shape[0]
    t = batch * seq

    rope = _rope_tables()
    g32, g64, g128 = _block_ones(32), _block_ones(64), _block_ones(128)
    dil_bias = [_dil_bias(rel_bias[:, :N_HEADS], d) for _, d in DIL_PATTERNS]
    diff_bias = _diff_bias(rel_bias[:, N_HEADS:])

    x2d = x.reshape(t, D_MODEL)
    for layer in range(depth):
        p = _layer_params(layer, norm1_g, w_in, mla_q_norm_g, mla_kv_norm_g, mla_w_uq, mla_w_ukv, mla_qk_g,
                          dil_qk_g, gqa_qk_g, diff_qk_g, diff_lambda, diff_subln_g, mix_beta, w_out, norm2_g,
                          router_group_w, router_group_b, router_expert_w, router_expert_b)
        act = _inproj(x2d, p["gv"], p["w_in"], p["wuq"], p["wuk"], p["wuv"], rope, g32, g64, g128)
        ya = _attn_full(act, batch, _attn_mla_kernel, A_QA, 512, A_KA, 512, A_VA, 256, "attn_mla")
        yb = _attn_dilated(act, batch, dil_bias)
        yc = _attn_full(act, batch, _attn_gqa_kernel, A_QC, 256, A_KC, 128, A_VC, 128, "attn_gqa")
        yd = _attn_diff(act, batch, p["lam"], diff_bias, p["sub_gain"])
        x1, h2, gate = _outproj(ya, yb, yc, yd, x2d, p["cv"], p["w_out"], p["wr_hi"], p["wr_lo"], p["rb"])
        x2d = _moe_dense(h2, gate, x1, _bf(expert_w_gate[layer]), _bf(expert_w_up[layer]),
                         _bf(expert_w_down[layer]))
    return x2d.reshape(batch, seq, D_MODEL)
```

```python
import functools
import math

import jax
import jax.numpy as jnp
import numpy as np
from jax import lax
from jax.experimental import pallas as pl
from jax.experimental.pallas import tpu as pltpu

F32 = jnp.float32
BF16 = jnp.bfloat16

D_MODEL = 1024
SEQ = 2048
HEAD_DIM = 64
GRID_W = 64
ROPE_THETA = 10000.0
NORM_EPS = 1e-6
NEG_INF = -1e30
NUM_BUCKETS = 32
REL_MAX_DISTANCE = 1024

N_HEADS = 4
MLA_NOPE_DIM = 64
MLA_ROPE_DIM = 32
MLA_QK_DIM = MLA_NOPE_DIM + MLA_ROPE_DIM
MLA_Q_RANK = 256
MLA_KV_RANK = 128
DIL_PATTERNS = ((128, 1), (512, 4), (2048, 16))
DIL_HALF = 64
DIFF_QK_DIM = 32
N_EXPERTS = 16
EXPERTS_PER_GROUP = 4
D_FF = 512

LANES = 128
MXU_DIM = 256
VMEM_LIMIT_BYTES = 56 * 1024 * 1024

P_CQ, P_CKV, P_KR = 0, 256, 384
P_BQ, P_BK, P_BV = 512, 768, 1024
P_CQ2, P_CK2, P_CV2 = 1280, 1536, 1664
P_DQ, P_DK, P_DV = 1792, 2048, 2304
PROJ_COLS = 2560

A_QB, A_KB, A_VB = 0, 256, 512
A_VA, A_QA, A_KA = 768, 1024, 1536
A_QC, A_KC, A_VC = 2048, 2304, 2432
A_QD0, A_QD1, A_KD, A_VD = 2560, 2816, 3072, 3328
ACT_COLS = 3584
DIL_COLS = 768

(G_NORM1, G_MLA_QN, G_MLA_KVN, G_MLA_Q, G_MLA_K, G_DIL_Q, G_DIL_K, G_GQA_Q, G_GQA_K,
 G_DIFF_Q0, G_DIFF_Q1, G_DIFF_K) = range(12)
GAIN_ROWS = 16

TM_PROJ = 256
TQ = 256
TM_OUT = 256
TM_MOE = 512
DIL_QBLK = 128


def _bf(x):
    return x.astype(BF16)


def _dot(a, b):
    return jnp.dot(a, b, preferred_element_type=F32)


def _dot_nt(a, b):
    return lax.dot_general(a, b, (((1,), (1,)), ((), ())), preferred_element_type=F32)


def _rms(x, width):
    return x * lax.rsqrt(jnp.sum(x * x, axis=-1, keepdims=True) * (1.0 / width) + NORM_EPS)


def _group_sumsq(x, g):
    x2 = x * x
    hi = _bf(x2)
    lo = _bf(x2 - hi.astype(F32))
    w = g.shape[0]
    outs = []
    for c in range(x.shape[1] // w):
        sl = slice(w * c, w * (c + 1))
        outs.append(_dot(hi[:, sl], g) + _dot(lo[:, sl], g))
    return outs[0] if len(outs) == 1 else jnp.concatenate(outs, axis=1)


def _group_rms(x, g, group):
    return x * lax.rsqrt(_group_sumsq(x, g) * (1.0 / group) + NORM_EPS)


def _rope(x, c, s_next, s_prev, half):
    n = x.shape[1]
    return x * c + pltpu.roll(x, n - half, axis=1) * s_next + pltpu.roll(x, half, axis=1) * s_prev


def _inproj_kernel(x_ref, gv_ref, w_ref, wuq_ref, wuk_ref, wuv_ref, rope_ref, g32_ref, g64_ref, g128_ref,
                   o_ref, o4_ref, o16_ref, stage):
    def gain(row, width):
        return gv_ref[row:row + 1, 0:width]

    x = x_ref[...]
    h = _rms(x, D_MODEL) * gain(G_NORM1, D_MODEL)
    proj = _dot(_bf(h), w_ref[...])

    g32 = g32_ref[...]
    g64 = g64_ref[...]
    g128 = g128_ref[...]
    m_c = rope_ref[:, 0:128]
    m_sn = rope_ref[:, 128:256]
    m_sp = rope_ref[:, 256:384]
    a_c = rope_ref[:, 384:512]
    a_sn = rope_ref[:, 512:640]
    a_sp = rope_ref[:, 640:768]

    cq = _rms(proj[:, P_CQ:P_CQ + MLA_Q_RANK], MLA_Q_RANK) * gain(G_MLA_QN, MLA_Q_RANK)
    ckv = _rms(proj[:, P_CKV:P_CKV + MLA_KV_RANK], MLA_KV_RANK) * gain(G_MLA_KVN, MLA_KV_RANK)
    q = _dot(_bf(cq), wuq_ref[...])
    k_nope = _dot(_bf(ckv), wuk_ref[...])
    v = _dot(_bf(ckv), wuv_ref[...])
    k_rope = pltpu.roll(proj[:, P_KR:P_KR + LANES], MLA_NOPE_DIM, axis=1)
    k = k_nope + jnp.concatenate([k_rope] * N_HEADS, axis=1)
    qn = _group_rms(q, g128, MLA_QK_DIM) * gain(G_MLA_Q, 512)
    kn = _group_rms(k, g128, MLA_QK_DIM) * gain(G_MLA_K, 512)
    for g in range(N_HEADS):
        sl = slice(LANES * g, LANES * (g + 1))
        o_ref[:, A_QA + LANES * g:A_QA + LANES * (g + 1)] = _bf(_rope(qn[:, sl], m_c, m_sn, m_sp, MLA_ROPE_DIM // 2))
        o_ref[:, A_KA + LANES * g:A_KA + LANES * (g + 1)] = _bf(_rope(kn[:, sl], m_c, m_sn, m_sp, MLA_ROPE_DIM // 2))
    o_ref[:, A_VA:A_VA + 256] = _bf(v)

    qb = _group_rms(proj[:, P_BQ:P_BQ + 256], g64, HEAD_DIM) * gain(G_DIL_Q, 256)
    kb = _group_rms(proj[:, P_BK:P_BK + 256], g64, HEAD_DIM) * gain(G_DIL_K, 256)
    vb = proj[:, P_BV:P_BV + 256]
    o_ref[:, A_QB:A_QB + 256] = _bf(qb)
    o_ref[:, A_KB:A_KB + 256] = _bf(kb)
    o_ref[:, A_VB:A_VB + 256] = _bf(vb)
    for c, val in enumerate((qb, kb, vb)):
        stage[2 * c] = val[:, 0:LANES]
        stage[2 * c + 1] = val[:, LANES:2 * LANES]
    n_chunks = DIL_COLS // LANES
    for d, ref in ((4, o4_ref), (16, o16_ref)):
        n = x.shape[0] // d
        for r in range(d):
            ref[0, r] = _bf(jnp.concatenate(
                [stage[c, pl.ds(r, n, stride=d), :] for c in range(n_chunks)], axis=1))

    qc = _group_rms(proj[:, P_CQ2:P_CQ2 + 256], g64, HEAD_DIM) * gain(G_GQA_Q, 256)
    for g in range(2):
        sl = slice(LANES * g, LANES * (g + 1))
        o_ref[:, A_QC + LANES * g:A_QC + LANES * (g + 1)] = _bf(_rope(qc[:, sl], a_c, a_sn, a_sp, HEAD_DIM // 4))
    kc = _group_rms(proj[:, P_CK2:P_CK2 + 128], g64[0:128, 0:128], HEAD_DIM) * gain(G_GQA_K, 128)
    o_ref[:, A_KC:A_KC + 128] = _bf(_rope(kc, a_c, a_sn, a_sp, HEAD_DIM // 4))
    o_ref[:, A_VC:A_VC + 128] = _bf(proj[:, P_CV2:P_CV2 + 128])

    dqn = _group_rms(proj[:, P_DQ:P_DQ + 256], g32, DIFF_QK_DIM)
    o_ref[:, A_QD0:A_QD0 + 256] = _bf(dqn * gain(G_DIFF_Q0, 256))
    o_ref[:, A_QD1:A_QD1 + 256] = _bf(dqn * gain(G_DIFF_Q1, 256))
    o_ref[:, A_KD:A_KD + 256] = _bf(_group_rms(proj[:, P_DK:P_DK + 256], g32, DIFF_QK_DIM) * gain(G_DIFF_K, 256))
    o_ref[:, A_VD:A_VD + 256] = _bf(proj[:, P_DV:P_DV + 256])


def _inproj(x2d, gv, w_in, wuq, wuk, wuv, rope, g32, g64, g128):
    t = x2d.shape[0]
    tm = TM_PROJ
    n_pos = SEQ // tm
    const = lambda i: (0, 0)
    return pl.pallas_call(
        _inproj_kernel,
        grid=(t // tm,),
        in_specs=[
            pl.BlockSpec((tm, D_MODEL), lambda i: (i, 0)),
            pl.BlockSpec((GAIN_ROWS, D_MODEL), const),
            pl.BlockSpec((D_MODEL, PROJ_COLS), const),
            pl.BlockSpec((MLA_Q_RANK, 512), const),
            pl.BlockSpec((MLA_KV_RANK, 512), const),
            pl.BlockSpec((MLA_KV_RANK, 256), const),
            pl.BlockSpec((tm, 768), lambda i: (i % n_pos, 0)),
            pl.BlockSpec((MXU_DIM, MXU_DIM), const),
            pl.BlockSpec((MXU_DIM, MXU_DIM), const),
            pl.BlockSpec((MXU_DIM, MXU_DIM), const),
        ],
        out_specs=[
            pl.BlockSpec((tm, ACT_COLS), lambda i: (i, 0)),
            pl.BlockSpec((1, 4, tm // 4, DIL_COLS), lambda i: (i // n_pos, 0, i % n_pos, 0)),
            pl.BlockSpec((1, 16, tm // 16, DIL_COLS), lambda i: (i // n_pos, 0, i % n_pos, 0)),
        ],
        out_shape=[
            jax.ShapeDtypeStruct((t, ACT_COLS), BF16),
            jax.ShapeDtypeStruct((t // SEQ, 4, SEQ // 4, DIL_COLS), BF16),
            jax.ShapeDtypeStruct((t // SEQ, 16, SEQ // 16, DIL_COLS), BF16),
        ],
        scratch_shapes=[pltpu.VMEM((DIL_COLS // LANES, tm, LANES), F32)],
        compiler_params=pltpu.CompilerParams(dimension_semantics=("parallel",), vmem_limit_bytes=VMEM_LIMIT_BYTES),
        name="inproj_prep",
    )(x2d, gv, w_in, wuq, wuk, wuv, rope, g32, g64, g128)


def _softmax_pv(s, v):
    m = jnp.max(s, axis=-1, keepdims=True)
    e = jnp.exp(s - m)
    l = jnp.sum(e, axis=-1, keepdims=True)
    return _dot(_bf(e), v) / l


def _attn_mla_kernel(q_ref, k_ref, v_ref, o_ref):
    for h in range(N_HEADS):
        sl = slice(LANES * h, LANES * (h + 1))
        s = _dot_nt(q_ref[:, sl], k_ref[:, sl])
        o_ref[:, HEAD_DIM * h:HEAD_DIM * (h + 1)] = _softmax_pv(s, v_ref[:, HEAD_DIM * h:HEAD_DIM * (h + 1)])


def _attn_gqa_kernel(q_ref, k_ref, v_ref, o_ref):
    for h in range(N_HEADS):
        g = h // 2
        hs = slice(HEAD_DIM * h, HEAD_DIM * (h + 1))
        gs = slice(HEAD_DIM * g, HEAD_DIM * (g + 1))
        s = _dot_nt(q_ref[:, hs], k_ref[:, gs])
        o_ref[:, hs] = _softmax_pv(s, v_ref[:, gs])


def _attn_full(act, batch, kernel, q_col, q_w, k_col, k_w, v_col, v_w, name):
    t = act.shape[0]
    nq = SEQ // TQ
    return pl.pallas_call(
        kernel,
        grid=(batch, nq),
        in_specs=[
            pl.BlockSpec((TQ, q_w), lambda b, i: (b * nq + i, q_col // q_w)),
            pl.BlockSpec((SEQ, k_w), lambda b, i: (b, k_col // k_w)),
            pl.BlockSpec((SEQ, v_w), lambda b, i: (b, v_col // v_w)),
        ],
        out_specs=pl.BlockSpec((TQ, 256), lambda b, i: (b * nq + i, 0)),
        out_shape=jax.ShapeDtypeStruct((t, 256), F32),
        compiler_params=pltpu.CompilerParams(
            dimension_semantics=("parallel", "parallel"), vmem_limit_bytes=VMEM_LIMIT_BYTES),
        name=name,
    )(act, act, act)


def _attn_diff_kernel(lam_ref, q0_ref, q1_ref, k_ref, v_ref, win_ref, sg_ref, o_ref, bias_ref):
    @pl.when(pl.program_id(1) == 0)
    def _():
        for h in range(N_HEADS):
            w = jnp.broadcast_to(win_ref[0, h:h + 1, :], (TQ, SEQ + TQ))
            bias_ref[h] = pltpu.roll(w, 0, axis=1, stride=1, stride_axis=0)[:, TQ:TQ + SEQ]

    lam = lam_ref[0]
    for h in range(N_HEADS):
        hs = slice(HEAD_DIM * h, HEAD_DIM * (h + 1))
        k = k_ref[:, hs]
        bias = bias_ref[h]
        s0 = _dot_nt(q0_ref[:, hs], k) + bias
        s1 = _dot_nt(q1_ref[:, hs], k) + bias
        e0 = jnp.exp(s0 - jnp.max(s0, axis=-1, keepdims=True))
        e1 = jnp.exp(s1 - jnp.max(s1, axis=-1, keepdims=True))
        r0 = 1.0 / jnp.sum(e0, axis=-1, keepdims=True)
        r1 = lam / jnp.sum(e1, axis=-1, keepdims=True)
        attn = e0 * r0 - e1 * r1
        o = _dot(_bf(attn), v_ref[:, hs])
        o_ref[:, hs] = _rms(o, HEAD_DIM) * sg_ref[:, hs]


def _attn_diff(act, batch, lam, bias_win, sub_gain):
    t = act.shape[0]
    nq = SEQ // TQ
    return pl.pallas_call(
        _attn_diff_kernel,
        grid=(nq, batch),
        in_specs=[
            pl.BlockSpec(memory_space=pltpu.SMEM),
            pl.BlockSpec((TQ, 256), lambda i, b: (b * nq + i, A_QD0 // 256)),
            pl.BlockSpec((TQ, 256), lambda i, b: (b * nq + i, A_QD1 // 256)),
            pl.BlockSpec((SEQ, 256), lambda i, b: (b, A_KD // 256)),
            pl.BlockSpec((SEQ, 256), lambda i, b: (b, A_VD // 256)),
            pl.BlockSpec((1, N_HEADS, SEQ + TQ), lambda i, b: (i, 0, 0)),
            pl.BlockSpec((1, 256), lambda i, b: (0, 0)),
        ],
        out_specs=pl.BlockSpec((TQ, 256), lambda i, b: (b * nq + i, 0)),
        out_shape=jax.ShapeDtypeStruct((t, 256), F32),
        scratch_shapes=[pltpu.VMEM((N_HEADS, TQ, SEQ), F32)],
        compiler_params=pltpu.CompilerParams(
            dimension_semantics=("arbitrary", "arbitrary"), vmem_limit_bytes=VMEM_LIMIT_BYTES),
        name="attn_diff",
    )(lam, act, act, act, act, bias_win, sub_gain)


def _dil_heads(q_of, k_of, v_of, bias_of, in_seq):
    outs, lses = [], []
    for h in range(N_HEADS):
        hs = slice(HEAD_DIM * h, HEAD_DIM * (h + 1))
        s = _dot_nt(q_of(hs), k_of(hs)) + bias_of(h)
        if in_seq is not None:
            s = jnp.where(in_seq, s, NEG_INF)
        m = jnp.max(s, axis=-1, keepdims=True)
        e = jnp.exp(s - m)
        l = jnp.sum(e, axis=-1, keepdims=True)
        outs.append(_dot(_bf(e), v_of(hs)) / l)
        lses.append(jnp.broadcast_to(m + jnp.log(l), (DIL_QBLK, HEAD_DIM)))
    cat = lambda xs: [jnp.concatenate(xs[2 * c:2 * c + 2], axis=1) for c in range(2)]
    return cat(outs), cat(lses)


def _dil_kernel(n1_ref, r4_ref, r16_ref, bias_ref, o_ref, kpad, vpad, o1, l1, o4, l4, out):
    zeros = jnp.zeros((DIL_HALF, 256), BF16)
    col = lax.broadcasted_iota(jnp.int32, (1, 2 * DIL_QBLK), 1) - DIL_HALF

    def fill(k, v, n):
        kpad[0:DIL_HALF, :] = zeros
        vpad[0:DIL_HALF, :] = zeros
        kpad[DIL_HALF:DIL_HALF + n, :] = k
        vpad[DIL_HALF:DIL_HALF + n, :] = v
        kpad[DIL_HALF + n:2 * DIL_HALF + n, :] = zeros
        vpad[DIL_HALF + n:2 * DIL_HALF + n, :] = zeros

    def in_seq(j, n):
        key_pos = col + j * DIL_QBLK
        return jnp.logical_and(key_pos >= 0, key_pos < n)

    def windowed(q_of, r0, j, n, branch):
        win = pl.ds(r0, 2 * DIL_QBLK)
        return _dil_heads(q_of, lambda hs: kpad[win, hs], lambda hs: vpad[win, hs],
                          lambda h: bias_ref[branch, h], in_seq(j, n))

    fill(n1_ref[:, 256:512], n1_ref[:, 512:768], SEQ)

    def body1(j, carry):
        r0 = pl.multiple_of(j * DIL_QBLK, DIL_QBLK)
        rows = pl.ds(r0, DIL_QBLK)
        oc, lc = windowed(lambda hs: n1_ref[rows, hs], r0, j, SEQ, 0)
        for c in range(2):
            o1[c, rows, :] = oc[c]
            l1[c, rows, :] = lc[c]
        return carry
    lax.fori_loop(0, SEQ // DIL_QBLK, body1, 0)

    n4 = SEQ // 4
    for r in range(4):
        fill(r4_ref[0, r, :, 256:512], r4_ref[0, r, :, 512:768], n4)
        for j in range(n4 // DIL_QBLK):
            r0 = j * DIL_QBLK
            oc, lc = windowed(lambda hs: r4_ref[0, r, r0:r0 + DIL_QBLK, hs], r0, j, n4, 1)
            tok = pl.ds(4 * r0 + r, DIL_QBLK, stride=4)
            for c in range(2):
                o4[c, tok, :] = oc[c]
                l4[c, tok, :] = lc[c]

    def body16(r, carry):
        oc, lc = _dil_heads(lambda hs: r16_ref[0, r, :, hs],
                            lambda hs: r16_ref[0, r, :, 256 + hs.start:256 + hs.stop],
                            lambda hs: r16_ref[0, r, :, 512 + hs.start:512 + hs.stop],
                            lambda h: bias_ref[2, h, :, DIL_HALF:DIL_HALF + DIL_QBLK], None)
        tok = pl.ds(r, DIL_QBLK, stride=16)
        for c in range(2):
            la, lb = l1[c, tok, :], l4[c, tok, :]
            mx = jnp.maximum(jnp.maximum(la, lb), lc[c])
            wa, wb, wc = jnp.exp(la - mx), jnp.exp(lb - mx), jnp.exp(lc[c] - mx)
            den = wa + wb + wc
            out[c, tok, :] = (wa / den) * o1[c, tok, :] + (wb / den) * o4[c, tok, :] + (wc / den) * oc[c]
        return carry
    lax.fori_loop(0, 16, body16, 0)
    o_ref[:, 0:LANES] = out[0]
    o_ref[:, LANES:2 * LANES] = out[1]


def _attn_dilated(act, act4, act16, batch, bias_tabs):
    t = act.shape[0]
    nat = pltpu.VMEM((2, SEQ, LANES), F32)
    pad = pltpu.VMEM((SEQ + 2 * DIL_HALF, 256), BF16)
    return pl.pallas_call(
        _dil_kernel,
        grid=(batch,),
        in_specs=[
            pl.BlockSpec((SEQ, DIL_COLS), lambda b: (b, 0)),
            pl.BlockSpec((1, 4, SEQ // 4, DIL_COLS), lambda b: (b, 0, 0, 0)),
            pl.BlockSpec((1, 16, SEQ // 16, DIL_COLS), lambda b: (b, 0, 0, 0)),
            pl.BlockSpec((3, N_HEADS, DIL_QBLK, 2 * DIL_QBLK), lambda b: (0, 0, 0, 0)),
        ],
        out_specs=pl.BlockSpec((SEQ, 256), lambda b: (b, 0)),
        out_shape=jax.ShapeDtypeStruct((t, 256), F32),
        scratch_shapes=[pad, pad, nat, nat, nat, nat, nat],
        compiler_params=pltpu.CompilerParams(dimension_semantics=("parallel",), vmem_limit_bytes=VMEM_LIMIT_BYTES),
        name="attn_dilated",
    )(act, act4, act16, bias_tabs)


def _route(lt):
    g = [lt[i:i + 1, :] for i in range(4)]
    gmax = jnp.maximum(jnp.maximum(g[0], g[1]), jnp.maximum(g[2], g[3]))
    gsum = sum(jnp.exp(gi - gmax) for gi in g)
    g_w = 1.0 / gsum
    gidx = jnp.where(g[0] == gmax, 0, jnp.where(g[1] == gmax, 1, jnp.where(g[2] == gmax, 2, 3)))
    el = []
    for j in range(EXPERTS_PER_GROUP):
        acc = jnp.zeros_like(g[0])
        for i in range(4):
            r = 4 + EXPERTS_PER_GROUP * i + j
            acc = jnp.where(gidx == i, lt[r:r + 1, :], acc)
        el.append(acc)
    emax = jnp.maximum(jnp.maximum(el[0], el[1]), jnp.maximum(el[2], el[3]))
    ee = [jnp.exp(e - emax) for e in el]
    esum = ee[0] + ee[1] + ee[2] + ee[3]
    p = [e / esum for e in ee]
    p1 = jnp.maximum(jnp.maximum(p[0], p[1]), jnp.maximum(p[2], p[3]))
    i1 = jnp.where(p[0] == p1, 0, jnp.where(p[1] == p1, 1, jnp.where(p[2] == p1, 2, 3)))
    pm = [jnp.where(i1 == j, -1.0, p[j]) for j in range(4)]
    p2 = jnp.maximum(jnp.maximum(pm[0], pm[1]), jnp.maximum(pm[2], pm[3]))
    i2 = jnp.where(pm[0] == p2, 0, jnp.where(pm[1] == p2, 1, jnp.where(pm[2] == p2, 2, 3)))
    den = p1 + p2
    return gidx * EXPERTS_PER_GROUP + i1, gidx * EXPERTS_PER_GROUP + i2, g_w * (p1 / den), g_w * (p2 / den)


def _outproj_kernel(ya_ref, yb_ref, yc_ref, yd_ref, x_ref, cv_ref, w_ref, wrh_ref, wrl_ref, rb_ref,
                    x1_ref, h2_ref, gate_ref):
    mixed = jnp.concatenate(
        [_rms(ya_ref[...], 256), _rms(yb_ref[...], 256), _rms(yc_ref[...], 256), yd_ref[...]], axis=1)
    mixed = mixed * cv_ref[0:1, :]
    x1 = x_ref[...] + _dot(_bf(mixed), w_ref[...])
    x1_ref[...] = x1
    h2 = _rms(x1, D_MODEL) * cv_ref[1:2, :]
    h_hi = _bf(h2)
    h2_ref[...] = h_hi
    h_lo = _bf(h2 - h_hi.astype(F32))
    logits = _dot(h_hi, wrh_ref[...]) + _dot(h_hi, wrl_ref[...]) + _dot(h_lo, wrh_ref[...]) + rb_ref[...]
    id1, id2, w1, w2 = _route(logits.T)
    rows = lax.broadcasted_iota(jnp.int32, (LANES, logits.shape[0]), 0)
    gate_t = jnp.where(rows == id1, w1, 0.0) + jnp.where(rows == id2, w2, 0.0)
    gate_ref[...] = gate_t.T


def _outproj(ya, yb, yc, yd, x2d, cv, w_out, wr_hi, wr_lo, rb):
    t = x2d.shape[0]
    tm = TM_OUT
    row = lambda w: pl.BlockSpec((tm, w), lambda i: (i, 0))
    const = lambda shape: pl.BlockSpec(shape, lambda i: (0, 0))
    return pl.pallas_call(
        _outproj_kernel,
        grid=(t // tm,),
        in_specs=[row(256), row(256), row(256), row(256), row(D_MODEL), const((8, D_MODEL)),
                  const((D_MODEL, D_MODEL)), const((D_MODEL, LANES)), const((D_MODEL, LANES)), const((1, LANES))],
        out_specs=[row(D_MODEL), row(D_MODEL), row(LANES)],
        out_shape=[jax.ShapeDtypeStruct((t, D_MODEL), F32), jax.ShapeDtypeStruct((t, D_MODEL), BF16),
                   jax.ShapeDtypeStruct((t, LANES), F32)],
        compiler_params=pltpu.CompilerParams(dimension_semantics=("parallel",), vmem_limit_bytes=VMEM_LIMIT_BYTES),
        name="outproj_router",
    )(ya, yb, yc, yd, x2d, cv, w_out, wr_hi, wr_lo, rb)


def _moe_dense_kernel(h_ref, gate_ref, x1_ref, wg_ref, wu_ref, wd_ref, o_ref):
    e = pl.program_id(1)

    @pl.when(e == 0)
    def _():
        o_ref[...] = x1_ref[...]

    gate = gate_ref[...]
    lane = lax.broadcasted_iota(jnp.int32, gate.shape, 1)
    g = jnp.sum(jnp.where(lane == e, gate, 0.0), axis=-1, keepdims=True)
    h = h_ref[...]
    a = _dot(h, wg_ref[0])
    u = _dot(h, wu_ref[0])
    hid = a * (1.0 / (1.0 + jnp.exp(-a))) * u
    o_ref[...] += g * _dot(_bf(hid), wd_ref[0])


def _moe_dense(h2, gate, x1, wg, wu, wd):
    t = h2.shape[0]
    tm = TM_MOE
    return pl.pallas_call(
        _moe_dense_kernel,
        grid=(t // tm, N_EXPERTS),
        in_specs=[
            pl.BlockSpec((tm, D_MODEL), lambda i, e: (i, 0)),
            pl.BlockSpec((tm, LANES), lambda i, e: (i, 0)),
            pl.BlockSpec((tm, D_MODEL), lambda i, e: (i, 0)),
            pl.BlockSpec((1, D_MODEL, D_FF), lambda i, e: (e, 0, 0)),
            pl.BlockSpec((1, D_MODEL, D_FF), lambda i, e: (e, 0, 0)),
            pl.BlockSpec((1, D_FF, D_MODEL), lambda i, e: (e, 0, 0)),
        ],
        out_specs=pl.BlockSpec((tm, D_MODEL), lambda i, e: (i, 0)),
        out_shape=jax.ShapeDtypeStruct((t, D_MODEL), F32),
        compiler_params=pltpu.CompilerParams(
            dimension_semantics=("parallel", "arbitrary"), vmem_limit_bytes=VMEM_LIMIT_BYTES),
        name="moe_dense",
    )(h2, gate, x1, wg, wu, wd)


def _rope_angles(pos, dim):
    inv = 1.0 / (ROPE_THETA ** (jnp.arange(0, dim, 2, dtype=F32) / dim))
    return pos.astype(F32)[:, None] * inv[None, :]


def _rope_tables():
    pos = jnp.arange(SEQ, dtype=jnp.int32)
    z = lambda w: jnp.zeros((SEQ, w), F32)
    ang = _rope_angles(pos, MLA_ROPE_DIM)
    c, s = jnp.cos(ang), jnp.sin(ang)
    m_c = jnp.concatenate([jnp.ones((SEQ, 64), F32), c, c, z(32)], axis=1)
    m_sn = jnp.concatenate([z(64), -s, z(48)], axis=1)
    m_sp = jnp.concatenate([z(80), s, z(32)], axis=1)
    row_pos = pos // GRID_W
    col_pos = pos - row_pos * GRID_W
    ra, ca = _rope_angles(row_pos, HEAD_DIM // 2), _rope_angles(col_pos, HEAD_DIM // 2)
    rc, rs, cc, cs = jnp.cos(ra), jnp.sin(ra), jnp.cos(ca), jnp.sin(ca)
    a_c = jnp.concatenate([rc, rc, cc, cc] * 2, axis=1)
    a_sn = jnp.concatenate([-rs, z(16), -cs, z(16)] * 2, axis=1)
    a_sp = jnp.concatenate([z(16), rs, z(16), cs] * 2, axis=1)
    return jnp.concatenate([m_c, m_sn, m_sp, a_c, a_sn, a_sp], axis=1)


def _rel_bucket(rel):
    half = NUM_BUCKETS // 2
    max_exact = half // 2
    n = jnp.abs(rel)
    nf = jnp.maximum(n, 1).astype(F32)
    log_ratio = jnp.log(nf / max_exact) / math.log(REL_MAX_DISTANCE / max_exact)
    large = jnp.minimum(max_exact + (log_ratio * (half - max_exact)).astype(jnp.int32), half - 1)
    return jnp.where(rel > 0, half, 0) + jnp.where(n < max_exact, n, large)


def _bias_lookup(table, rel):
    bucket = _rel_bucket(rel)
    onehot = bucket[None, ..., None] == jnp.arange(NUM_BUCKETS, dtype=jnp.int32)
    tab = table.T.reshape((table.shape[1],) + (1,) * rel.ndim + (NUM_BUCKETS,))
    return jnp.sum(jnp.where(onehot, tab, 0.0), axis=-1)


def _dil_bias(table):
    qi = jnp.arange(DIL_QBLK, dtype=jnp.int32)
    kj = jnp.arange(2 * DIL_QBLK, dtype=jnp.int32) - DIL_HALF
    rel = kj[None, :] - qi[:, None]
    band = (jnp.abs(rel) <= DIL_HALF)[None]
    return jnp.stack([jnp.where(band, _bias_lookup(table, rel * d), NEG_INF) for _, d in DIL_PATTERNS])


def _diff_bias_windows(table):
    rel = jnp.arange(2 * SEQ, dtype=jnp.int32) - SEQ
    line = _bias_lookup(table, rel)
    starts = [SEQ - (i + 1) * TQ for i in range(SEQ // TQ)]
    return jnp.stack([line[:, s0:s0 + SEQ + TQ] for s0 in starts])


def _block_ones(group):
    idx = np.arange(MXU_DIM) // group
    return jnp.asarray(idx[:, None] == idx[None, :], dtype=BF16)


def _pad_row(v):
    return jnp.pad(v.astype(F32), (0, D_MODEL - v.shape[0]))


def _layer_params(layer, norm1_g, w_in, mla_q_norm_g, mla_kv_norm_g, mla_w_uq, mla_w_ukv, mla_qk_g, dil_qk_g,
                  gqa_qk_g, diff_qk_g, diff_lambda, diff_subln_g, mix_beta, w_out, norm2_g, router_group_w,
                  router_group_b, router_expert_w, router_expert_b):
    w = w_in[layer]
    w_p = _bf(jnp.concatenate([w[:, :416], jnp.zeros((D_MODEL, 96), F32), w[:, 416:]], axis=1))
    uq = mla_w_uq[layer].reshape(MLA_Q_RANK, N_HEADS, MLA_QK_DIM)
    wuq = _bf(jnp.pad(uq, ((0, 0), (0, 0), (0, LANES - MLA_QK_DIM))).reshape(MLA_Q_RANK, 512))
    ukv = mla_w_ukv[layer].reshape(MLA_KV_RANK, N_HEADS, 2 * MLA_NOPE_DIM)
    wuk = _bf(jnp.pad(ukv[:, :, :MLA_NOPE_DIM], ((0, 0), (0, 0), (0, LANES - MLA_NOPE_DIM))).reshape(MLA_KV_RANK, 512))
    wuv = _bf(ukv[:, :, MLA_NOPE_DIM:].reshape(MLA_KV_RANK, 256))

    pad96 = lambda g: jnp.tile(jnp.pad(g, (0, LANES - MLA_QK_DIM)), N_HEADS)
    m0 = jnp.tile(jnp.concatenate([jnp.ones(32, F32), jnp.zeros(32, F32)]), N_HEADS)
    dq = jnp.tile(diff_qk_g[layer, 0], 2 * N_HEADS) * DIFF_QK_DIM ** -0.5
    rows = [
        norm1_g[layer], mla_q_norm_g[layer], mla_kv_norm_g[layer],
        pad96(mla_qk_g[layer, 0]) * MLA_QK_DIM ** -0.5, pad96(mla_qk_g[layer, 1]),
        jnp.tile(dil_qk_g[layer, 0], N_HEADS) * HEAD_DIM ** -0.5, jnp.tile(dil_qk_g[layer, 1], N_HEADS),
        jnp.tile(gqa_qk_g[layer, 0], N_HEADS) * HEAD_DIM ** -0.5, jnp.tile(gqa_qk_g[layer, 1], 2),
        dq * m0, dq * (1.0 - m0), jnp.tile(diff_qk_g[layer, 1], 2 * N_HEADS),
    ]
    gv = jnp.stack([_pad_row(r) for r in rows] + [jnp.zeros(D_MODEL, F32)] * (GAIN_ROWS - len(rows)))

    lambda_init = 0.8 - 0.6 * math.exp(-0.3 * layer)
    lv = diff_lambda[layer].astype(F32)
    lam = (jnp.exp(jnp.sum(lv[0] * lv[1])) - jnp.exp(jnp.sum(lv[2] * lv[3])) + lambda_init).reshape(1)
    sub_gain = (jnp.tile(diff_subln_g[layer], N_HEADS) * (1.0 - lambda_init)).reshape(1, 256)

    cv = jnp.stack([mix_beta[layer], norm2_g[layer]] + [jnp.zeros(D_MODEL, F32)] * 6)
    wr = jnp.concatenate([router_group_w[layer], router_expert_w[layer],
                          jnp.zeros((D_MODEL, LANES - 4 - N_EXPERTS), F32)], axis=1)
    wr_hi = _bf(wr)
    wr_lo = _bf(wr - wr_hi.astype(F32))
    rb = jnp.concatenate([router_group_b[layer], router_expert_b[layer],
                          jnp.zeros(LANES - 4 - N_EXPERTS, F32)]).reshape(1, LANES)
    return dict(w_in=w_p, wuq=wuq, wuk=wuk, wuv=wuv, gv=gv, lam=lam, sub_gain=sub_gain, cv=cv,
                w_out=_bf(w_out[layer]), wr_hi=wr_hi, wr_lo=wr_lo, rb=rb)


def kernel(x, rel_bias, norm1_g, w_in, mla_q_norm_g, mla_kv_norm_g, mla_w_uq, mla_w_ukv, mla_qk_g, dil_qk_g, gqa_qk_g, diff_qk_g, diff_lambda, diff_subln_g, mix_beta, w_out, norm2_g, router_group_w, router_group_b, router_expert_w, router_expert_b, expert_w_gate, expert_w_up, expert_w_down):
    batch, seq, d_model = x.shape
    assert seq == SEQ and d_model == D_MODEL
    depth = w_in.shape[0]
    t = batch * seq

    rope = _rope_tables()
    g32, g64, g128 = _block_ones(32), _block_ones(64), _block_ones(128)
    dil_bias = _dil_bias(rel_bias[:, :N_HEADS])
    diff_bias = _diff_bias_windows(rel_bias[:, N_HEADS:])

    x2d = x.reshape(t, D_MODEL)
    for layer in range(depth):
        p = _layer_params(layer, norm1_g, w_in, mla_q_norm_g, mla_kv_norm_g, mla_w_uq, mla_w_ukv, mla_qk_g,
                          dil_qk_g, gqa_qk_g, diff_qk_g, diff_lambda, diff_subln_g, mix_beta, w_out, norm2_g,
                          router_group_w, router_group_b, router_expert_w, router_expert_b)
        act, act4, act16 = _inproj(x2d, p["gv"], p["w_in"], p["wuq"], p["wuk"], p["wuv"], rope, g32, g64, g128)
        ya = _attn_full(act, batch, _attn_mla_kernel, A_QA, 512, A_KA, 512, A_VA, 256, "attn_mla")
        yb = _attn_dilated(act, act4, act16, batch, dil_bias)
        yc = _attn_full(act, batch, _attn_gqa_kernel, A_QC, 256, A_KC, 128, A_VC, 128, "attn_gqa")
        yd = _attn_diff(act, batch, p["lam"], diff_bias, p["sub_gain"])
        x1, h2, gate = _outproj(ya, yb, yc, yd, x2d, p["cv"], p["w_out"], p["wr_hi"], p["wr_lo"], p["rb"])
        x2d = _moe_dense(h2, gate, x1, _bf(expert_w_gate[layer]), _bf(expert_w_up[layer]),
                         _bf(expert_w_down[layer]))
    return x2d.reshape(batch, seq, D_MODEL)
```

```python
import functools
import math

import jax
import jax.numpy as jnp
import numpy as np
from jax import lax
from jax.experimental import pallas as pl
from jax.experimental.pallas import tpu as pltpu

F32 = jnp.float32
BF16 = jnp.bfloat16

D_MODEL = 1024
SEQ = 2048
HEAD_DIM = 64
GRID_W = 64
ROPE_THETA = 10000.0
NORM_EPS = 1e-6
NEG_INF = -1e30
NUM_BUCKETS = 32
REL_MAX_DISTANCE = 1024

N_HEADS = 4
MLA_NOPE_DIM = 64
MLA_ROPE_DIM = 32
MLA_QK_DIM = MLA_NOPE_DIM + MLA_ROPE_DIM
MLA_Q_RANK = 256
MLA_KV_RANK = 128
DIL_PATTERNS = ((128, 1), (512, 4), (2048, 16))
DIL_HALF = 64
DIFF_QK_DIM = 32
N_EXPERTS = 16
EXPERTS_PER_GROUP = 4
D_FF = 512

LANES = 128
MXU_DIM = 256
VMEM_LIMIT_BYTES = 56 * 1024 * 1024

P_CQ, P_CKV, P_KR = 0, 256, 384
P_BQ, P_BK, P_BV = 512, 768, 1024
P_CQ2, P_CK2, P_CV2 = 1280, 1536, 1664
P_DQ, P_DK, P_DV = 1792, 2048, 2304
PROJ_COLS = 2560

A_QB, A_KB, A_VB = 0, 256, 512
A_VA, A_QA, A_KA = 768, 1024, 1536
A_QC, A_KC, A_VC = 2048, 2304, 2432
A_QD0, A_QD1, A_KD, A_VD = 2560, 2816, 3072, 3328
ACT_COLS = 3584
DIL_COLS = 768

(G_NORM1, G_MLA_QN, G_MLA_KVN, G_MLA_Q, G_MLA_K, G_DIL_Q, G_DIL_K, G_GQA_Q, G_GQA_K,
 G_DIFF_Q0, G_DIFF_Q1, G_DIFF_K) = range(12)
GAIN_ROWS = 16

TM_PROJ = 256
TQ = 256
TM_OUT = 256
TM_MOE = 256
DIL_QBLK = 128

PAIRS_PER_GROUP = 6
N_CLASSES = 4 * PAIRS_PER_GROUP
CLS_ROWS = 32
XS_COLS = D_MODEL // 2 + LANES


def _bf(x):
    return x.astype(BF16)


def _dot(a, b):
    return jnp.dot(a, b, preferred_element_type=F32)


def _dot_nt(a, b):
    return lax.dot_general(a, b, (((1,), (1,)), ((), ())), preferred_element_type=F32)


def _rms(x, width):
    return x * lax.rsqrt(jnp.sum(x * x, axis=-1, keepdims=True) * (1.0 / width) + NORM_EPS)


def _group_sumsq(x, g):
    x2 = x * x
    hi = _bf(x2)
    lo = _bf(x2 - hi.astype(F32))
    w = g.shape[0]
    outs = []
    for c in range(x.shape[1] // w):
        sl = slice(w * c, w * (c + 1))
        outs.append(_dot(hi[:, sl], g) + _dot(lo[:, sl], g))
    return outs[0] if len(outs) == 1 else jnp.concatenate(outs, axis=1)


def _group_rms(x, g, group):
    return x * lax.rsqrt(_group_sumsq(x, g) * (1.0 / group) + NORM_EPS)


def _rope(x, c, s_next, s_prev, half):
    n = x.shape[1]
    return x * c + pltpu.roll(x, n - half, axis=1) * s_next + pltpu.roll(x, half, axis=1) * s_prev


def _inproj_kernel(x_ref, gv_ref, w_ref, wuq_ref, wuk_ref, wuv_ref, rope_ref, g32_ref, g64_ref, g128_ref,
                   o_ref, o4_ref, o16_ref, stage):
    def gain(row, width):
        return gv_ref[row:row + 1, 0:width]

    x = x_ref[...]
    h = _rms(x, D_MODEL) * gain(G_NORM1, D_MODEL)
    proj = _dot(_bf(h), w_ref[...])

    g32 = g32_ref[...]
    g64 = g64_ref[...]
    g128 = g128_ref[...]
    m_c = rope_ref[:, 0:128]
    m_sn = rope_ref[:, 128:256]
    m_sp = rope_ref[:, 256:384]
    a_c = rope_ref[:, 384:512]
    a_sn = rope_ref[:, 512:640]
    a_sp = rope_ref[:, 640:768]

    cq = _rms(proj[:, P_CQ:P_CQ + MLA_Q_RANK], MLA_Q_RANK) * gain(G_MLA_QN, MLA_Q_RANK)
    ckv = _rms(proj[:, P_CKV:P_CKV + MLA_KV_RANK], MLA_KV_RANK) * gain(G_MLA_KVN, MLA_KV_RANK)
    q = _dot(_bf(cq), wuq_ref[...])
    k_nope = _dot(_bf(ckv), wuk_ref[...])
    v = _dot(_bf(ckv), wuv_ref[...])
    k_rope = pltpu.roll(proj[:, P_KR:P_KR + LANES], MLA_NOPE_DIM, axis=1)
    k = k_nope + jnp.concatenate([k_rope] * N_HEADS, axis=1)
    qn = _group_rms(q, g128, MLA_QK_DIM) * gain(G_MLA_Q, 512)
    kn = _group_rms(k, g128, MLA_QK_DIM) * gain(G_MLA_K, 512)
    for g in range(N_HEADS):
        sl = slice(LANES * g, LANES * (g + 1))
        o_ref[:, A_QA + LANES * g:A_QA + LANES * (g + 1)] = _bf(_rope(qn[:, sl], m_c, m_sn, m_sp, MLA_ROPE_DIM // 2))
        o_ref[:, A_KA + LANES * g:A_KA + LANES * (g + 1)] = _bf(_rope(kn[:, sl], m_c, m_sn, m_sp, MLA_ROPE_DIM // 2))
    o_ref[:, A_VA:A_VA + 256] = _bf(v)

    qb = _group_rms(proj[:, P_BQ:P_BQ + 256], g64, HEAD_DIM) * gain(G_DIL_Q, 256)
    kb = _group_rms(proj[:, P_BK:P_BK + 256], g64, HEAD_DIM) * gain(G_DIL_K, 256)
    vb = proj[:, P_BV:P_BV + 256]
    o_ref[:, A_QB:A_QB + 256] = _bf(qb)
    o_ref[:, A_KB:A_KB + 256] = _bf(kb)
    o_ref[:, A_VB:A_VB + 256] = _bf(vb)
    for c, val in enumerate((qb, kb, vb)):
        stage[2 * c] = val[:, 0:LANES]
        stage[2 * c + 1] = val[:, LANES:2 * LANES]
    n_chunks = DIL_COLS // LANES
    for d, ref in ((4, o4_ref), (16, o16_ref)):
        n = x.shape[0] // d
        for r in range(d):
            ref[0, r] = _bf(jnp.concatenate(
                [stage[c, pl.ds(r, n, stride=d), :] for c in range(n_chunks)], axis=1))

    qc = _group_rms(proj[:, P_CQ2:P_CQ2 + 256], g64, HEAD_DIM) * gain(G_GQA_Q, 256)
    for g in range(2):
        sl = slice(LANES * g, LANES * (g + 1))
        o_ref[:, A_QC + LANES * g:A_QC + LANES * (g + 1)] = _bf(_rope(qc[:, sl], a_c, a_sn, a_sp, HEAD_DIM // 4))
    kc = _group_rms(proj[:, P_CK2:P_CK2 + 128], g64[0:128, 0:128], HEAD_DIM) * gain(G_GQA_K, 128)
    o_ref[:, A_KC:A_KC + 128] = _bf(_rope(kc, a_c, a_sn, a_sp, HEAD_DIM // 4))
    o_ref[:, A_VC:A_VC + 128] = _bf(proj[:, P_CV2:P_CV2 + 128])

    dqn = _group_rms(proj[:, P_DQ:P_DQ + 256], g32, DIFF_QK_DIM)
    o_ref[:, A_QD0:A_QD0 + 256] = _bf(dqn * gain(G_DIFF_Q0, 256))
    o_ref[:, A_QD1:A_QD1 + 256] = _bf(dqn * gain(G_DIFF_Q1, 256))
    o_ref[:, A_KD:A_KD + 256] = _bf(_group_rms(proj[:, P_DK:P_DK + 256], g32, DIFF_QK_DIM) * gain(G_DIFF_K, 256))
    o_ref[:, A_VD:A_VD + 256] = _bf(proj[:, P_DV:P_DV + 256])


def _inproj(x2d, gv, w_in, wuq, wuk, wuv, rope, g32, g64, g128):
    t = x2d.shape[0]
    tm = TM_PROJ
    n_pos = SEQ // tm
    const = lambda i: (0, 0)
    return pl.pallas_call(
        _inproj_kernel,
        grid=(t // tm,),
        in_specs=[
            pl.BlockSpec((tm, D_MODEL), lambda i: (i, 0)),
            pl.BlockSpec((GAIN_ROWS, D_MODEL), const),
            pl.BlockSpec((D_MODEL, PROJ_COLS), const),
            pl.BlockSpec((MLA_Q_RANK, 512), const),
            pl.BlockSpec((MLA_KV_RANK, 512), const),
            pl.BlockSpec((MLA_KV_RANK, 256), const),
            pl.BlockSpec((tm, 768), lambda i: (i % n_pos, 0)),
            pl.BlockSpec((MXU_DIM, MXU_DIM), const),
            pl.BlockSpec((MXU_DIM, MXU_DIM), const),
            pl.BlockSpec((MXU_DIM, MXU_DIM), const),
        ],
        out_specs=[
            pl.BlockSpec((tm, ACT_COLS), lambda i: (i, 0)),
            pl.BlockSpec((1, 4, tm // 4, DIL_COLS), lambda i: (i // n_pos, 0, i % n_pos, 0)),
            pl.BlockSpec((1, 16, tm // 16, DIL_COLS), lambda i: (i // n_pos, 0, i % n_pos, 0)),
        ],
        out_shape=[
            jax.ShapeDtypeStruct((t, ACT_COLS), BF16),
            jax.ShapeDtypeStruct((t // SEQ, 4, SEQ // 4, DIL_COLS), BF16),
            jax.ShapeDtypeStruct((t // SEQ, 16, SEQ // 16, DIL_COLS), BF16),
        ],
        scratch_shapes=[pltpu.VMEM((DIL_COLS // LANES, tm, LANES), F32)],
        compiler_params=pltpu.CompilerParams(dimension_semantics=("parallel",), vmem_limit_bytes=VMEM_LIMIT_BYTES),
        name="inproj_prep",
    )(x2d, gv, w_in, wuq, wuk, wuv, rope, g32, g64, g128)


def _softmax_pv(s, v):
    m = jnp.max(s, axis=-1, keepdims=True)
    e = jnp.exp(s - m)
    l = jnp.sum(e, axis=-1, keepdims=True)
    return _dot(_bf(e), v) / l


def _attn_mla_kernel(q_ref, k_ref, v_ref, o_ref):
    for h in range(N_HEADS):
        sl = slice(LANES * h, LANES * (h + 1))
        s = _dot_nt(q_ref[:, sl], k_ref[:, sl])
        o_ref[:, HEAD_DIM * h:HEAD_DIM * (h + 1)] = _softmax_pv(s, v_ref[:, HEAD_DIM * h:HEAD_DIM * (h + 1)])


def _attn_gqa_kernel(q_ref, k_ref, v_ref, o_ref):
    for h in range(N_HEADS):
        g = h // 2
        hs = slice(HEAD_DIM * h, HEAD_DIM * (h + 1))
        gs = slice(HEAD_DIM * g, HEAD_DIM * (g + 1))
        s = _dot_nt(q_ref[:, hs], k_ref[:, gs])
        o_ref[:, hs] = _softmax_pv(s, v_ref[:, gs])


def _attn_full(act, batch, kernel, q_col, q_w, k_col, k_w, v_col, v_w, name):
    t = act.shape[0]
    nq = SEQ // TQ
    return pl.pallas_call(
        kernel,
        grid=(batch, nq),
        in_specs=[
            pl.BlockSpec((TQ, q_w), lambda b, i: (b * nq + i, q_col // q_w)),
            pl.BlockSpec((SEQ, k_w), lambda b, i: (b, k_col // k_w)),
            pl.BlockSpec((SEQ, v_w), lambda b, i: (b, v_col // v_w)),
        ],
        out_specs=pl.BlockSpec((TQ, 256), lambda b, i: (b * nq + i, 0)),
        out_shape=jax.ShapeDtypeStruct((t, 256), F32),
        compiler_params=pltpu.CompilerParams(
            dimension_semantics=("parallel", "parallel"), vmem_limit_bytes=VMEM_LIMIT_BYTES),
        name=name,
    )(act, act, act)


def _attn_diff_kernel(lam_ref, q0_ref, q1_ref, k_ref, v_ref, win_ref, sg_ref, o_ref, bias_ref):
    @pl.when(pl.program_id(1) == 0)
    def _():
        for h in range(N_HEADS):
            w = jnp.broadcast_to(win_ref[0, h:h + 1, :], (TQ, SEQ + TQ))
            bias_ref[h] = pltpu.roll(w, 0, axis=1, stride=1, stride_axis=0)[:, TQ:TQ + SEQ]

    lam = lam_ref[0]
    for h in range(N_HEADS):
        hs = slice(HEAD_DIM * h, HEAD_DIM * (h + 1))
        k = k_ref[:, hs]
        bias = bias_ref[h]
        s0 = _dot_nt(q0_ref[:, hs], k) + bias
        s1 = _dot_nt(q1_ref[:, hs], k) + bias
        e0 = jnp.exp(s0 - jnp.max(s0, axis=-1, keepdims=True))
        e1 = jnp.exp(s1 - jnp.max(s1, axis=-1, keepdims=True))
        r0 = 1.0 / jnp.sum(e0, axis=-1, keepdims=True)
        r1 = lam / jnp.sum(e1, axis=-1, keepdims=True)
        attn = e0 * r0 - e1 * r1
        o = _dot(_bf(attn), v_ref[:, hs])
        o_ref[:, hs] = _rms(o, HEAD_DIM) * sg_ref[:, hs]


def _attn_diff(act, batch, lam, bias_win, sub_gain):
    t = act.shape[0]
    nq = SEQ // TQ
    return pl.pallas_call(
        _attn_diff_kernel,
        grid=(nq, batch),
        in_specs=[
            pl.BlockSpec(memory_space=pltpu.SMEM),
            pl.BlockSpec((TQ, 256), lambda i, b: (b * nq + i, A_QD0 // 256)),
            pl.BlockSpec((TQ, 256), lambda i, b: (b * nq + i, A_QD1 // 256)),
            pl.BlockSpec((SEQ, 256), lambda i, b: (b, A_KD // 256)),
            pl.BlockSpec((SEQ, 256), lambda i, b: (b, A_VD // 256)),
            pl.BlockSpec((1, N_HEADS, SEQ + TQ), lambda i, b: (i, 0, 0)),
            pl.BlockSpec((1, 256), lambda i, b: (0, 0)),
        ],
        out_specs=pl.BlockSpec((TQ, 256), lambda i, b: (b * nq + i, 0)),
        out_shape=jax.ShapeDtypeStruct((t, 256), F32),
        scratch_shapes=[pltpu.VMEM((N_HEADS, TQ, SEQ), F32)],
        compiler_params=pltpu.CompilerParams(
            dimension_semantics=("arbitrary", "arbitrary"), vmem_limit_bytes=VMEM_LIMIT_BYTES),
        name="attn_diff",
    )(lam, act, act, act, act, bias_win, sub_gain)


def _dil_heads(q_of, k_of, v_of, bias_of, in_seq):
    outs, lses = [], []
    for h in range(N_HEADS):
        hs = slice(HEAD_DIM * h, HEAD_DIM * (h + 1))
        s = _dot_nt(q_of(hs), k_of(hs)) + bias_of(h)
        if in_seq is not None:
            s = jnp.where(in_seq, s, NEG_INF)
        m = jnp.max(s, axis=-1, keepdims=True)
        e = jnp.exp(s - m)
        l = jnp.sum(e, axis=-1, keepdims=True)
        outs.append(_dot(_bf(e), v_of(hs)) / l)
        lses.append(jnp.broadcast_to(m + jnp.log(l), (DIL_QBLK, HEAD_DIM)))
    cat = lambda xs: [jnp.concatenate(xs[2 * c:2 * c + 2], axis=1) for c in range(2)]
    return cat(outs), cat(lses)


def _dil_kernel(n1_ref, r4_ref, r16_ref, bias_ref, o_ref, kpad, vpad, o1, l1, o4, l4, out):
    zeros = jnp.zeros((DIL_HALF, 256), BF16)
    col = lax.broadcasted_iota(jnp.int32, (1, 2 * DIL_QBLK), 1) - DIL_HALF

    def fill(k, v, n):
        kpad[0:DIL_HALF, :] = zeros
        vpad[0:DIL_HALF, :] = zeros
        kpad[DIL_HALF:DIL_HALF + n, :] = k
        vpad[DIL_HALF:DIL_HALF + n, :] = v
        kpad[DIL_HALF + n:2 * DIL_HALF + n, :] = zeros
        vpad[DIL_HALF + n:2 * DIL_HALF + n, :] = zeros

    def in_seq(j, n):
        key_pos = col + j * DIL_QBLK
        return jnp.logical_and(key_pos >= 0, key_pos < n)

    def windowed(q_of, r0, j, n, branch):
        win = pl.ds(r0, 2 * DIL_QBLK)
        return _dil_heads(q_of, lambda hs: kpad[win, hs], lambda hs: vpad[win, hs],
                          lambda h: bias_ref[branch, h], in_seq(j, n))

    fill(n1_ref[:, 256:512], n1_ref[:, 512:768], SEQ)

    def body1(j, carry):
        r0 = pl.multiple_of(j * DIL_QBLK, DIL_QBLK)
        rows = pl.ds(r0, DIL_QBLK)
        oc, lc = windowed(lambda hs: n1_ref[rows, hs], r0, j, SEQ, 0)
        for c in range(2):
            o1[c, rows, :] = oc[c]
            l1[c, rows, :] = lc[c]
        return carry
    lax.fori_loop(0, SEQ // DIL_QBLK, body1, 0)

    n4 = SEQ // 4
    for r in range(4):
        fill(r4_ref[0, r, :, 256:512], r4_ref[0, r, :, 512:768], n4)
        for j in range(n4 // DIL_QBLK):
            r0 = j * DIL_QBLK
            oc, lc = windowed(lambda hs: r4_ref[0, r, r0:r0 + DIL_QBLK, hs], r0, j, n4, 1)
            tok = pl.ds(4 * r0 + r, DIL_QBLK, stride=4)
            for c in range(2):
                o4[c, tok, :] = oc[c]
                l4[c, tok, :] = lc[c]

    def body16(r, carry):
        oc, lc = _dil_heads(lambda hs: r16_ref[0, r, :, hs],
                            lambda hs: r16_ref[0, r, :, 256 + hs.start:256 + hs.stop],
                            lambda hs: r16_ref[0, r, :, 512 + hs.start:512 + hs.stop],
                            lambda h: bias_ref[2, h, :, DIL_HALF:DIL_HALF + DIL_QBLK], None)
        tok = pl.ds(r, DIL_QBLK, stride=16)
        for c in range(2):
            la, lb = l1[c, tok, :], l4[c, tok, :]
            mx = jnp.maximum(jnp.maximum(la, lb), lc[c])
            wa, wb, wc = jnp.exp(la - mx), jnp.exp(lb - mx), jnp.exp(lc[c] - mx)
            den = wa + wb + wc
            out[c, tok, :] = (wa / den) * o1[c, tok, :] + (wb / den) * o4[c, tok, :] + (wc / den) * oc[c]
        return carry
    lax.fori_loop(0, 16, body16, 0)
    o_ref[:, 0:LANES] = out[0]
    o_ref[:, LANES:2 * LANES] = out[1]


def _attn_dilated(act, act4, act16, batch, bias_tabs):
    t = act.shape[0]
    nat = pltpu.VMEM((2, SEQ, LANES), F32)
    pad = pltpu.VMEM((SEQ + 2 * DIL_HALF, 256), BF16)
    return pl.pallas_call(
        _dil_kernel,
        grid=(batch,),
        in_specs=[
            pl.BlockSpec((SEQ, DIL_COLS), lambda b: (b, 0)),
            pl.BlockSpec((1, 4, SEQ // 4, DIL_COLS), lambda b: (b, 0, 0, 0)),
            pl.BlockSpec((1, 16, SEQ // 16, DIL_COLS), lambda b: (b, 0, 0, 0)),
            pl.BlockSpec((3, N_HEADS, DIL_QBLK, 2 * DIL_QBLK), lambda b: (0, 0, 0, 0)),
        ],
        out_specs=pl.BlockSpec((SEQ, 256), lambda b: (b, 0)),
        out_shape=jax.ShapeDtypeStruct((t, 256), F32),
        scratch_shapes=[pad, pad, nat, nat, nat, nat, nat],
        compiler_params=pltpu.CompilerParams(dimension_semantics=("parallel",), vmem_limit_bytes=VMEM_LIMIT_BYTES),
        name="attn_dilated",
    )(act, act4, act16, bias_tabs)


def _route(lt):
    g = [lt[i:i + 1, :] for i in range(4)]
    gmax = jnp.maximum(jnp.maximum(g[0], g[1]), jnp.maximum(g[2], g[3]))
    gsum = sum(jnp.exp(gi - gmax) for gi in g)
    g_w = 1.0 / gsum
    gidx = jnp.where(g[0] == gmax, 0, jnp.where(g[1] == gmax, 1, jnp.where(g[2] == gmax, 2, 3)))
    el = []
    for j in range(EXPERTS_PER_GROUP):
        acc = jnp.zeros_like(g[0])
        for i in range(4):
            r = 4 + EXPERTS_PER_GROUP * i + j
            acc = jnp.where(gidx == i, lt[r:r + 1, :], acc)
        el.append(acc)
    emax = jnp.maximum(jnp.maximum(el[0], el[1]), jnp.maximum(el[2], el[3]))
    ee = [jnp.exp(e - emax) for e in el]
    esum = ee[0] + ee[1] + ee[2] + ee[3]
    p = [e / esum for e in ee]
    p1 = jnp.maximum(jnp.maximum(p[0], p[1]), jnp.maximum(p[2], p[3]))
    i1 = jnp.where(p[0] == p1, 0, jnp.where(p[1] == p1, 1, jnp.where(p[2] == p1, 2, 3)))
    pm = [jnp.where(i1 == j, -1.0, p[j]) for j in range(4)]
    p2 = jnp.maximum(jnp.maximum(pm[0], pm[1]), jnp.maximum(pm[2], pm[3]))
    i2 = jnp.where(pm[0] == p2, 0, jnp.where(pm[1] == p2, 1, jnp.where(pm[2] == p2, 2, 3)))
    den = p1 + p2
    return gidx, i1, i2, g_w * (p1 / den), g_w * (p2 / den)


def _pair_offset(lo):
    return jnp.where(lo == 0, 0, jnp.where(lo == 1, 3, 5))


def _outproj_kernel(ya_ref, yb_ref, yc_ref, yd_ref, x_ref, cv_ref, w_ref, wrh_ref, wrl_ref, rb_ref, tri_ref,
                    x1_ref, hp_ref, ri_ref, cnt_out_ref, cnt_ref):
    tm = x_ref.shape[0]

    @pl.when(pl.program_id(0) == 0)
    def _():
        cnt_ref[...] = jnp.zeros_like(cnt_ref)

    mixed = jnp.concatenate(
        [_rms(ya_ref[...], 256), _rms(yb_ref[...], 256), _rms(yc_ref[...], 256), yd_ref[...]], axis=1)
    mixed = mixed * cv_ref[0:1, :]
    x1 = x_ref[...] + _dot(_bf(mixed), w_ref[...])
    x1_ref[...] = x1
    h2 = _rms(x1, D_MODEL) * cv_ref[1:2, :]
    h_hi = _bf(h2)
    h_rt = h_hi.astype(F32)
    h_lo = _bf(h2 - h_rt)
    logits = _dot(h_hi, wrh_ref[...]) + _dot(h_hi, wrl_ref[...]) + _dot(h_lo, wrh_ref[...]) + rb_ref[...]
    gidx, i1, i2, wa, wb = _route(logits.T)

    lo, hi = jnp.minimum(i1, i2), jnp.maximum(i1, i2)
    first_is_lo = i1 < i2
    w_lo, w_hi = jnp.where(first_is_lo, wa, wb), jnp.where(first_is_lo, wb, wa)
    cls = gidx * PAIRS_PER_GROUP + _pair_offset(lo) + (hi - lo - 1)

    rows = lax.broadcasted_iota(jnp.int32, (CLS_ROWS, tm), 0)
    onehot = rows == cls
    oh = jnp.where(onehot, 1.0, 0.0)
    prefix = _dot(_bf(oh), tri_ref[...])
    before = cnt_ref[:, 0:1] + prefix
    rank = jnp.sum(jnp.where(onehot, before, 0.0), axis=0, keepdims=True)
    cnt_ref[...] = cnt_ref[...] + jnp.sum(oh, axis=1, keepdims=True)
    cnt_out_ref[...] = cnt_ref[...]
    ri_ref[...] = jnp.concatenate([cls, rank.astype(jnp.int32), jnp.zeros((6, tm), jnp.int32)], axis=0)

    half = D_MODEL // 2
    lo_bits = lax.shift_right_logical(pltpu.bitcast(h_rt[:, 0:half], jnp.uint32), jnp.uint32(16))
    hi_bits = pltpu.bitcast(h_rt[:, half:D_MODEL], jnp.uint32) & jnp.uint32(0xFFFF0000)
    hp_ref[:, 0:half] = hi_bits | lo_bits
    rows_w = lax.broadcasted_iota(jnp.int32, (LANES, tm), 0)
    w_t = jnp.where(rows_w == 0, w_lo, jnp.where(rows_w == 1, w_hi, 0.0))
    hp_ref[:, half:XS_COLS] = pltpu.bitcast(w_t.T, jnp.uint32)


def _outproj(ya, yb, yc, yd, x2d, cv, w_out, wr_hi, wr_lo, rb, tri):
    t = x2d.shape[0]
    tm = TM_OUT
    row = lambda w: pl.BlockSpec((tm, w), lambda i: (i, 0))
    const = lambda shape: pl.BlockSpec(shape, lambda i: (0, 0))
    return pl.pallas_call(
        _outproj_kernel,
        grid=(t // tm,),
        in_specs=[row(256), row(256), row(256), row(256), row(D_MODEL), const((8, D_MODEL)),
                  const((D_MODEL, D_MODEL)), const((D_MODEL, LANES)), const((D_MODEL, LANES)), const((1, LANES)),
                  const((tm, tm))],
        out_specs=[row(D_MODEL), row(XS_COLS), pl.BlockSpec((8, tm), lambda i: (0, i)), const((CLS_ROWS, LANES))],
        out_shape=[jax.ShapeDtypeStruct((t, D_MODEL), F32), jax.ShapeDtypeStruct((t, XS_COLS), jnp.uint32),
                   jax.ShapeDtypeStruct((8, t), jnp.int32), jax.ShapeDtypeStruct((CLS_ROWS, LANES), F32)],
        scratch_shapes=[pltpu.VMEM((CLS_ROWS, LANES), F32)],
        compiler_params=pltpu.CompilerParams(dimension_semantics=("arbitrary",), vmem_limit_bytes=VMEM_LIMIT_BYTES),
        name="outproj_router",
    )(ya, yb, yc, yd, x2d, cv, w_out, wr_hi, wr_lo, rb, tri)


def _moe_plan(ri, counts, n_tiles):
    cls, rank = ri[0], ri[1]
    cnt = counts[:N_CLASSES, 0].astype(jnp.int32)
    padded = ((cnt + TM_MOE - 1) // TM_MOE) * TM_MOE
    ends = jnp.cumsum(padded)
    offs = ends - padded
    classes = jnp.arange(N_CLASSES, dtype=jnp.int32)
    dest = jnp.sum(jnp.where(cls[:, None] == classes[None, :], offs[None, :], 0), axis=1) + rank
    tile_start = jnp.arange(n_tiles, dtype=jnp.int32) * TM_MOE
    tile_cls = jnp.sum((tile_start[:, None] >= ends[None, :]).astype(jnp.int32), axis=1)
    tile_cls = jnp.minimum(tile_cls, N_CLASSES - 1)
    group, pair = tile_cls // PAIRS_PER_GROUP, tile_cls % PAIRS_PER_GROUP
    lo = (pair >= 3).astype(jnp.int32) + (pair >= 5).astype(jnp.int32)
    hi = pair - _pair_offset(lo) + lo + 1
    n_active = (ends[-1] // TM_MOE).reshape(1)
    return dest, group * EXPERTS_PER_GROUP + lo, group * EXPERTS_PER_GROUP + hi, n_active


def _row_copy(src_ref, src_row, dst_ref, dst_row, sem):
    return pltpu.make_async_copy(src_ref.at[pl.ds(src_row, 1), :], dst_ref.at[pl.ds(dst_row, 1), :], sem)


def _dispatch_kernel(dest_ref, h_ref, xs_in_ref, xs_ref, sem):
    del xs_in_ref
    tm = h_ref.shape[0]

    def start(r, carry):
        _row_copy(h_ref, r, xs_ref, dest_ref[0, 0, r], sem).start()
        return carry
    lax.fori_loop(0, tm, start, 0, unroll=8)

    def wait(r, carry):
        _row_copy(h_ref, 0, xs_ref, 0, sem).wait()
        return carry
    lax.fori_loop(0, tm, wait, 0)


def _dispatch(dest, hp, n_rows):
    t = hp.shape[0]
    tm = TM_MOE
    return pl.pallas_call(
        _dispatch_kernel,
        grid=(t // tm,),
        in_specs=[
            pl.BlockSpec((1, 1, tm), lambda i: (i, 0, 0), memory_space=pltpu.SMEM),
            pl.BlockSpec((tm, XS_COLS), lambda i: (i, 0)),
            pl.BlockSpec(memory_space=pl.ANY),
        ],
        out_specs=pl.BlockSpec(memory_space=pl.ANY),
        out_shape=jax.ShapeDtypeStruct((n_rows, XS_COLS), jnp.uint32),
        scratch_shapes=[pltpu.SemaphoreType.DMA],
        input_output_aliases={2: 0},
        compiler_params=pltpu.CompilerParams(dimension_semantics=("arbitrary",)),
        name="moe_dispatch",
    )(dest.reshape(t // tm, 1, tm), hp, jnp.zeros((n_rows, XS_COLS), jnp.uint32))


def _moe_group_kernel(elo_ref, ehi_ref, nact_ref, xs_ref, wg0, wu0, wd0, wg1, wu1, wd1, o_ref):
    del elo_ref, ehi_ref
    active = pl.program_id(0) < nact_ref[0]

    @pl.when(active)
    def _():
        half = D_MODEL // 2
        words = xs_ref[:, 0:half]
        x = jnp.concatenate([
            _bf(pltpu.bitcast(lax.shift_left(words, jnp.uint32(16)), F32)),
            _bf(pltpu.bitcast(words & jnp.uint32(0xFFFF0000), F32))], axis=1)
        gate = pltpu.bitcast(xs_ref[:, half:XS_COLS], F32)

        def expert(wg, wu, wd):
            a = _dot(x, wg[0])
            hid = a * (1.0 / (1.0 + jnp.exp(-a))) * _dot(x, wu[0])
            return _dot(_bf(hid), wd[0])
        o_ref[...] = gate[:, 0:1] * expert(wg0, wu0, wd0) + gate[:, 1:2] * expert(wg1, wu1, wd1)

    @pl.when(jnp.logical_not(active))
    def _():
        o_ref[...] = jnp.zeros_like(o_ref)


def _moe_group(e_lo, e_hi, n_active, xs, wg, wu, wd):
    n_rows = xs.shape[0]
    tm = TM_MOE
    w_in = lambda which: pl.BlockSpec((1, D_MODEL, D_FF), lambda i, lo, hi, na: ((lo, hi)[which][i], 0, 0))
    w_dn = lambda which: pl.BlockSpec((1, D_FF, D_MODEL), lambda i, lo, hi, na: ((lo, hi)[which][i], 0, 0))
    return pl.pallas_call(
        _moe_group_kernel,
        grid_spec=pltpu.PrefetchScalarGridSpec(
            num_scalar_prefetch=3,
            grid=(n_rows // tm,),
            in_specs=[pl.BlockSpec((tm, XS_COLS), lambda i, lo, hi, na: (i, 0)),
                      w_in(0), w_in(0), w_dn(0), w_in(1), w_in(1), w_dn(1)],
            out_specs=pl.BlockSpec((tm, D_MODEL), lambda i, lo, hi, na: (i, 0)),
        ),
        out_shape=jax.ShapeDtypeStruct((n_rows, D_MODEL), F32),
        compiler_params=pltpu.CompilerParams(dimension_semantics=("arbitrary",), vmem_limit_bytes=VMEM_LIMIT_BYTES),
        name="moe_group",
    )(e_lo, e_hi, n_active, xs, wg, wu, wd, wg, wu, wd)


def _combine_kernel(dest_ref, x1_ref, y_ref, o_ref, buf, sem):
    tm = x1_ref.shape[0]

    def start(r, carry):
        _row_copy(y_ref, dest_ref[0, 0, r], buf, r, sem).start()
        return carry
    lax.fori_loop(0, tm, start, 0, unroll=8)

    def wait(r, carry):
        _row_copy(y_ref, 0, buf, 0, sem).wait()
        return carry
    lax.fori_loop(0, tm, wait, 0)
    o_ref[...] = x1_ref[...] + buf[...]


def _combine(dest, x1, y):
    t = x1.shape[0]
    tm = TM_MOE
    return pl.pallas_call(
        _combine_kernel,
        grid=(t // tm,),
        in_specs=[
            pl.BlockSpec((1, 1, tm), lambda i: (i, 0, 0), memory_space=pltpu.SMEM),
            pl.BlockSpec((tm, D_MODEL), lambda i: (i, 0)),
            pl.BlockSpec(memory_space=pl.ANY),
        ],
        out_specs=pl.BlockSpec((tm, D_MODEL), lambda i: (i, 0)),
        out_shape=jax.ShapeDtypeStruct((t, D_MODEL), F32),
        scratch_shapes=[pltpu.VMEM((tm, D_MODEL), F32), pltpu.SemaphoreType.DMA],
        compiler_params=pltpu.CompilerParams(dimension_semantics=("arbitrary",)),
        name="moe_combine",
    )(dest.reshape(t // tm, 1, tm), x1, y)


def _rope_angles(pos, dim):
    inv = 1.0 / (ROPE_THETA ** (jnp.arange(0, dim, 2, dtype=F32) / dim))
    return pos.astype(F32)[:, None] * inv[None, :]


def _rope_tables():
    pos = jnp.arange(SEQ, dtype=jnp.int32)
    z = lambda w: jnp.zeros((SEQ, w), F32)
    ang = _rope_angles(pos, MLA_ROPE_DIM)
    c, s = jnp.cos(ang), jnp.sin(ang)
    m_c = jnp.concatenate([jnp.ones((SEQ, 64), F32), c, c, z(32)], axis=1)
    m_sn = jnp.concatenate([z(64), -s, z(48)], axis=1)
    m_sp = jnp.concatenate([z(80), s, z(32)], axis=1)
    row_pos = pos // GRID_W
    col_pos = pos - row_pos * GRID_W
    ra, ca = _rope_angles(row_pos, HEAD_DIM // 2), _rope_angles(col_pos, HEAD_DIM // 2)
    rc, rs, cc, cs = jnp.cos(ra), jnp.sin(ra), jnp.cos(ca), jnp.sin(ca)
    a_c = jnp.concatenate([rc, rc, cc, cc] * 2, axis=1)
    a_sn = jnp.concatenate([-rs, z(16), -cs, z(16)] * 2, axis=1)
    a_sp = jnp.concatenate([z(16), rs, z(16), cs] * 2, axis=1)
    return jnp.concatenate([m_c, m_sn, m_sp, a_c, a_sn, a_sp], axis=1)


def _rel_bucket(rel):
    half = NUM_BUCKETS // 2
    max_exact = half // 2
    n = jnp.abs(rel)
    nf = jnp.maximum(n, 1).astype(F32)
    log_ratio = jnp.log(nf / max_exact) / math.log(REL_MAX_DISTANCE / max_exact)
    large = jnp.minimum(max_exact + (log_ratio * (half - max_exact)).astype(jnp.int32), half - 1)
    return jnp.where(rel > 0, half, 0) + jnp.where(n < max_exact, n, large)


def _bias_lookup(table, rel):
    bucket = _rel_bucket(rel)
    onehot = bucket[None, ..., None] == jnp.arange(NUM_BUCKETS, dtype=jnp.int32)
    tab = table.T.reshape((table.shape[1],) + (1,) * rel.ndim + (NUM_BUCKETS,))
    return jnp.sum(jnp.where(onehot, tab, 0.0), axis=-1)


def _dil_bias(table):
    qi = jnp.arange(DIL_QBLK, dtype=jnp.int32)
    kj = jnp.arange(2 * DIL_QBLK, dtype=jnp.int32) - DIL_HALF
    rel = kj[None, :] - qi[:, None]
    band = (jnp.abs(rel) <= DIL_HALF)[None]
    return jnp.stack([jnp.where(band, _bias_lookup(table, rel * d), NEG_INF) for _, d in DIL_PATTERNS])


def _diff_bias_windows(table):
    rel = jnp.arange(2 * SEQ, dtype=jnp.int32) - SEQ
    line = _bias_lookup(table, rel)
    starts = [SEQ - (i + 1) * TQ for i in range(SEQ // TQ)]
    return jnp.stack([line[:, s0:s0 + SEQ + TQ] for s0 in starts])


def _block_ones(group):
    idx = np.arange(MXU_DIM) // group
    return jnp.asarray(idx[:, None] == idx[None, :], dtype=BF16)


def _pad_row(v):
    return jnp.pad(v.astype(F32), (0, D_MODEL - v.shape[0]))


def _layer_params(layer, norm1_g, w_in, mla_q_norm_g, mla_kv_norm_g, mla_w_uq, mla_w_ukv, mla_qk_g, dil_qk_g,
                  gqa_qk_g, diff_qk_g, diff_lambda, diff_subln_g, mix_beta, w_out, norm2_g, router_group_w,
                  router_group_b, router_expert_w, router_expert_b):
    w = w_in[layer]
    w_p = _bf(jnp.concatenate([w[:, :416], jnp.zeros((D_MODEL, 96), F32), w[:, 416:]], axis=1))
    uq = mla_w_uq[layer].reshape(MLA_Q_RANK, N_HEADS, MLA_QK_DIM)
    wuq = _bf(jnp.pad(uq, ((0, 0), (0, 0), (0, LANES - MLA_QK_DIM))).reshape(MLA_Q_RANK, 512))
    ukv = mla_w_ukv[layer].reshape(MLA_KV_RANK, N_HEADS, 2 * MLA_NOPE_DIM)
    wuk = _bf(jnp.pad(ukv[:, :, :MLA_NOPE_DIM], ((0, 0), (0, 0), (0, LANES - MLA_NOPE_DIM))).reshape(MLA_KV_RANK, 512))
    wuv = _bf(ukv[:, :, MLA_NOPE_DIM:].reshape(MLA_KV_RANK, 256))

    pad96 = lambda g: jnp.tile(jnp.pad(g, (0, LANES - MLA_QK_DIM)), N_HEADS)
    m0 = jnp.tile(jnp.concatenate([jnp.ones(32, F32), jnp.zeros(32, F32)]), N_HEADS)
    dq = jnp.tile(diff_qk_g[layer, 0], 2 * N_HEADS) * DIFF_QK_DIM ** -0.5
    rows = [
        norm1_g[layer], mla_q_norm_g[layer], mla_kv_norm_g[layer],
        pad96(mla_qk_g[layer, 0]) * MLA_QK_DIM ** -0.5, pad96(mla_qk_g[layer, 1]),
        jnp.tile(dil_qk_g[layer, 0], N_HEADS) * HEAD_DIM ** -0.5, jnp.tile(dil_qk_g[layer, 1], N_HEADS),
        jnp.tile(gqa_qk_g[layer, 0], N_HEADS) * HEAD_DIM ** -0.5, jnp.tile(gqa_qk_g[layer, 1], 2),
        dq * m0, dq * (1.0 - m0), jnp.tile(diff_qk_g[layer, 1], 2 * N_HEADS),
    ]
    gv = jnp.stack([_pad_row(r) for r in rows] + [jnp.zeros(D_MODEL, F32)] * (GAIN_ROWS - len(rows)))

    lambda_init = 0.8 - 0.6 * math.exp(-0.3 * layer)
    lv = diff_lambda[layer].astype(F32)
    lam = (jnp.exp(jnp.sum(lv[0] * lv[1])) - jnp.exp(jnp.sum(lv[2] * lv[3])) + lambda_init).reshape(1)
    sub_gain = (jnp.tile(diff_subln_g[layer], N_HEADS) * (1.0 - lambda_init)).reshape(1, 256)

    cv = jnp.stack([mix_beta[layer], norm2_g[layer]] + [jnp.zeros(D_MODEL, F32)] * 6)
    wr = jnp.concatenate([router_group_w[layer], router_expert_w[layer],
                          jnp.zeros((D_MODEL, LANES - 4 - N_EXPERTS), F32)], axis=1)
    wr_hi = _bf(wr)
    wr_lo = _bf(wr - wr_hi.astype(F32))
    rb = jnp.concatenate([router_group_b[layer], router_expert_b[layer],
                          jnp.zeros(LANES - 4 - N_EXPERTS, F32)]).reshape(1, LANES)
    return dict(w_in=w_p, wuq=wuq, wuk=wuk, wuv=wuv, gv=gv, lam=lam, sub_gain=sub_gain, cv=cv,
                w_out=_bf(w_out[layer]), wr_hi=wr_hi, wr_lo=wr_lo, rb=rb)


def kernel(x, rel_bias, norm1_g, w_in, mla_q_norm_g, mla_kv_norm_g, mla_w_uq, mla_w_ukv, mla_qk_g, dil_qk_g, gqa_qk_g, diff_qk_g, diff_lambda, diff_subln_g, mix_beta, w_out, norm2_g, router_group_w, router_group_b, router_expert_w, router_expert_b, expert_w_gate, expert_w_up, expert_w_down):
    batch, seq, d_model = x.shape
    assert seq == SEQ and d_model == D_MODEL
    depth = w_in.shape[0]
    t = batch * seq

    rope = _rope_tables()
    g32, g64, g128 = _block_ones(32), _block_ones(64), _block_ones(128)
    tri = jnp.asarray(np.arange(TM_OUT)[:, None] < np.arange(TM_OUT)[None, :], dtype=BF16)
    dil_bias = _dil_bias(rel_bias[:, :N_HEADS])
    diff_bias = _diff_bias_windows(rel_bias[:, N_HEADS:])

    x2d = x.reshape(t, D_MODEL)
    for layer in range(depth):
        p = _layer_params(layer, norm1_g, w_in, mla_q_norm_g, mla_kv_norm_g, mla_w_uq, mla_w_ukv, mla_qk_g,
                          dil_qk_g, gqa_qk_g, diff_qk_g, diff_lambda, diff_subln_g, mix_beta, w_out, norm2_g,
                          router_group_w, router_group_b, router_expert_w, router_expert_b)
        act, act4, act16 = _inproj(x2d, p["gv"], p["w_in"], p["wuq"], p["wuk"], p["wuv"], rope, g32, g64, g128)
        ya = _attn_full(act, batch, _attn_mla_kernel, A_QA, 512, A_KA, 512, A_VA, 256, "attn_mla")
        yb = _attn_dilated(act, act4, act16, batch, dil_bias)
        yc = _attn_full(act, batch, _attn_gqa_kernel, A_QC, 256, A_KC, 128, A_VC, 128, "attn_gqa")
        yd = _attn_diff(act, batch, p["lam"], diff_bias, p["sub_gain"])
        x1, hp, ri, counts = _outproj(ya, yb, yc, yd, x2d, p["cv"], p["w_out"], p["wr_hi"], p["wr_lo"], p["rb"], tri)
        n_tiles = t // TM_MOE + N_CLASSES
        dest, e_lo, e_hi, n_active = _moe_plan(ri, counts, n_tiles)
        xs = _dispatch(dest, hp, n_tiles * TM_MOE)
        y = _moe_group(e_lo, e_hi, n_active, xs, _bf(expert_w_gate[layer]), _bf(expert_w_up[layer]),
                       _bf(expert_w_down[layer]))
        x2d = _combine(dest, x1, y)
    return x2d.reshape(batch, seq, D_MODEL)
```

```python
import functools
import math

import jax
import jax.numpy as jnp
import numpy as np
from jax import lax
from jax.experimental import pallas as pl
from jax.experimental.pallas import tpu as pltpu

F32 = jnp.float32
BF16 = jnp.bfloat16

D_MODEL = 1024
SEQ = 2048
HEAD_DIM = 64
GRID_W = 64
ROPE_THETA = 10000.0
NORM_EPS = 1e-6
NEG_INF = -1e30
NUM_BUCKETS = 32
REL_MAX_DISTANCE = 1024

N_HEADS = 4
MLA_NOPE_DIM = 64
MLA_ROPE_DIM = 32
MLA_QK_DIM = MLA_NOPE_DIM + MLA_ROPE_DIM
MLA_Q_RANK = 256
MLA_KV_RANK = 128
DIL_PATTERNS = ((128, 1), (512, 4), (2048, 16))
DIL_HALF = 64
DIFF_QK_DIM = 32
N_EXPERTS = 16
EXPERTS_PER_GROUP = 4
D_FF = 512

LANES = 128
MXU_DIM = 256
VMEM_LIMIT_BYTES = 56 * 1024 * 1024

P_CQ, P_CKV, P_KR = 0, 256, 384
P_BQ, P_BK, P_BV = 512, 768, 1024
P_CQ2, P_CK2, P_CV2 = 1280, 1536, 1664
P_DQ, P_DK, P_DV = 1792, 2048, 2304
PROJ_COLS = 2560

A_QB, A_KB, A_VB = 0, 256, 512
A_VA, A_QA, A_KA = 768, 1024, 1536
A_QC, A_KC, A_VC = 2048, 2304, 2432
A_QD0, A_QD1, A_KD, A_VD = 2560, 2816, 3072, 3328
ACT_COLS = 3584
DIL_COLS = 768

(G_NORM1, G_MLA_QN, G_MLA_KVN, G_MLA_Q, G_MLA_K, G_DIL_Q, G_DIL_K, G_GQA_Q, G_GQA_K,
 G_DIFF_Q0, G_DIFF_Q1, G_DIFF_K) = range(12)
GAIN_ROWS = 16

TM_PROJ = 256
TQ = 256
TM_OUT = 256
TM_MOE = 256
DIL_QBLK = 128

PAIRS_PER_GROUP = 6
N_CLASSES = 4 * PAIRS_PER_GROUP
CLS_ROWS = 32
XS_COLS = D_MODEL + LANES


def _bf(x):
    return x.astype(BF16)


def _dot(a, b):
    return jnp.dot(a, b, preferred_element_type=F32)


def _dot_nt(a, b):
    return lax.dot_general(a, b, (((1,), (1,)), ((), ())), preferred_element_type=F32)


def _rms(x, width):
    return x * lax.rsqrt(jnp.sum(x * x, axis=-1, keepdims=True) * (1.0 / width) + NORM_EPS)


def _group_sumsq(x, g):
    x2 = x * x
    hi = _bf(x2)
    lo = _bf(x2 - hi.astype(F32))
    w = g.shape[0]
    outs = []
    for c in range(x.shape[1] // w):
        sl = slice(w * c, w * (c + 1))
        outs.append(_dot(hi[:, sl], g) + _dot(lo[:, sl], g))
    return outs[0] if len(outs) == 1 else jnp.concatenate(outs, axis=1)


def _group_rms(x, g, group):
    return x * lax.rsqrt(_group_sumsq(x, g) * (1.0 / group) + NORM_EPS)


def _rope(x, c, s_next, s_prev, half):
    n = x.shape[1]
    return x * c + pltpu.roll(x, n - half, axis=1) * s_next + pltpu.roll(x, half, axis=1) * s_prev


def _inproj_kernel(x_ref, gv_ref, w_ref, wuq_ref, wuk_ref, wuv_ref, rope_ref, g32_ref, g64_ref, g128_ref,
                   o_ref, o4_ref, o16_ref, stage):
    def gain(row, width):
        return gv_ref[row:row + 1, 0:width]

    x = x_ref[...]
    h = _rms(x, D_MODEL) * gain(G_NORM1, D_MODEL)
    proj = _dot(_bf(h), w_ref[...])

    g32 = g32_ref[...]
    g64 = g64_ref[...]
    g128 = g128_ref[...]
    m_c = rope_ref[:, 0:128]
    m_sn = rope_ref[:, 128:256]
    m_sp = rope_ref[:, 256:384]
    a_c = rope_ref[:, 384:512]
    a_sn = rope_ref[:, 512:640]
    a_sp = rope_ref[:, 640:768]

    cq = _rms(proj[:, P_CQ:P_CQ + MLA_Q_RANK], MLA_Q_RANK) * gain(G_MLA_QN, MLA_Q_RANK)
    ckv = _rms(proj[:, P_CKV:P_CKV + MLA_KV_RANK], MLA_KV_RANK) * gain(G_MLA_KVN, MLA_KV_RANK)
    q = _dot(_bf(cq), wuq_ref[...])
    k_nope = _dot(_bf(ckv), wuk_ref[...])
    v = _dot(_bf(ckv), wuv_ref[...])
    k_rope = pltpu.roll(proj[:, P_KR:P_KR + LANES], MLA_NOPE_DIM, axis=1)
    k = k_nope + jnp.concatenate([k_rope] * N_HEADS, axis=1)
    qn = _group_rms(q, g128, MLA_QK_DIM) * gain(G_MLA_Q, 512)
    kn = _group_rms(k, g128, MLA_QK_DIM) * gain(G_MLA_K, 512)
    for g in range(N_HEADS):
        sl = slice(LANES * g, LANES * (g + 1))
        o_ref[:, A_QA + LANES * g:A_QA + LANES * (g + 1)] = _bf(_rope(qn[:, sl], m_c, m_sn, m_sp, MLA_ROPE_DIM // 2))
        o_ref[:, A_KA + LANES * g:A_KA + LANES * (g + 1)] = _bf(_rope(kn[:, sl], m_c, m_sn, m_sp, MLA_ROPE_DIM // 2))
    o_ref[:, A_VA:A_VA + 256] = _bf(v)

    qb = _group_rms(proj[:, P_BQ:P_BQ + 256], g64, HEAD_DIM) * gain(G_DIL_Q, 256)
    kb = _group_rms(proj[:, P_BK:P_BK + 256], g64, HEAD_DIM) * gain(G_DIL_K, 256)
    vb = proj[:, P_BV:P_BV + 256]
    o_ref[:, A_QB:A_QB + 256] = _bf(qb)
    o_ref[:, A_KB:A_KB + 256] = _bf(kb)
    o_ref[:, A_VB:A_VB + 256] = _bf(vb)
    for c, val in enumerate((qb, kb, vb)):
        stage[2 * c] = val[:, 0:LANES]
        stage[2 * c + 1] = val[:, LANES:2 * LANES]
    n_chunks = DIL_COLS // LANES
    for d, ref in ((4, o4_ref), (16, o16_ref)):
        n = x.shape[0] // d
        for r in range(d):
            ref[0, r] = _bf(jnp.concatenate(
                [stage[c, pl.ds(r, n, stride=d), :] for c in range(n_chunks)], axis=1))

    qc = _group_rms(proj[:, P_CQ2:P_CQ2 + 256], g64, HEAD_DIM) * gain(G_GQA_Q, 256)
    for g in range(2):
        sl = slice(LANES * g, LANES * (g + 1))
        o_ref[:, A_QC + LANES * g:A_QC + LANES * (g + 1)] = _bf(_rope(qc[:, sl], a_c, a_sn, a_sp, HEAD_DIM // 4))
    kc = _group_rms(proj[:, P_CK2:P_CK2 + 128], g64[0:128, 0:128], HEAD_DIM) * gain(G_GQA_K, 128)
    o_ref[:, A_KC:A_KC + 128] = _bf(_rope(kc, a_c, a_sn, a_sp, HEAD_DIM // 4))
    o_ref[:, A_VC:A_VC + 128] = _bf(proj[:, P_CV2:P_CV2 + 128])

    dqn = _group_rms(proj[:, P_DQ:P_DQ + 256], g32, DIFF_QK_DIM)
    o_ref[:, A_QD0:A_QD0 + 256] = _bf(dqn * gain(G_DIFF_Q0, 256))
    o_ref[:, A_QD1:A_QD1 + 256] = _bf(dqn * gain(G_DIFF_Q1, 256))
    o_ref[:, A_KD:A_KD + 256] = _bf(_group_rms(proj[:, P_DK:P_DK + 256], g32, DIFF_QK_DIM) * gain(G_DIFF_K, 256))
    o_ref[:, A_VD:A_VD + 256] = _bf(proj[:, P_DV:P_DV + 256])


def _inproj(x2d, gv, w_in, wuq, wuk, wuv, rope, g32, g64, g128):
    t = x2d.shape[0]
    tm = TM_PROJ
    n_pos = SEQ // tm
    const = lambda i: (0, 0)
    return pl.pallas_call(
        _inproj_kernel,
        grid=(t // tm,),
        in_specs=[
            pl.BlockSpec((tm, D_MODEL), lambda i: (i, 0)),
            pl.BlockSpec((GAIN_ROWS, D_MODEL), const),
            pl.BlockSpec((D_MODEL, PROJ_COLS), const),
            pl.BlockSpec((MLA_Q_RANK, 512), const),
            pl.BlockSpec((MLA_KV_RANK, 512), const),
            pl.BlockSpec((MLA_KV_RANK, 256), const),
            pl.BlockSpec((tm, 768), lambda i: (i % n_pos, 0)),
            pl.BlockSpec((MXU_DIM, MXU_DIM), const),
            pl.BlockSpec((MXU_DIM, MXU_DIM), const),
            pl.BlockSpec((MXU_DIM, MXU_DIM), const),
        ],
        out_specs=[
            pl.BlockSpec((tm, ACT_COLS), lambda i: (i, 0)),
            pl.BlockSpec((1, 4, tm // 4, DIL_COLS), lambda i: (i // n_pos, 0, i % n_pos, 0)),
            pl.BlockSpec((1, 16, tm // 16, DIL_COLS), lambda i: (i // n_pos, 0, i % n_pos, 0)),
        ],
        out_shape=[
            jax.ShapeDtypeStruct((t, ACT_COLS), BF16),
            jax.ShapeDtypeStruct((t // SEQ, 4, SEQ // 4, DIL_COLS), BF16),
            jax.ShapeDtypeStruct((t // SEQ, 16, SEQ // 16, DIL_COLS), BF16),
        ],
        scratch_shapes=[pltpu.VMEM((DIL_COLS // LANES, tm, LANES), F32)],
        compiler_params=pltpu.CompilerParams(dimension_semantics=("parallel",), vmem_limit_bytes=VMEM_LIMIT_BYTES),
        name="inproj_prep",
    )(x2d, gv, w_in, wuq, wuk, wuv, rope, g32, g64, g128)


def _softmax_pv(s, v):
    m = jnp.max(s, axis=-1, keepdims=True)
    e = jnp.exp(s - m)
    l = jnp.sum(e, axis=-1, keepdims=True)
    return _dot(_bf(e), v) / l


def _attn_mla_kernel(q_ref, k_ref, v_ref, o_ref):
    for h in range(N_HEADS):
        sl = slice(LANES * h, LANES * (h + 1))
        s = _dot_nt(q_ref[:, sl], k_ref[:, sl])
        o_ref[:, HEAD_DIM * h:HEAD_DIM * (h + 1)] = _softmax_pv(s, v_ref[:, HEAD_DIM * h:HEAD_DIM * (h + 1)])


def _attn_gqa_kernel(q_ref, k_ref, v_ref, o_ref):
    for h in range(N_HEADS):
        g = h // 2
        hs = slice(HEAD_DIM * h, HEAD_DIM * (h + 1))
        gs = slice(HEAD_DIM * g, HEAD_DIM * (g + 1))
        s = _dot_nt(q_ref[:, hs], k_ref[:, gs])
        o_ref[:, hs] = _softmax_pv(s, v_ref[:, gs])


def _attn_full(act, batch, kernel, q_col, q_w, k_col, k_w, v_col, v_w, name):
    t = act.shape[0]
    nq = SEQ // TQ
    return pl.pallas_call(
        kernel,
        grid=(batch, nq),
        in_specs=[
            pl.BlockSpec((TQ, q_w), lambda b, i: (b * nq + i, q_col // q_w)),
            pl.BlockSpec((SEQ, k_w), lambda b, i: (b, k_col // k_w)),
            pl.BlockSpec((SEQ, v_w), lambda b, i: (b, v_col // v_w)),
        ],
        out_specs=pl.BlockSpec((TQ, 256), lambda b, i: (b * nq + i, 0)),
        out_shape=jax.ShapeDtypeStruct((t, 256), F32),
        compiler_params=pltpu.CompilerParams(
            dimension_semantics=("parallel", "parallel"), vmem_limit_bytes=VMEM_LIMIT_BYTES),
        name=name,
    )(act, act, act)


def _attn_diff_kernel(lam_ref, q0_ref, q1_ref, k_ref, v_ref, win_ref, sg_ref, o_ref, bias_ref):
    @pl.when(pl.program_id(1) == 0)
    def _():
        for h in range(N_HEADS):
            w = jnp.broadcast_to(win_ref[0, h:h + 1, :], (TQ, SEQ + TQ))
            bias_ref[h] = pltpu.roll(w, 0, axis=1, stride=1, stride_axis=0)[:, TQ:TQ + SEQ]

    lam = lam_ref[0]
    for h in range(N_HEADS):
        hs = slice(HEAD_DIM * h, HEAD_DIM * (h + 1))
        k = k_ref[:, hs]
        bias = bias_ref[h]
        s0 = _dot_nt(q0_ref[:, hs], k) + bias
        s1 = _dot_nt(q1_ref[:, hs], k) + bias
        e0 = jnp.exp(s0 - jnp.max(s0, axis=-1, keepdims=True))
        e1 = jnp.exp(s1 - jnp.max(s1, axis=-1, keepdims=True))
        r0 = 1.0 / jnp.sum(e0, axis=-1, keepdims=True)
        r1 = lam / jnp.sum(e1, axis=-1, keepdims=True)
        attn = e0 * r0 - e1 * r1
        o = _dot(_bf(attn), v_ref[:, hs])
        o_ref[:, hs] = _rms(o, HEAD_DIM) * sg_ref[:, hs]


def _attn_diff(act, batch, lam, bias_win, sub_gain):
    t = act.shape[0]
    nq = SEQ // TQ
    return pl.pallas_call(
        _attn_diff_kernel,
        grid=(nq, batch),
        in_specs=[
            pl.BlockSpec(memory_space=pltpu.SMEM),
            pl.BlockSpec((TQ, 256), lambda i, b: (b * nq + i, A_QD0 // 256)),
            pl.BlockSpec((TQ, 256), lambda i, b: (b * nq + i, A_QD1 // 256)),
            pl.BlockSpec((SEQ, 256), lambda i, b: (b, A_KD // 256)),
            pl.BlockSpec((SEQ, 256), lambda i, b: (b, A_VD // 256)),
            pl.BlockSpec((1, N_HEADS, SEQ + TQ), lambda i, b: (i, 0, 0)),
            pl.BlockSpec((1, 256), lambda i, b: (0, 0)),
        ],
        out_specs=pl.BlockSpec((TQ, 256), lambda i, b: (b * nq + i, 0)),
        out_shape=jax.ShapeDtypeStruct((t, 256), F32),
        scratch_shapes=[pltpu.VMEM((N_HEADS, TQ, SEQ), F32)],
        compiler_params=pltpu.CompilerParams(
            dimension_semantics=("arbitrary", "arbitrary"), vmem_limit_bytes=VMEM_LIMIT_BYTES),
        name="attn_diff",
    )(lam, act, act, act, act, bias_win, sub_gain)


def _dil_branch(q_of, k_of, v_of, bias_of, n_seq):
    outs, lses = [], []
    for h in range(N_HEADS):
        hs = slice(HEAD_DIM * h, HEAD_DIM * (h + 1))
        q, k, v = q_of(hs), k_of(hs), v_of(hs)
        n = q.shape[1]
        if n == DIL_QBLK:
            kw, vw = k, v
            bias = bias_of(h)[:, DIL_HALF:DIL_HALF + DIL_QBLK]
            edge = None
        else:
            n_blk = n // DIL_QBLK

            def windows(x):
                zeros = jnp.zeros((n_seq, DIL_HALF, HEAD_DIM), BF16)
                xp = jnp.concatenate([zeros, x, zeros], axis=1)
                first = xp[:, 0:n].reshape(n_seq * n_blk, DIL_QBLK, HEAD_DIM)
                second = xp[:, DIL_QBLK:DIL_QBLK + n].reshape(n_seq * n_blk, DIL_QBLK, HEAD_DIM)
                return jnp.concatenate([first, second], axis=1)
            kw, vw = windows(k), windows(v)
            q = q.reshape(n_seq * n_blk, DIL_QBLK, HEAD_DIM)
            bias = bias_of(h)
            blk = lax.broadcasted_iota(jnp.int32, (n_seq * n_blk, 1, 2 * DIL_QBLK), 0) % n_blk
            col = lax.broadcasted_iota(jnp.int32, (n_seq * n_blk, 1, 2 * DIL_QBLK), 2)
            key_pos = col - DIL_HALF + blk * DIL_QBLK
            edge = jnp.where(jnp.logical_and(key_pos >= 0, key_pos < n), 0.0, NEG_INF)
        s = jnp.einsum("bqe,bke->bqk", q, kw, preferred_element_type=F32) + bias[None]
        if edge is not None:
            s = s + edge
        m = jnp.max(s, axis=-1, keepdims=True)
        e = jnp.exp(s - m)
        l = jnp.sum(e, axis=-1, keepdims=True)
        o = jnp.einsum("bqk,bke->bqe", _bf(e), vw, preferred_element_type=F32) / l
        lse = jnp.broadcast_to(m + jnp.log(l), o.shape)
        outs.append(o.reshape(n_seq, n, HEAD_DIM))
        lses.append(lse.reshape(n_seq, n, HEAD_DIM))
    cat = lambda xs: [jnp.concatenate(xs[2 * c:2 * c + 2], axis=2) for c in range(2)]
    return cat(outs), cat(lses)


def _dil_kernel(n1_ref, r4_ref, r16_ref, bias_ref, o_ref, o1, l1, o4, l4, out):
    col_of = lambda base: (lambda hs: slice(base + hs.start, base + hs.stop))
    q_cols, k_cols, v_cols = col_of(0), col_of(256), col_of(512)

    oc, lc = _dil_branch(lambda hs: n1_ref[:, q_cols(hs)][None], lambda hs: n1_ref[:, k_cols(hs)][None],
                         lambda hs: n1_ref[:, v_cols(hs)][None], lambda h: bias_ref[0, h], 1)
    for c in range(2):
        o1[c] = oc[c][0]
        l1[c] = lc[c][0]

    oc, lc = _dil_branch(lambda hs: r4_ref[0, :, :, q_cols(hs)], lambda hs: r4_ref[0, :, :, k_cols(hs)],
                         lambda hs: r4_ref[0, :, :, v_cols(hs)], lambda h: bias_ref[1, h], 4)
    for c in range(2):
        for r in range(4):
            tok = pl.ds(r, SEQ // 4, stride=4)
            o4[c, tok, :] = oc[c][r]
            l4[c, tok, :] = lc[c][r]

    oc, lc = _dil_branch(lambda hs: r16_ref[0, :, :, q_cols(hs)], lambda hs: r16_ref[0, :, :, k_cols(hs)],
                         lambda hs: r16_ref[0, :, :, v_cols(hs)], lambda h: bias_ref[2, h], 16)
    toks = [pl.ds(r, SEQ // 16, stride=16) for r in range(16)]
    for c in range(2):
        gather = lambda ref: jnp.stack([ref[c, tok, :] for tok in toks])
        la, lb = gather(l1), gather(l4)
        mx = jnp.maximum(jnp.maximum(la, lb), lc[c])
        wa, wb, wc = jnp.exp(la - mx), jnp.exp(lb - mx), jnp.exp(lc[c] - mx)
        den = wa + wb + wc
        res = (wa / den) * gather(o1) + (wb / den) * gather(o4) + (wc / den) * oc[c]
        for r, tok in enumerate(toks):
            out[c, tok, :] = res[r]
    o_ref[:, 0:LANES] = out[0]
    o_ref[:, LANES:2 * LANES] = out[1]


def _attn_dilated(act, act4, act16, batch, bias_tabs):
    t = act.shape[0]
    nat = pltpu.VMEM((2, SEQ, LANES), F32)
    return pl.pallas_call(
        _dil_kernel,
        grid=(batch,),
        in_specs=[
            pl.BlockSpec((SEQ, DIL_COLS), lambda b: (b, 0)),
            pl.BlockSpec((1, 4, SEQ // 4, DIL_COLS), lambda b: (b, 0, 0, 0)),
            pl.BlockSpec((1, 16, SEQ // 16, DIL_COLS), lambda b: (b, 0, 0, 0)),
            pl.BlockSpec((3, N_HEADS, DIL_QBLK, 2 * DIL_QBLK), lambda b: (0, 0, 0, 0)),
        ],
        out_specs=pl.BlockSpec((SEQ, 256), lambda b: (b, 0)),
        out_shape=jax.ShapeDtypeStruct((t, 256), F32),
        scratch_shapes=[nat, nat, nat, nat, nat],
        compiler_params=pltpu.CompilerParams(dimension_semantics=("parallel",), vmem_limit_bytes=VMEM_LIMIT_BYTES),
        name="attn_dilated",
    )(act, act4, act16, bias_tabs)


def _route(lt):
    g = [lt[i:i + 1, :] for i in range(4)]
    gmax = jnp.maximum(jnp.maximum(g[0], g[1]), jnp.maximum(g[2], g[3]))
    gsum = sum(jnp.exp(gi - gmax) for gi in g)
    g_w = 1.0 / gsum
    gidx = jnp.where(g[0] == gmax, 0, jnp.where(g[1] == gmax, 1, jnp.where(g[2] == gmax, 2, 3)))
    el = []
    for j in range(EXPERTS_PER_GROUP):
        acc = jnp.zeros_like(g[0])
        for i in range(4):
            r = 4 + EXPERTS_PER_GROUP * i + j
            acc = jnp.where(gidx == i, lt[r:r + 1, :], acc)
        el.append(acc)
    emax = jnp.maximum(jnp.maximum(el[0], el[1]), jnp.maximum(el[2], el[3]))
    ee = [jnp.exp(e - emax) for e in el]
    esum = ee[0] + ee[1] + ee[2] + ee[3]
    p = [e / esum for e in ee]
    p1 = jnp.maximum(jnp.maximum(p[0], p[1]), jnp.maximum(p[2], p[3]))
    i1 = jnp.where(p[0] == p1, 0, jnp.where(p[1] == p1, 1, jnp.where(p[2] == p1, 2, 3)))
    pm = [jnp.where(i1 == j, -1.0, p[j]) for j in range(4)]
    p2 = jnp.maximum(jnp.maximum(pm[0], pm[1]), jnp.maximum(pm[2], pm[3]))
    i2 = jnp.where(pm[0] == p2, 0, jnp.where(pm[1] == p2, 1, jnp.where(pm[2] == p2, 2, 3)))
    den = p1 + p2
    return gidx, i1, i2, g_w * (p1 / den), g_w * (p2 / den)


def _pair_offset(lo):
    return jnp.where(lo == 0, 0, jnp.where(lo == 1, 3, 5))


def _outproj_kernel(ya_ref, yb_ref, yc_ref, yd_ref, x_ref, cv_ref, w_ref, wrh_ref, wrl_ref, rb_ref, tri_ref,
                    x1_ref, hp_ref, ri_ref, cnt_out_ref, cnt_ref):
    tm = x_ref.shape[0]

    @pl.when(pl.program_id(0) == 0)
    def _():
        cnt_ref[...] = jnp.zeros_like(cnt_ref)

    mixed = jnp.concatenate(
        [_rms(ya_ref[...], 256), _rms(yb_ref[...], 256), _rms(yc_ref[...], 256), yd_ref[...]], axis=1)
    mixed = mixed * cv_ref[0:1, :]
    x1 = x_ref[...] + _dot(_bf(mixed), w_ref[...])
    x1_ref[...] = x1
    h2 = _rms(x1, D_MODEL) * cv_ref[1:2, :]
    h_hi = _bf(h2)
    h_rt = h_hi.astype(F32)
    h_lo = _bf(h2 - h_rt)
    logits = _dot(h_hi, wrh_ref[...]) + _dot(h_hi, wrl_ref[...]) + _dot(h_lo, wrh_ref[...]) + rb_ref[...]
    gidx, i1, i2, wa, wb = _route(logits.T)

    lo, hi = jnp.minimum(i1, i2), jnp.maximum(i1, i2)
    first_is_lo = i1 < i2
    w_lo, w_hi = jnp.where(first_is_lo, wa, wb), jnp.where(first_is_lo, wb, wa)
    cls = gidx * PAIRS_PER_GROUP + _pair_offset(lo) + (hi - lo - 1)

    rows = lax.broadcasted_iota(jnp.int32, (CLS_ROWS, tm), 0)
    onehot = rows == cls
    oh = jnp.where(onehot, 1.0, 0.0)
    prefix = _dot(_bf(oh), tri_ref[...])
    before = cnt_ref[:, 0:1] + prefix
    rank = jnp.sum(jnp.where(onehot, before, 0.0), axis=0, keepdims=True)
    cnt_ref[...] = cnt_ref[...] + jnp.sum(oh, axis=1, keepdims=True)
    cnt_out_ref[...] = cnt_ref[...]
    ri_ref[...] = jnp.concatenate([cls, rank.astype(jnp.int32), jnp.zeros((6, tm), jnp.int32)], axis=0)

    hp_ref[:, 0:D_MODEL] = h_rt
    rows_w = lax.broadcasted_iota(jnp.int32, (LANES, tm), 0)
    w_t = jnp.where(rows_w == 0, w_lo, jnp.where(rows_w == 1, w_hi, 0.0))
    hp_ref[:, D_MODEL:XS_COLS] = w_t.T


def _outproj(ya, yb, yc, yd, x2d, cv, w_out, wr_hi, wr_lo, rb, tri):
    t = x2d.shape[0]
    tm = TM_OUT
    row = lambda w: pl.BlockSpec((tm, w), lambda i: (i, 0))
    const = lambda shape: pl.BlockSpec(shape, lambda i: (0, 0))
    return pl.pallas_call(
        _outproj_kernel,
        grid=(t // tm,),
        in_specs=[row(256), row(256), row(256), row(256), row(D_MODEL), const((8, D_MODEL)),
                  const((D_MODEL, D_MODEL)), const((D_MODEL, LANES)), const((D_MODEL, LANES)), const((1, LANES)),
                  const((tm, tm))],
        out_specs=[row(D_MODEL), row(XS_COLS), pl.BlockSpec((8, tm), lambda i: (0, i)), const((CLS_ROWS, LANES))],
        out_shape=[jax.ShapeDtypeStruct((t, D_MODEL), F32), jax.ShapeDtypeStruct((t, XS_COLS), F32),
                   jax.ShapeDtypeStruct((8, t), jnp.int32), jax.ShapeDtypeStruct((CLS_ROWS, LANES), F32)],
        scratch_shapes=[pltpu.VMEM((CLS_ROWS, LANES), F32)],
        compiler_params=pltpu.CompilerParams(dimension_semantics=("arbitrary",), vmem_limit_bytes=VMEM_LIMIT_BYTES),
        name="outproj_router",
    )(ya, yb, yc, yd, x2d, cv, w_out, wr_hi, wr_lo, rb, tri)


def _moe_plan(ri, counts, n_tiles):
    cls, rank = ri[0], ri[1]
    cnt = counts[:N_CLASSES, 0].astype(jnp.int32)
    padded = ((cnt + TM_MOE - 1) // TM_MOE) * TM_MOE
    ends = jnp.cumsum(padded)
    offs = ends - padded
    classes = jnp.arange(N_CLASSES, dtype=jnp.int32)
    dest = jnp.sum(jnp.where(cls[:, None] == classes[None, :], offs[None, :], 0), axis=1) + rank
    tile_start = jnp.arange(n_tiles, dtype=jnp.int32) * TM_MOE
    tile_cls = jnp.sum((tile_start[:, None] >= ends[None, :]).astype(jnp.int32), axis=1)
    tile_cls = jnp.minimum(tile_cls, N_CLASSES - 1)
    group, pair = tile_cls // PAIRS_PER_GROUP, tile_cls % PAIRS_PER_GROUP
    lo = (pair >= 3).astype(jnp.int32) + (pair >= 5).astype(jnp.int32)
    hi = pair - _pair_offset(lo) + lo + 1
    n_active = (ends[-1] // TM_MOE).reshape(1)
    return dest, group * EXPERTS_PER_GROUP + lo, group * EXPERTS_PER_GROUP + hi, n_active


def _row_copy(src_ref, src_row, dst_ref, dst_row, sem):
    return pltpu.make_async_copy(src_ref.at[pl.ds(src_row, 1), :], dst_ref.at[pl.ds(dst_row, 1), :], sem)


def _dispatch_kernel(dest_ref, h_ref, xs_in_ref, xs_ref, sem):
    del xs_in_ref
    tm = h_ref.shape[0]

    def start(r, carry):
        _row_copy(h_ref, r, xs_ref, dest_ref[0, 0, r], sem).start()
        return carry
    lax.fori_loop(0, tm, start, 0, unroll=8)

    pltpu.make_async_copy(h_ref, xs_ref.at[pl.ds(0, tm), :], sem).wait()


def _dispatch(dest, hp, n_rows):
    t = hp.shape[0]
    tm = TM_MOE
    return pl.pallas_call(
        _dispatch_kernel,
        grid=(t // tm,),
        in_specs=[
            pl.BlockSpec((1, 1, tm), lambda i: (i, 0, 0), memory_space=pltpu.SMEM),
            pl.BlockSpec((tm, XS_COLS), lambda i: (i, 0)),
            pl.BlockSpec(memory_space=pl.ANY),
        ],
        out_specs=pl.BlockSpec(memory_space=pl.ANY),
        out_shape=jax.ShapeDtypeStruct((n_rows, XS_COLS), F32),
        scratch_shapes=[pltpu.SemaphoreType.DMA],
        input_output_aliases={2: 0},
        compiler_params=pltpu.CompilerParams(dimension_semantics=("arbitrary",)),
        name="moe_dispatch",
    )(dest.reshape(t // tm, 1, tm), hp, jnp.zeros((n_rows, XS_COLS), F32))


def _moe_group_kernel(elo_ref, ehi_ref, nact_ref, xs_ref, wg0, wu0, wd0, wg1, wu1, wd1, o_ref):
    del elo_ref, ehi_ref
    active = pl.program_id(0) < nact_ref[0]

    @pl.when(active)
    def _():
        x = _bf(xs_ref[:, 0:D_MODEL])
        gate = xs_ref[:, D_MODEL:XS_COLS]

        def expert(wg, wu, wd):
            a = _dot(x, wg[0])
            hid = a * (1.0 / (1.0 + jnp.exp(-a))) * _dot(x, wu[0])
            return _dot(_bf(hid), wd[0])
        o_ref[...] = gate[:, 0:1] * expert(wg0, wu0, wd0) + gate[:, 1:2] * expert(wg1, wu1, wd1)

    @pl.when(jnp.logical_not(active))
    def _():
        o_ref[...] = jnp.zeros_like(o_ref)


def _moe_group(e_lo, e_hi, n_active, xs, wg, wu, wd):
    n_rows = xs.shape[0]
    tm = TM_MOE
    w_in = lambda which: pl.BlockSpec((1, D_MODEL, D_FF), lambda i, lo, hi, na: ((lo, hi)[which][i], 0, 0))
    w_dn = lambda which: pl.BlockSpec((1, D_FF, D_MODEL), lambda i, lo, hi, na: ((lo, hi)[which][i], 0, 0))
    return pl.pallas_call(
        _moe_group_kernel,
        grid_spec=pltpu.PrefetchScalarGridSpec(
            num_scalar_prefetch=3,
            grid=(n_rows // tm,),
            in_specs=[pl.BlockSpec((tm, XS_COLS), lambda i, lo, hi, na: (i, 0)),
                      w_in(0), w_in(0), w_dn(0), w_in(1), w_in(1), w_dn(1)],
            out_specs=pl.BlockSpec((tm, D_MODEL), lambda i, lo, hi, na: (i, 0)),
        ),
        out_shape=jax.ShapeDtypeStruct((n_rows, D_MODEL), F32),
        compiler_params=pltpu.CompilerParams(dimension_semantics=("arbitrary",), vmem_limit_bytes=VMEM_LIMIT_BYTES),
        name="moe_group",
    )(e_lo, e_hi, n_active, xs, wg, wu, wd, wg, wu, wd)


def _combine_kernel(dest_ref, x1_ref, y_ref, o_ref, buf, sem):
    tm = x1_ref.shape[0]

    def start(r, carry):
        _row_copy(y_ref, dest_ref[0, 0, r], buf, r, sem).start()
        return carry
    lax.fori_loop(0, tm, start, 0, unroll=8)

    pltpu.make_async_copy(y_ref.at[pl.ds(0, tm), :], buf, sem).wait()
    o_ref[...] = x1_ref[...] + buf[...]


def _combine(dest, x1, y):
    t = x1.shape[0]
    tm = TM_MOE
    return pl.pallas_call(
        _combine_kernel,
        grid=(t // tm,),
        in_specs=[
            pl.BlockSpec((1, 1, tm), lambda i: (i, 0, 0), memory_space=pltpu.SMEM),
            pl.BlockSpec((tm, D_MODEL), lambda i: (i, 0)),
            pl.BlockSpec(memory_space=pl.ANY),
        ],
        out_specs=pl.BlockSpec((tm, D_MODEL), lambda i: (i, 0)),
        out_shape=jax.ShapeDtypeStruct((t, D_MODEL), F32),
        scratch_shapes=[pltpu.VMEM((tm, D_MODEL), F32), pltpu.SemaphoreType.DMA],
        compiler_params=pltpu.CompilerParams(dimension_semantics=("arbitrary",)),
        name="moe_combine",
    )(dest.reshape(t // tm, 1, tm), x1, y)


def _rope_angles(pos, dim):
    inv = 1.0 / (ROPE_THETA ** (jnp.arange(0, dim, 2, dtype=F32) / dim))
    return pos.astype(F32)[:, None] * inv[None, :]


def _rope_tables():
    pos = jnp.arange(SEQ, dtype=jnp.int32)
    z = lambda w: jnp.zeros((SEQ, w), F32)
    ang = _rope_angles(pos, MLA_ROPE_DIM)
    c, s = jnp.cos(ang), jnp.sin(ang)
    m_c = jnp.concatenate([jnp.ones((SEQ, 64), F32), c, c, z(32)], axis=1)
    m_sn = jnp.concatenate([z(64), -s, z(48)], axis=1)
    m_sp = jnp.concatenate([z(80), s, z(32)], axis=1)
    row_pos = pos // GRID_W
    col_pos = pos - row_pos * GRID_W
    ra, ca = _rope_angles(row_pos, HEAD_DIM // 2), _rope_angles(col_pos, HEAD_DIM // 2)
    rc, rs, cc, cs = jnp.cos(ra), jnp.sin(ra), jnp.cos(ca), jnp.sin(ca)
    a_c = jnp.concatenate([rc, rc, cc, cc] * 2, axis=1)
    a_sn = jnp.concatenate([-rs, z(16), -cs, z(16)] * 2, axis=1)
    a_sp = jnp.concatenate([z(16), rs, z(16), cs] * 2, axis=1)
    return jnp.concatenate([m_c, m_sn, m_sp, a_c, a_sn, a_sp], axis=1)


def _rel_bucket(rel):
    half = NUM_BUCKETS // 2
    max_exact = half // 2
    n = jnp.abs(rel)
    nf = jnp.maximum(n, 1).astype(F32)
    log_ratio = jnp.log(nf / max_exact) / math.log(REL_MAX_DISTANCE / max_exact)
    large = jnp.minimum(max_exact + (log_ratio * (half - max_exact)).astype(jnp.int32), half - 1)
    return jnp.where(rel > 0, half, 0) + jnp.where(n < max_exact, n, large)


def _bias_line(table, rel):
    return table[_rel_bucket(rel)].T


def _toeplitz(line, rows, cols, first):
    n = line.shape[-1]
    padded = jnp.pad(line, [(0, 0)] * (line.ndim - 1) + [(0, 1)])
    flat = jnp.tile(padded, rows)[..., :rows * n]
    skew = flat.reshape(line.shape[:-1] + (rows, n))
    return skew[..., first:first + cols]


def _dil_bias(table):
    reach = DIL_QBLK + DIL_HALF - 1
    steps = jnp.arange(-reach, reach + 1, dtype=jnp.int32)
    band = jnp.abs(steps) <= DIL_HALF
    lines = jnp.stack([jnp.where(band[None], _bias_line(table, steps * d), NEG_INF) for _, d in DIL_PATTERNS])
    return _toeplitz(lines, DIL_QBLK, 2 * DIL_QBLK, DIL_QBLK - 1)


def _diff_bias_windows(table):
    rel = jnp.arange(2 * SEQ, dtype=jnp.int32) - SEQ
    line = _bias_line(table, rel)
    starts = [SEQ - (i + 1) * TQ for i in range(SEQ // TQ)]
    return jnp.stack([line[:, s0:s0 + SEQ + TQ] for s0 in starts])


def _block_ones(group):
    idx = np.arange(MXU_DIM) // group
    return jnp.asarray(idx[:, None] == idx[None, :], dtype=BF16)


def _pad_row(v):
    return jnp.pad(v.astype(F32), (0, D_MODEL - v.shape[0]))


def _layer_params(layer, norm1_g, w_in, mla_q_norm_g, mla_kv_norm_g, mla_w_uq, mla_w_ukv, mla_qk_g, dil_qk_g,
                  gqa_qk_g, diff_qk_g, diff_lambda, diff_subln_g, mix_beta, w_out, norm2_g, router_group_w,
                  router_group_b, router_expert_w, router_expert_b):
    w = w_in[layer]
    w_p = _bf(jnp.concatenate([w[:, :416], jnp.zeros((D_MODEL, 96), F32), w[:, 416:]], axis=1))
    uq = mla_w_uq[layer].reshape(MLA_Q_RANK, N_HEADS, MLA_QK_DIM)
    wuq = _bf(jnp.pad(uq, ((0, 0), (0, 0), (0, LANES - MLA_QK_DIM))).reshape(MLA_Q_RANK, 512))
    ukv = mla_w_ukv[layer].reshape(MLA_KV_RANK, N_HEADS, 2 * MLA_NOPE_DIM)
    wuk = _bf(jnp.pad(ukv[:, :, :MLA_NOPE_DIM], ((0, 0), (0, 0), (0, LANES - MLA_NOPE_DIM))).reshape(MLA_KV_RANK, 512))
    wuv = _bf(ukv[:, :, MLA_NOPE_DIM:].reshape(MLA_KV_RANK, 256))

    pad96 = lambda g: jnp.tile(jnp.pad(g, (0, LANES - MLA_QK_DIM)), N_HEADS)
    m0 = jnp.tile(jnp.concatenate([jnp.ones(32, F32), jnp.zeros(32, F32)]), N_HEADS)
    dq = jnp.tile(diff_qk_g[layer, 0], 2 * N_HEADS) * DIFF_QK_DIM ** -0.5
    rows = [
        norm1_g[layer], mla_q_norm_g[layer], mla_kv_norm_g[layer],
        pad96(mla_qk_g[layer, 0]) * MLA_QK_DIM ** -0.5, pad96(mla_qk_g[layer, 1]),
        jnp.tile(dil_qk_g[layer, 0], N_HEADS) * HEAD_DIM ** -0.5, jnp.tile(dil_qk_g[layer, 1], N_HEADS),
        jnp.tile(gqa_qk_g[layer, 0], N_HEADS) * HEAD_DIM ** -0.5, jnp.tile(gqa_qk_g[layer, 1], 2),
        dq * m0, dq * (1.0 - m0), jnp.tile(diff_qk_g[layer, 1], 2 * N_HEADS),
    ]
    gv = jnp.stack([_pad_row(r) for r in rows] + [jnp.zeros(D_MODEL, F32)] * (GAIN_ROWS - len(rows)))

    lambda_init = 0.8 - 0.6 * math.exp(-0.3 * layer)
    lv = diff_lambda[layer].astype(F32)
    lam = (jnp.exp(jnp.sum(lv[0] * lv[1])) - jnp.exp(jnp.sum(lv[2] * lv[3])) + lambda_init).reshape(1)
    sub_gain = (jnp.tile(diff_subln_g[layer], N_HEADS) * (1.0 - lambda_init)).reshape(1, 256)

    cv = jnp.stack([mix_beta[layer], norm2_g[layer]] + [jnp.zeros(D_MODEL, F32)] * 6)
    wr = jnp.concatenate([router_group_w[layer], router_expert_w[layer],
                          jnp.zeros((D_MODEL, LANES - 4 - N_EXPERTS), F32)], axis=1)
    wr_hi = _bf(wr)
    wr_lo = _bf(wr - wr_hi.astype(F32))
    rb = jnp.concatenate([router_group_b[layer], router_expert_b[layer],
                          jnp.zeros(LANES - 4 - N_EXPERTS, F32)]).reshape(1, LANES)
    return dict(w_in=w_p, wuq=wuq, wuk=wuk, wuv=wuv, gv=gv, lam=lam, sub_gain=sub_gain, cv=cv,
                w_out=_bf(w_out[layer]), wr_hi=wr_hi, wr_lo=wr_lo, rb=rb)


def kernel(x, rel_bias, norm1_g, w_in, mla_q_norm_g, mla_kv_norm_g, mla_w_uq, mla_w_ukv, mla_qk_g, dil_qk_g, gqa_qk_g, diff_qk_g, diff_lambda, diff_subln_g, mix_beta, w_out, norm2_g, router_group_w, router_group_b, router_expert_w, router_expert_b, expert_w_gate, expert_w_up, expert_w_down):
    batch, seq, d_model = x.shape
    assert seq == SEQ and d_model == D_MODEL
    depth = w_in.shape[0]
    t = batch * seq

    rope = _rope_tables()
    g32, g64, g128 = _block_ones(32), _block_ones(64), _block_ones(128)
    tri = jnp.asarray(np.arange(TM_OUT)[:, None] < np.arange(TM_OUT)[None, :], dtype=BF16)
    dil_bias = _dil_bias(rel_bias[:, :N_HEADS])
    diff_bias = _diff_bias_windows(rel_bias[:, N_HEADS:])

    x2d = x.reshape(t, D_MODEL)
    for layer in range(depth):
        p = _layer_params(layer, norm1_g, w_in, mla_q_norm_g, mla_kv_norm_g, mla_w_uq, mla_w_ukv, mla_qk_g,
                          dil_qk_g, gqa_qk_g, diff_qk_g, diff_lambda, diff_subln_g, mix_beta, w_out, norm2_g,
                          router_group_w, router_group_b, router_expert_w, router_expert_b)
        act, act4, act16 = _inproj(x2d, p["gv"], p["w_in"], p["wuq"], p["wuk"], p["wuv"], rope, g32, g64, g128)
        ya = _attn_full(act, batch, _attn_mla_kernel, A_QA, 512, A_KA, 512, A_VA, 256, "attn_mla")
        yb = _attn_dilated(act, act4, act16, batch, dil_bias)
        yc = _attn_full(act, batch, _attn_gqa_kernel, A_QC, 256, A_KC, 128, A_VC, 128, "attn_gqa")
        yd = _attn_diff(act, batch, p["lam"], diff_bias, p["sub_gain"])
        x1, hp, ri, counts = _outproj(ya, yb, yc, yd, x2d, p["cv"], p["w_out"], p["wr_hi"], p["wr_lo"], p["rb"], tri)
        n_tiles = t // TM_MOE + N_CLASSES
        dest, e_lo, e_hi, n_active = _moe_plan(ri, counts, n_tiles)
        xs = _dispatch(dest, hp, n_tiles * TM_MOE)
        y = _moe_group(e_lo, e_hi, n_active, xs, _bf(expert_w_gate[layer]), _bf(expert_w_up[layer]),
                       _bf(expert_w_down[layer]))
        x2d = _combine(dest, x1, y)
    return x2d.reshape(batch, seq, D_MODEL)
```

```python
import functools
import math

import jax
import jax.numpy as jnp
import numpy as np
from jax import lax
from jax.experimental import pallas as pl
from jax.experimental.pallas import tpu as pltpu

F32 = jnp.float32
BF16 = jnp.bfloat16

D_MODEL = 1024
SEQ = 2048
HEAD_DIM = 64
GRID_W = 64
ROPE_THETA = 10000.0
LOG2E = 1.0 / math.log(2.0)
NORM_EPS = 1e-6
NEG_INF = -1e30
NUM_BUCKETS = 32
REL_MAX_DISTANCE = 1024

N_HEADS = 4
MLA_NOPE_DIM = 64
MLA_ROPE_DIM = 32
MLA_QK_DIM = MLA_NOPE_DIM + MLA_ROPE_DIM
MLA_Q_RANK = 256
MLA_KV_RANK = 128
DIL_PATTERNS = ((128, 1), (512, 4), (2048, 16))
DIL_HALF = 64
DIFF_QK_DIM = 32
N_EXPERTS = 16
EXPERTS_PER_GROUP = 4
D_FF = 512

LANES = 128
MXU_DIM = 256
VMEM_LIMIT_BYTES = 56 * 1024 * 1024

P_CQ, P_CKV, P_KR = 0, 256, 384
P_BQ, P_BK, P_BV = 512, 768, 1024
P_CQ2, P_CK2, P_CV2 = 1280, 1536, 1664
P_DQ, P_DK, P_DV = 1792, 2048, 2304
PROJ_COLS = 2560

A_QB, A_KB, A_VB = 0, 256, 512
A_VA, A_QA, A_KA = 768, 1024, 1536
A_QC, A_KC, A_VC = 2048, 2304, 2432
A_QD0, A_QD1, A_KD, A_VD = 2560, 2816, 3072, 3328
ACT_COLS = 3584
DIL_COLS = 768

(G_NORM1, G_MLA_QN, G_MLA_KVN, G_MLA_Q, G_MLA_K, G_DIL_Q, G_DIL_K, G_GQA_Q, G_GQA_K,
 G_DIFF_Q0, G_DIFF_Q1, G_DIFF_K) = range(12)
GAIN_ROWS = 16

TM_PROJ = 512
TQ = 256
TQ_FULL = 512
TM_OUT = 512
TM_MOE = 256
DIL_QBLK = 128

PAIRS_PER_GROUP = 6
N_CLASSES = 4 * PAIRS_PER_GROUP
CLS_ROWS = 32
XS_COLS = D_MODEL + LANES


def _bf(x):
    return x.astype(BF16)


def _dot(a, b):
    return jnp.dot(a, b, preferred_element_type=F32)


def _dot_nt(a, b):
    return lax.dot_general(a, b, (((1,), (1,)), ((), ())), preferred_element_type=F32)


def _rms(x, width):
    return x * lax.rsqrt(jnp.sum(x * x, axis=-1, keepdims=True) * (1.0 / width) + NORM_EPS)


def _group_sumsq(x, g):
    x2 = x * x
    hi = _bf(x2)
    lo = _bf(x2 - hi.astype(F32))
    w = g.shape[0]
    outs = []
    for c in range(x.shape[1] // w):
        sl = slice(w * c, w * (c + 1))
        outs.append(_dot(hi[:, sl], g) + _dot(lo[:, sl], g))
    return outs[0] if len(outs) == 1 else jnp.concatenate(outs, axis=1)


def _group_rms(x, g, group):
    return x * lax.rsqrt(_group_sumsq(x, g) * (1.0 / group) + NORM_EPS)


def _rope(x, c, s_next, s_prev, half):
    n = x.shape[1]
    return x * c + pltpu.roll(x, n - half, axis=1) * s_next + pltpu.roll(x, half, axis=1) * s_prev


def _inproj_kernel(x_ref, gv_ref, w_ref, wuq_ref, wuk_ref, wuv_ref, rope_ref, g32_ref, g64_ref, g128_ref,
                   o_ref, o4_ref, o16_ref, stage):
    def gain(row, width):
        return gv_ref[row:row + 1, 0:width]

    x = x_ref[...]
    h = _rms(x, D_MODEL) * gain(G_NORM1, D_MODEL)
    proj = _dot(_bf(h), w_ref[...])

    g32 = g32_ref[...]
    g64 = g64_ref[...]
    g128 = g128_ref[...]
    m_c = rope_ref[:, 0:128]
    m_sn = rope_ref[:, 128:256]
    m_sp = rope_ref[:, 256:384]
    a_c = rope_ref[:, 384:512]
    a_sn = rope_ref[:, 512:640]
    a_sp = rope_ref[:, 640:768]

    cq = _rms(proj[:, P_CQ:P_CQ + MLA_Q_RANK], MLA_Q_RANK) * gain(G_MLA_QN, MLA_Q_RANK)
    ckv = _rms(proj[:, P_CKV:P_CKV + MLA_KV_RANK], MLA_KV_RANK) * gain(G_MLA_KVN, MLA_KV_RANK)
    q = _dot(_bf(cq), wuq_ref[...])
    k_nope = _dot(_bf(ckv), wuk_ref[...])
    v = _dot(_bf(ckv), wuv_ref[...])
    k_rope = pltpu.roll(proj[:, P_KR:P_KR + LANES], MLA_NOPE_DIM, axis=1)
    k = k_nope + jnp.concatenate([k_rope] * N_HEADS, axis=1)
    qn = _group_rms(q, g128, MLA_QK_DIM) * gain(G_MLA_Q, 512)
    kn = _group_rms(k, g128, MLA_QK_DIM) * gain(G_MLA_K, 512)
    for g in range(N_HEADS):
        sl = slice(LANES * g, LANES * (g + 1))
        o_ref[:, A_QA + LANES * g:A_QA + LANES * (g + 1)] = _bf(_rope(qn[:, sl], m_c, m_sn, m_sp, MLA_ROPE_DIM // 2))
        o_ref[:, A_KA + LANES * g:A_KA + LANES * (g + 1)] = _bf(_rope(kn[:, sl], m_c, m_sn, m_sp, MLA_ROPE_DIM // 2))
    o_ref[:, A_VA:A_VA + 256] = _bf(v)

    qb = _group_rms(proj[:, P_BQ:P_BQ + 256], g64, HEAD_DIM) * gain(G_DIL_Q, 256)
    kb = _group_rms(proj[:, P_BK:P_BK + 256], g64, HEAD_DIM) * gain(G_DIL_K, 256)
    vb = proj[:, P_BV:P_BV + 256]
    o_ref[:, A_QB:A_QB + 256] = _bf(qb)
    o_ref[:, A_KB:A_KB + 256] = _bf(kb)
    o_ref[:, A_VB:A_VB + 256] = _bf(vb)
    for c, val in enumerate((qb, kb, vb)):
        stage[2 * c] = val[:, 0:LANES]
        stage[2 * c + 1] = val[:, LANES:2 * LANES]
    n_chunks = DIL_COLS // LANES
    for d, ref in ((4, o4_ref), (16, o16_ref)):
        n = x.shape[0] // d
        for r in range(d):
            ref[0, r] = _bf(jnp.concatenate(
                [stage[c, pl.ds(r, n, stride=d), :] for c in range(n_chunks)], axis=1))

    qc = _group_rms(proj[:, P_CQ2:P_CQ2 + 256], g64, HEAD_DIM) * gain(G_GQA_Q, 256)
    for g in range(2):
        sl = slice(LANES * g, LANES * (g + 1))
        o_ref[:, A_QC + LANES * g:A_QC + LANES * (g + 1)] = _bf(_rope(qc[:, sl], a_c, a_sn, a_sp, HEAD_DIM // 4))
    kc = _group_rms(proj[:, P_CK2:P_CK2 + 128], g64[0:128, 0:128], HEAD_DIM) * gain(G_GQA_K, 128)
    o_ref[:, A_KC:A_KC + 128] = _bf(_rope(kc, a_c, a_sn, a_sp, HEAD_DIM // 4))
    o_ref[:, A_VC:A_VC + 128] = _bf(proj[:, P_CV2:P_CV2 + 128])

    dqn = _group_rms(proj[:, P_DQ:P_DQ + 256], g32, DIFF_QK_DIM)
    o_ref[:, A_QD0:A_QD0 + 256] = _bf(dqn * gain(G_DIFF_Q0, 256))
    o_ref[:, A_QD1:A_QD1 + 256] = _bf(dqn * gain(G_DIFF_Q1, 256))
    o_ref[:, A_KD:A_KD + 256] = _bf(_group_rms(proj[:, P_DK:P_DK + 256], g32, DIFF_QK_DIM) * gain(G_DIFF_K, 256))
    o_ref[:, A_VD:A_VD + 256] = _bf(proj[:, P_DV:P_DV + 256])


def _inproj(x2d, gv, w_in, wuq, wuk, wuv, rope, g32, g64, g128):
    t = x2d.shape[0]
    tm = TM_PROJ
    n_pos = SEQ // tm
    const = lambda i: (0, 0)
    return pl.pallas_call(
        _inproj_kernel,
        grid=(t // tm,),
        in_specs=[
            pl.BlockSpec((tm, D_MODEL), lambda i: (i, 0)),
            pl.BlockSpec((GAIN_ROWS, D_MODEL), const),
            pl.BlockSpec((D_MODEL, PROJ_COLS), const),
            pl.BlockSpec((MLA_Q_RANK, 512), const),
            pl.BlockSpec((MLA_KV_RANK, 512), const),
            pl.BlockSpec((MLA_KV_RANK, 256), const),
            pl.BlockSpec((tm, 768), lambda i: (i % n_pos, 0)),
            pl.BlockSpec((MXU_DIM, MXU_DIM), const),
            pl.BlockSpec((MXU_DIM, MXU_DIM), const),
            pl.BlockSpec((MXU_DIM, MXU_DIM), const),
        ],
        out_specs=[
            pl.BlockSpec((tm, ACT_COLS), lambda i: (i, 0)),
            pl.BlockSpec((1, 4, tm // 4, DIL_COLS), lambda i: (i // n_pos, 0, i % n_pos, 0)),
            pl.BlockSpec((1, 16, tm // 16, DIL_COLS), lambda i: (i // n_pos, 0, i % n_pos, 0)),
        ],
        out_shape=[
            jax.ShapeDtypeStruct((t, ACT_COLS), BF16),
            jax.ShapeDtypeStruct((t // SEQ, 4, SEQ // 4, DIL_COLS), BF16),
            jax.ShapeDtypeStruct((t // SEQ, 16, SEQ // 16, DIL_COLS), BF16),
        ],
        scratch_shapes=[pltpu.VMEM((DIL_COLS // LANES, tm, LANES), F32)],
        compiler_params=pltpu.CompilerParams(dimension_semantics=("parallel",), vmem_limit_bytes=VMEM_LIMIT_BYTES),
        name="inproj_prep",
    )(x2d, gv, w_in, wuq, wuk, wuv, rope, g32, g64, g128)


def _softmax_pv(s, v):
    m = jnp.max(s, axis=-1, keepdims=True)
    e = jnp.exp2(s - m)
    l = jnp.sum(e, axis=-1, keepdims=True)
    return _dot(_bf(e), v) / l


def _attn_mla_kernel(q_ref, k_ref, v_ref, o_ref):
    for h in range(N_HEADS):
        sl = slice(LANES * h, LANES * (h + 1))
        s = _dot_nt(q_ref[:, sl], k_ref[:, sl])
        o_ref[:, HEAD_DIM * h:HEAD_DIM * (h + 1)] = _softmax_pv(s, v_ref[:, HEAD_DIM * h:HEAD_DIM * (h + 1)])


def _attn_gqa_kernel(q_ref, k_ref, v_ref, o_ref):
    for h in range(N_HEADS):
        g = h // 2
        hs = slice(HEAD_DIM * h, HEAD_DIM * (h + 1))
        gs = slice(HEAD_DIM * g, HEAD_DIM * (g + 1))
        s = _dot_nt(q_ref[:, hs], k_ref[:, gs])
        o_ref[:, hs] = _softmax_pv(s, v_ref[:, gs])


def _attn_full(act, batch, kernel, q_col, q_w, k_col, k_w, v_col, v_w, name):
    t = act.shape[0]
    tq = TQ_FULL
    nq = SEQ // tq
    return pl.pallas_call(
        kernel,
        grid=(batch, nq),
        in_specs=[
            pl.BlockSpec((tq, q_w), lambda b, i: (b * nq + i, q_col // q_w)),
            pl.BlockSpec((SEQ, k_w), lambda b, i: (b, k_col // k_w)),
            pl.BlockSpec((SEQ, v_w), lambda b, i: (b, v_col // v_w)),
        ],
        out_specs=pl.BlockSpec((tq, 256), lambda b, i: (b * nq + i, 0)),
        out_shape=jax.ShapeDtypeStruct((t, 256), F32),
        compiler_params=pltpu.CompilerParams(
            dimension_semantics=("parallel", "parallel"), vmem_limit_bytes=VMEM_LIMIT_BYTES),
        name=name,
    )(act, act, act)


def _attn_diff_kernel(lam_ref, q0_ref, q1_ref, k_ref, v_ref, win_ref, sg_ref, o_ref, bias_ref):
    @pl.when(pl.program_id(1) == 0)
    def _():
        for h in range(N_HEADS):
            w = jnp.broadcast_to(win_ref[0, h:h + 1, :], (TQ, SEQ + TQ))
            bias_ref[h] = pltpu.roll(w, 0, axis=1, stride=1, stride_axis=0)[:, TQ:TQ + SEQ]

    lam = lam_ref[0]
    for h in range(N_HEADS):
        hs = slice(HEAD_DIM * h, HEAD_DIM * (h + 1))
        k = k_ref[:, hs]
        bias = bias_ref[h]
        s0 = _dot_nt(q0_ref[:, hs], k) + bias
        s1 = _dot_nt(q1_ref[:, hs], k) + bias
        e0 = jnp.exp2(s0 - jnp.max(s0, axis=-1, keepdims=True))
        e1 = jnp.exp2(s1 - jnp.max(s1, axis=-1, keepdims=True))
        r0 = 1.0 / jnp.sum(e0, axis=-1, keepdims=True)
        r1 = lam / jnp.sum(e1, axis=-1, keepdims=True)
        v = v_ref[:, hs]
        o = r0 * _dot(_bf(e0), v) - r1 * _dot(_bf(e1), v)
        o_ref[:, hs] = _rms(o, HEAD_DIM) * sg_ref[:, hs]


def _attn_diff(act, batch, lam, bias_win, sub_gain):
    t = act.shape[0]
    nq = SEQ // TQ
    return pl.pallas_call(
        _attn_diff_kernel,
        grid=(nq, batch),
        in_specs=[
            pl.BlockSpec(memory_space=pltpu.SMEM),
            pl.BlockSpec((TQ, 256), lambda i, b: (b * nq + i, A_QD0 // 256)),
            pl.BlockSpec((TQ, 256), lambda i, b: (b * nq + i, A_QD1 // 256)),
            pl.BlockSpec((SEQ, 256), lambda i, b: (b, A_KD // 256)),
            pl.BlockSpec((SEQ, 256), lambda i, b: (b, A_VD // 256)),
            pl.BlockSpec((1, N_HEADS, SEQ + TQ), lambda i, b: (i, 0, 0)),
            pl.BlockSpec((1, 256), lambda i, b: (0, 0)),
        ],
        out_specs=pl.BlockSpec((TQ, 256), lambda i, b: (b * nq + i, 0)),
        out_shape=jax.ShapeDtypeStruct((t, 256), F32),
        scratch_shapes=[pltpu.VMEM((N_HEADS, TQ, SEQ), F32)],
        compiler_params=pltpu.CompilerParams(
            dimension_semantics=("arbitrary", "arbitrary"), vmem_limit_bytes=VMEM_LIMIT_BYTES),
        name="attn_diff",
    )(lam, act, act, act, act, bias_win, sub_gain)


def _dil_branch(q_of, k_of, v_of, bias_of, n_seq):
    outs, lses = [], []
    for h in range(N_HEADS):
        hs = slice(HEAD_DIM * h, HEAD_DIM * (h + 1))
        q, k, v = q_of(hs), k_of(hs), v_of(hs)
        n = q.shape[1]
        if n == DIL_QBLK:
            kw, vw = k, v
            bias = bias_of(h)[:, DIL_HALF:DIL_HALF + DIL_QBLK]
            edge = None
        else:
            n_blk = n // DIL_QBLK

            def windows(x):
                zeros = jnp.zeros((n_seq, DIL_HALF, HEAD_DIM), BF16)
                xp = jnp.concatenate([zeros, x, zeros], axis=1)
                first = xp[:, 0:n].reshape(n_seq * n_blk, DIL_QBLK, HEAD_DIM)
                second = xp[:, DIL_QBLK:DIL_QBLK + n].reshape(n_seq * n_blk, DIL_QBLK, HEAD_DIM)
                return jnp.concatenate([first, second], axis=1)
            kw, vw = windows(k), windows(v)
            q = q.reshape(n_seq * n_blk, DIL_QBLK, HEAD_DIM)
            bias = bias_of(h)
            blk = lax.broadcasted_iota(jnp.int32, (n_seq * n_blk, 1, 2 * DIL_QBLK), 0) % n_blk
            col = lax.broadcasted_iota(jnp.int32, (n_seq * n_blk, 1, 2 * DIL_QBLK), 2)
            key_pos = col - DIL_HALF + blk * DIL_QBLK
            edge = jnp.where(jnp.logical_and(key_pos >= 0, key_pos < n), 0.0, NEG_INF)
        s = jnp.einsum("bqe,bke->bqk", q, kw, preferred_element_type=F32) + bias[None]
        if edge is not None:
            s = s + edge
        m = jnp.max(s, axis=-1, keepdims=True)
        e = jnp.exp2(s - m)
        l = jnp.sum(e, axis=-1, keepdims=True)
        o = jnp.einsum("bqk,bke->bqe", _bf(e), vw, preferred_element_type=F32) / l
        lse = jnp.broadcast_to(m + jnp.log2(l), o.shape)
        outs.append(o.reshape(n_seq, n, HEAD_DIM))
        lses.append(lse.reshape(n_seq, n, HEAD_DIM))
    cat = lambda xs: [jnp.concatenate(xs[2 * c:2 * c + 2], axis=2) for c in range(2)]
    return cat(outs), cat(lses)


def _dil_kernel(n1_ref, r4_ref, r16_ref, bias_ref, o_ref, o1, l1, o4, l4, out):
    col_of = lambda base: (lambda hs: slice(base + hs.start, base + hs.stop))
    q_cols, k_cols, v_cols = col_of(0), col_of(256), col_of(512)

    oc, lc = _dil_branch(lambda hs: n1_ref[:, q_cols(hs)][None], lambda hs: n1_ref[:, k_cols(hs)][None],
                         lambda hs: n1_ref[:, v_cols(hs)][None], lambda h: bias_ref[0, h], 1)
    for c in range(2):
        o1[c] = oc[c][0]
        l1[c] = lc[c][0]

    oc, lc = _dil_branch(lambda hs: r4_ref[0, :, :, q_cols(hs)], lambda hs: r4_ref[0, :, :, k_cols(hs)],
                         lambda hs: r4_ref[0, :, :, v_cols(hs)], lambda h: bias_ref[1, h], 4)
    for c in range(2):
        for r in range(4):
            tok = pl.ds(r, SEQ // 4, stride=4)
            o4[c, tok, :] = oc[c][r]
            l4[c, tok, :] = lc[c][r]

    oc, lc = _dil_branch(lambda hs: r16_ref[0, :, :, q_cols(hs)], lambda hs: r16_ref[0, :, :, k_cols(hs)],
                         lambda hs: r16_ref[0, :, :, v_cols(hs)], lambda h: bias_ref[2, h], 16)
    toks = [pl.ds(r, SEQ // 16, stride=16) for r in range(16)]
    for c in range(2):
        gather = lambda ref: jnp.stack([ref[c, tok, :] for tok in toks])
        la, lb = gather(l1), gather(l4)
        mx = jnp.maximum(jnp.maximum(la, lb), lc[c])
        wa, wb, wc = jnp.exp2(la - mx), jnp.exp2(lb - mx), jnp.exp2(lc[c] - mx)
        den = wa + wb + wc
        res = (wa / den) * gather(o1) + (wb / den) * gather(o4) + (wc / den) * oc[c]
        for r, tok in enumerate(toks):
            out[c, tok, :] = res[r]
    o_ref[:, 0:LANES] = out[0]
    o_ref[:, LANES:2 * LANES] = out[1]


def _attn_dilated(act, act4, act16, batch, bias_tabs):
    t = act.shape[0]
    nat = pltpu.VMEM((2, SEQ, LANES), F32)
    return pl.pallas_call(
        _dil_kernel,
        grid=(batch,),
        in_specs=[
            pl.BlockSpec((SEQ, DIL_COLS), lambda b: (b, 0)),
            pl.BlockSpec((1, 4, SEQ // 4, DIL_COLS), lambda b: (b, 0, 0, 0)),
            pl.BlockSpec((1, 16, SEQ // 16, DIL_COLS), lambda b: (b, 0, 0, 0)),
            pl.BlockSpec((3, N_HEADS, DIL_QBLK, 2 * DIL_QBLK), lambda b: (0, 0, 0, 0)),
        ],
        out_specs=pl.BlockSpec((SEQ, 256), lambda b: (b, 0)),
        out_shape=jax.ShapeDtypeStruct((t, 256), F32),
        scratch_shapes=[nat, nat, nat, nat, nat],
        compiler_params=pltpu.CompilerParams(dimension_semantics=("parallel",), vmem_limit_bytes=VMEM_LIMIT_BYTES),
        name="attn_dilated",
    )(act, act4, act16, bias_tabs)


def _route(lt):
    g = [lt[i:i + 1, :] for i in range(4)]
    gmax = jnp.maximum(jnp.maximum(g[0], g[1]), jnp.maximum(g[2], g[3]))
    gsum = sum(jnp.exp(gi - gmax) for gi in g)
    g_w = 1.0 / gsum
    gidx = jnp.where(g[0] == gmax, 0, jnp.where(g[1] == gmax, 1, jnp.where(g[2] == gmax, 2, 3)))
    el = []
    for j in range(EXPERTS_PER_GROUP):
        acc = jnp.zeros_like(g[0])
        for i in range(4):
            r = 4 + EXPERTS_PER_GROUP * i + j
            acc = jnp.where(gidx == i, lt[r:r + 1, :], acc)
        el.append(acc)
    emax = jnp.maximum(jnp.maximum(el[0], el[1]), jnp.maximum(el[2], el[3]))
    ee = [jnp.exp(e - emax) for e in el]
    esum = ee[0] + ee[1] + ee[2] + ee[3]
    p = [e / esum for e in ee]
    p1 = jnp.maximum(jnp.maximum(p[0], p[1]), jnp.maximum(p[2], p[3]))
    i1 = jnp.where(p[0] == p1, 0, jnp.where(p[1] == p1, 1, jnp.where(p[2] == p1, 2, 3)))
    pm = [jnp.where(i1 == j, -1.0, p[j]) for j in range(4)]
    p2 = jnp.maximum(jnp.maximum(pm[0], pm[1]), jnp.maximum(pm[2], pm[3]))
    i2 = jnp.where(pm[0] == p2, 0, jnp.where(pm[1] == p2, 1, jnp.where(pm[2] == p2, 2, 3)))
    den = p1 + p2
    return gidx, i1, i2, g_w * (p1 / den), g_w * (p2 / den)


def _pair_offset(lo):
    return jnp.where(lo == 0, 0, jnp.where(lo == 1, 3, 5))


def _outproj_kernel(ya_ref, yb_ref, yc_ref, yd_ref, x_ref, cv_ref, w_ref, wrh_ref, wrl_ref, rb_ref, tri_ref,
                    x1_ref, hp_ref, ri_ref, cnt_out_ref, cnt_ref):
    tm = x_ref.shape[0]

    @pl.when(pl.program_id(0) == 0)
    def _():
        cnt_ref[...] = jnp.zeros_like(cnt_ref)

    mixed = jnp.concatenate(
        [_rms(ya_ref[...], 256), _rms(yb_ref[...], 256), _rms(yc_ref[...], 256), yd_ref[...]], axis=1)
    mixed = mixed * cv_ref[0:1, :]
    x1 = x_ref[...] + _dot(_bf(mixed), w_ref[...])
    x1_ref[...] = x1
    h2 = _rms(x1, D_MODEL) * cv_ref[1:2, :]
    h_hi = _bf(h2)
    h_rt = h_hi.astype(F32)
    h_lo = _bf(h2 - h_rt)
    logits = _dot(h_hi, wrh_ref[...]) + _dot(h_hi, wrl_ref[...]) + _dot(h_lo, wrh_ref[...]) + rb_ref[...]
    gidx, i1, i2, wa, wb = _route(logits.T)

    lo, hi = jnp.minimum(i1, i2), jnp.maximum(i1, i2)
    first_is_lo = i1 < i2
    w_lo, w_hi = jnp.where(first_is_lo, wa, wb), jnp.where(first_is_lo, wb, wa)
    cls = gidx * PAIRS_PER_GROUP + _pair_offset(lo) + (hi - lo - 1)

    rows = lax.broadcasted_iota(jnp.int32, (CLS_ROWS, tm), 0)
    onehot = rows == cls
    oh = jnp.where(onehot, 1.0, 0.0)
    prefix = _dot(_bf(oh), tri_ref[...])
    before = cnt_ref[:, 0:1] + prefix
    rank = jnp.sum(jnp.where(onehot, before, 0.0), axis=0, keepdims=True)
    cnt_ref[...] = cnt_ref[...] + jnp.sum(oh, axis=1, keepdims=True)
    cnt_out_ref[...] = cnt_ref[...]
    ri_ref[...] = jnp.concatenate([cls, rank.astype(jnp.int32), jnp.zeros((6, tm), jnp.int32)], axis=0)

    hp_ref[:, 0:D_MODEL] = h_rt
    rows_w = lax.broadcasted_iota(jnp.int32, (LANES, tm), 0)
    w_t = jnp.where(rows_w == 0, w_lo, jnp.where(rows_w == 1, w_hi, 0.0))
    hp_ref[:, D_MODEL:XS_COLS] = w_t.T


def _outproj(ya, yb, yc, yd, x2d, cv, w_out, wr_hi, wr_lo, rb, tri):
    t = x2d.shape[0]
    tm = TM_OUT
    row = lambda w: pl.BlockSpec((tm, w), lambda i: (i, 0))
    const = lambda shape: pl.BlockSpec(shape, lambda i: (0, 0))
    return pl.pallas_call(
        _outproj_kernel,
        grid=(t // tm,),
        in_specs=[row(256), row(256), row(256), row(256), row(D_MODEL), const((8, D_MODEL)),
                  const((D_MODEL, D_MODEL)), const((D_MODEL, LANES)), const((D_MODEL, LANES)), const((1, LANES)),
                  const((tm, tm))],
        out_specs=[row(D_MODEL), row(XS_COLS), pl.BlockSpec((8, tm), lambda i: (0, i)), const((CLS_ROWS, LANES))],
        out_shape=[jax.ShapeDtypeStruct((t, D_MODEL), F32), jax.ShapeDtypeStruct((t, XS_COLS), F32),
                   jax.ShapeDtypeStruct((8, t), jnp.int32), jax.ShapeDtypeStruct((CLS_ROWS, LANES), F32)],
        scratch_shapes=[pltpu.VMEM((CLS_ROWS, LANES), F32)],
        compiler_params=pltpu.CompilerParams(dimension_semantics=("arbitrary",), vmem_limit_bytes=VMEM_LIMIT_BYTES),
        name="outproj_router",
    )(ya, yb, yc, yd, x2d, cv, w_out, wr_hi, wr_lo, rb, tri)


def _moe_plan(ri, counts, n_tiles):
    cls, rank = ri[0], ri[1]
    cnt = counts[:N_CLASSES, 0].astype(jnp.int32)
    padded = ((cnt + TM_MOE - 1) // TM_MOE) * TM_MOE
    ends = jnp.cumsum(padded)
    offs = ends - padded
    classes = jnp.arange(N_CLASSES, dtype=jnp.int32)
    dest = jnp.sum(jnp.where(cls[:, None] == classes[None, :], offs[None, :], 0), axis=1) + rank
    tile_start = jnp.arange(n_tiles, dtype=jnp.int32) * TM_MOE
    tile_cls = jnp.sum((tile_start[:, None] >= ends[None, :]).astype(jnp.int32), axis=1)
    tile_cls = jnp.minimum(tile_cls, N_CLASSES - 1)
    group, pair = tile_cls // PAIRS_PER_GROUP, tile_cls % PAIRS_PER_GROUP
    lo = (pair >= 3).astype(jnp.int32) + (pair >= 5).astype(jnp.int32)
    hi = pair - _pair_offset(lo) + lo + 1
    n_active = (ends[-1] // TM_MOE).reshape(1)
    return dest, group * EXPERTS_PER_GROUP + lo, group * EXPERTS_PER_GROUP + hi, n_active


def _row_copy(src_ref, src_row, dst_ref, dst_row, sem):
    return pltpu.make_async_copy(src_ref.at[pl.ds(src_row, 1), :], dst_ref.at[pl.ds(dst_row, 1), :], sem)


def _dispatch_kernel(dest_ref, h_ref, xs_in_ref, xs_ref, sem):
    del xs_in_ref
    tm = h_ref.shape[0]

    for r in range(tm):
        _row_copy(h_ref, r, xs_ref, dest_ref[0, 0, r], sem).start()

    pltpu.make_async_copy(h_ref, xs_ref.at[pl.ds(0, tm), :], sem).wait()


def _dispatch(dest, hp, n_rows):
    t = hp.shape[0]
    tm = TM_MOE
    return pl.pallas_call(
        _dispatch_kernel,
        grid=(t // tm,),
        in_specs=[
            pl.BlockSpec((1, 1, tm), lambda i: (i, 0, 0), memory_space=pltpu.SMEM),
            pl.BlockSpec((tm, XS_COLS), lambda i: (i, 0)),
            pl.BlockSpec(memory_space=pl.ANY),
        ],
        out_specs=pl.BlockSpec(memory_space=pl.ANY),
        out_shape=jax.ShapeDtypeStruct((n_rows, XS_COLS), F32),
        scratch_shapes=[pltpu.SemaphoreType.DMA],
        input_output_aliases={2: 0},
        compiler_params=pltpu.CompilerParams(dimension_semantics=("arbitrary",)),
        name="moe_dispatch",
    )(dest.reshape(t // tm, 1, tm), hp, jnp.zeros((n_rows, XS_COLS), F32))


def _moe_group_kernel(elo_ref, ehi_ref, nact_ref, xs_ref, wg0, wu0, wd0, wg1, wu1, wd1, o_ref):
    del elo_ref, ehi_ref
    active = pl.program_id(0) < nact_ref[0]

    @pl.when(active)
    def _():
        x = _bf(xs_ref[:, 0:D_MODEL])
        gate = xs_ref[:, D_MODEL:XS_COLS]

        def expert(wg, wu, wd):
            a = _dot(x, wg[0])
            hid = a * (1.0 / (1.0 + jnp.exp(-a))) * _dot(x, wu[0])
            return _dot(_bf(hid), wd[0])
        o_ref[...] = gate[:, 0:1] * expert(wg0, wu0, wd0) + gate[:, 1:2] * expert(wg1, wu1, wd1)

    @pl.when(jnp.logical_not(active))
    def _():
        o_ref[...] = jnp.zeros_like(o_ref)


def _moe_group(e_lo, e_hi, n_active, xs, wg, wu, wd):
    n_rows = xs.shape[0]
    tm = TM_MOE
    w_in = lambda which: pl.BlockSpec((1, D_MODEL, D_FF), lambda i, lo, hi, na: ((lo, hi)[which][i], 0, 0))
    w_dn = lambda which: pl.BlockSpec((1, D_FF, D_MODEL), lambda i, lo, hi, na: ((lo, hi)[which][i], 0, 0))
    return pl.pallas_call(
        _moe_group_kernel,
        grid_spec=pltpu.PrefetchScalarGridSpec(
            num_scalar_prefetch=3,
            grid=(n_rows // tm,),
            in_specs=[pl.BlockSpec((tm, XS_COLS), lambda i, lo, hi, na: (i, 0)),
                      w_in(0), w_in(0), w_dn(0), w_in(1), w_in(1), w_dn(1)],
            out_specs=pl.BlockSpec((tm, D_MODEL), lambda i, lo, hi, na: (i, 0)),
        ),
        out_shape=jax.ShapeDtypeStruct((n_rows, D_MODEL), F32),
        compiler_params=pltpu.CompilerParams(dimension_semantics=("arbitrary",), vmem_limit_bytes=VMEM_LIMIT_BYTES),
        name="moe_group",
    )(e_lo, e_hi, n_active, xs, wg, wu, wd, wg, wu, wd)


def _combine_kernel(dest_ref, x1_ref, y_ref, o_ref, buf, sem):
    tm = x1_ref.shape[0]

    for r in range(tm):
        _row_copy(y_ref, dest_ref[0, 0, r], buf, r, sem).start()

    pltpu.make_async_copy(y_ref.at[pl.ds(0, tm), :], buf, sem).wait()
    o_ref[...] = x1_ref[...] + buf[...]


def _combine(dest, x1, y):
    t = x1.shape[0]
    tm = TM_MOE
    return pl.pallas_call(
        _combine_kernel,
        grid=(t // tm,),
        in_specs=[
            pl.BlockSpec((1, 1, tm), lambda i: (i, 0, 0), memory_space=pltpu.SMEM),
            pl.BlockSpec((tm, D_MODEL), lambda i: (i, 0)),
            pl.BlockSpec(memory_space=pl.ANY),
        ],
        out_specs=pl.BlockSpec((tm, D_MODEL), lambda i: (i, 0)),
        out_shape=jax.ShapeDtypeStruct((t, D_MODEL), F32),
        scratch_shapes=[pltpu.VMEM((tm, D_MODEL), F32), pltpu.SemaphoreType.DMA],
        compiler_params=pltpu.CompilerParams(dimension_semantics=("arbitrary",)),
        name="moe_combine",
    )(dest.reshape(t // tm, 1, tm), x1, y)


def _rope_angles(pos, dim):
    inv = 1.0 / (ROPE_THETA ** (jnp.arange(0, dim, 2, dtype=F32) / dim))
    return pos.astype(F32)[:, None] * inv[None, :]


def _rope_tables():
    pos = jnp.arange(SEQ, dtype=jnp.int32)
    z = lambda w: jnp.zeros((SEQ, w), F32)
    ang = _rope_angles(pos, MLA_ROPE_DIM)
    c, s = jnp.cos(ang), jnp.sin(ang)
    m_c = jnp.concatenate([jnp.ones((SEQ, 64), F32), c, c, z(32)], axis=1)
    m_sn = jnp.concatenate([z(64), -s, z(48)], axis=1)
    m_sp = jnp.concatenate([z(80), s, z(32)], axis=1)
    row_pos = pos // GRID_W
    col_pos = pos - row_pos * GRID_W
    ra, ca = _rope_angles(row_pos, HEAD_DIM // 2), _rope_angles(col_pos, HEAD_DIM // 2)
    rc, rs, cc, cs = jnp.cos(ra), jnp.sin(ra), jnp.cos(ca), jnp.sin(ca)
    a_c = jnp.concatenate([rc, rc, cc, cc] * 2, axis=1)
    a_sn = jnp.concatenate([-rs, z(16), -cs, z(16)] * 2, axis=1)
    a_sp = jnp.concatenate([z(16), rs, z(16), cs] * 2, axis=1)
    return jnp.concatenate([m_c, m_sn, m_sp, a_c, a_sn, a_sp], axis=1)


def _rel_bucket(rel):
    half = NUM_BUCKETS // 2
    max_exact = half // 2
    n = jnp.abs(rel)
    nf = jnp.maximum(n, 1).astype(F32)
    log_ratio = jnp.log(nf / max_exact) / math.log(REL_MAX_DISTANCE / max_exact)
    large = jnp.minimum(max_exact + (log_ratio * (half - max_exact)).astype(jnp.int32), half - 1)
    return jnp.where(rel > 0, half, 0) + jnp.where(n < max_exact, n, large)


def _bias_line(table, rel):
    return table[_rel_bucket(rel)].T


def _toeplitz(line, rows, cols, first):
    n = line.shape[-1]
    padded = jnp.pad(line, [(0, 0)] * (line.ndim - 1) + [(0, 1)])
    flat = jnp.tile(padded, rows)[..., :rows * n]
    skew = flat.reshape(line.shape[:-1] + (rows, n))
    return skew[..., first:first + cols]


def _dil_bias(table):
    reach = DIL_QBLK + DIL_HALF - 1
    steps = jnp.arange(-reach, reach + 1, dtype=jnp.int32)
    band = jnp.abs(steps) <= DIL_HALF
    lines = jnp.stack([jnp.where(band[None], _bias_line(table, steps * d), NEG_INF) for _, d in DIL_PATTERNS])
    return _toeplitz(lines, DIL_QBLK, 2 * DIL_QBLK, DIL_QBLK - 1)


def _diff_bias_windows(table):
    rel = jnp.arange(2 * SEQ, dtype=jnp.int32) - SEQ
    line = _bias_line(table, rel)
    starts = [SEQ - (i + 1) * TQ for i in range(SEQ // TQ)]
    return jnp.stack([line[:, s0:s0 + SEQ + TQ] for s0 in starts])


def _block_ones(group):
    idx = np.arange(MXU_DIM) // group
    return jnp.asarray(idx[:, None] == idx[None, :], dtype=BF16)


def _pad_row(v):
    return jnp.pad(v.astype(F32), (0, D_MODEL - v.shape[0]))


def _layer_params(layer, norm1_g, w_in, mla_q_norm_g, mla_kv_norm_g, mla_w_uq, mla_w_ukv, mla_qk_g, dil_qk_g,
                  gqa_qk_g, diff_qk_g, diff_lambda, diff_subln_g, mix_beta, w_out, norm2_g, router_group_w,
                  router_group_b, router_expert_w, router_expert_b):
    w = w_in[layer]
    w_p = _bf(jnp.concatenate([w[:, :416], jnp.zeros((D_MODEL, 96), F32), w[:, 416:]], axis=1))
    uq = mla_w_uq[layer].reshape(MLA_Q_RANK, N_HEADS, MLA_QK_DIM)
    wuq = _bf(jnp.pad(uq, ((0, 0), (0, 0), (0, LANES - MLA_QK_DIM))).reshape(MLA_Q_RANK, 512))
    ukv = mla_w_ukv[layer].reshape(MLA_KV_RANK, N_HEADS, 2 * MLA_NOPE_DIM)
    wuk = _bf(jnp.pad(ukv[:, :, :MLA_NOPE_DIM], ((0, 0), (0, 0), (0, LANES - MLA_NOPE_DIM))).reshape(MLA_KV_RANK, 512))
    wuv = _bf(ukv[:, :, MLA_NOPE_DIM:].reshape(MLA_KV_RANK, 256))

    pad96 = lambda g: jnp.tile(jnp.pad(g, (0, LANES - MLA_QK_DIM)), N_HEADS)
    m0 = jnp.tile(jnp.concatenate([jnp.ones(32, F32), jnp.zeros(32, F32)]), N_HEADS)
    dq = jnp.tile(diff_qk_g[layer, 0], 2 * N_HEADS) * (DIFF_QK_DIM ** -0.5 * LOG2E)
    rows = [
        norm1_g[layer], mla_q_norm_g[layer], mla_kv_norm_g[layer],
        pad96(mla_qk_g[layer, 0]) * (MLA_QK_DIM ** -0.5 * LOG2E), pad96(mla_qk_g[layer, 1]),
        jnp.tile(dil_qk_g[layer, 0], N_HEADS) * (HEAD_DIM ** -0.5 * LOG2E), jnp.tile(dil_qk_g[layer, 1], N_HEADS),
        jnp.tile(gqa_qk_g[layer, 0], N_HEADS) * (HEAD_DIM ** -0.5 * LOG2E), jnp.tile(gqa_qk_g[layer, 1], 2),
        dq * m0, dq * (1.0 - m0), jnp.tile(diff_qk_g[layer, 1], 2 * N_HEADS),
    ]
    gv = jnp.stack([_pad_row(r) for r in rows] + [jnp.zeros(D_MODEL, F32)] * (GAIN_ROWS - len(rows)))

    lambda_init = 0.8 - 0.6 * math.exp(-0.3 * layer)
    lv = diff_lambda[layer].astype(F32)
    lam = (jnp.exp(jnp.sum(lv[0] * lv[1])) - jnp.exp(jnp.sum(lv[2] * lv[3])) + lambda_init).reshape(1)
    sub_gain = (jnp.tile(diff_subln_g[layer], N_HEADS) * (1.0 - lambda_init)).reshape(1, 256)

    cv = jnp.stack([mix_beta[layer], norm2_g[layer]] + [jnp.zeros(D_MODEL, F32)] * 6)
    wr = jnp.concatenate([router_group_w[layer], router_expert_w[layer],
                          jnp.zeros((D_MODEL, LANES - 4 - N_EXPERTS), F32)], axis=1)
    wr_hi = _bf(wr)
    wr_lo = _bf(wr - wr_hi.astype(F32))
    rb = jnp.concatenate([router_group_b[layer], router_expert_b[layer],
                          jnp.zeros(LANES - 4 - N_EXPERTS, F32)]).reshape(1, LANES)
    return dict(w_in=w_p, wuq=wuq, wuk=wuk, wuv=wuv, gv=gv, lam=lam, sub_gain=sub_gain, cv=cv,
                w_out=_bf(w_out[layer]), wr_hi=wr_hi, wr_lo=wr_lo, rb=rb)


def kernel(x, rel_bias, norm1_g, w_in, mla_q_norm_g, mla_kv_norm_g, mla_w_uq, mla_w_ukv, mla_qk_g, dil_qk_g, gqa_qk_g, diff_qk_g, diff_lambda, diff_subln_g, mix_beta, w_out, norm2_g, router_group_w, router_group_b, router_expert_w, router_expert_b, expert_w_gate, expert_w_up, expert_w_down):
    batch, seq, d_model = x.shape
    assert seq == SEQ and d_model == D_MODEL
    depth = w_in.shape[0]
    t = batch * seq

    rope = _rope_tables()
    g32, g64, g128 = _block_ones(32), _block_ones(64), _block_ones(128)
    tri = jnp.asarray(np.arange(TM_OUT)[:, None] < np.arange(TM_OUT)[None, :], dtype=BF16)
    dil_bias = _dil_bias(rel_bias[:, :N_HEADS] * LOG2E)
    diff_bias = _diff_bias_windows(rel_bias[:, N_HEADS:] * LOG2E)

    x2d = x.reshape(t, D_MODEL)
    for layer in range(depth):
        p = _layer_params(layer, norm1_g, w_in, mla_q_norm_g, mla_kv_norm_g, mla_w_uq, mla_w_ukv, mla_qk_g,
                          dil_qk_g, gqa_qk_g, diff_qk_g, diff_lambda, diff_subln_g, mix_beta, w_out, norm2_g,
                          router_group_w, router_group_b, router_expert_w, router_expert_b)
        act, act4, act16 = _inproj(x2d, p["gv"], p["w_in"], p["wuq"], p["wuk"], p["wuv"], rope, g32, g64, g128)
        ya = _attn_full(act, batch, _attn_mla_kernel, A_QA, 512, A_KA, 512, A_VA, 256, "attn_mla")
        yb = _attn_dilated(act, act4, act16, batch, dil_bias)
        yc = _attn_full(act, batch, _attn_gqa_kernel, A_QC, 256, A_KC, 128, A_VC, 128, "attn_gqa")
        yd = _attn_diff(act, batch, p["lam"], diff_bias, p["sub_gain"])
        x1, hp, ri, counts = _outproj(ya, yb, yc, yd, x2d, p["cv"], p["w_out"], p["wr_hi"], p["wr_lo"], p["rb"], tri)
        n_tiles = t // TM_MOE + N_CLASSES
        dest, e_lo, e_hi, n_active = _moe_plan(ri, counts, n_tiles)
        xs = _dispatch(dest, hp, n_tiles * TM_MOE)
        y = _moe_group(e_lo, e_hi, n_active, xs, _bf(expert_w_gate[layer]), _bf(expert_w_up[layer]),
                       _bf(expert_w_down[layer]))
        x2d = _combine(dest, x1, y)
    return x2d.reshape(batch, seq, D_MODEL)
```

```python
import functools
import math

import jax
import jax.numpy as jnp
import numpy as np
from jax import lax
from jax.experimental import pallas as pl
from jax.experimental.pallas import tpu as pltpu

F32 = jnp.float32
BF16 = jnp.bfloat16

D_MODEL = 1024
SEQ = 2048
HEAD_DIM = 64
GRID_W = 64
ROPE_THETA = 10000.0
LOG2E = 1.0 / math.log(2.0)
NORM_EPS = 1e-6
NEG_INF = -1e30
NUM_BUCKETS = 32
REL_MAX_DISTANCE = 1024

N_HEADS = 4
MLA_NOPE_DIM = 64
MLA_ROPE_DIM = 32
MLA_QK_DIM = MLA_NOPE_DIM + MLA_ROPE_DIM
MLA_Q_RANK = 256
MLA_KV_RANK = 128
DIL_PATTERNS = ((128, 1), (512, 4), (2048, 16))
DIL_HALF = 64
DIFF_QK_DIM = 32
N_EXPERTS = 16
EXPERTS_PER_GROUP = 4
D_FF = 512

LANES = 128
MXU_DIM = 256
VMEM_LIMIT_BYTES = 56 * 1024 * 1024

P_CQ, P_CKV, P_KR = 0, 256, 384
P_BQ, P_BK, P_BV = 512, 768, 1024
P_CQ2, P_CK2, P_CV2 = 1280, 1536, 1664
P_DQ, P_DK, P_DV = 1792, 2048, 2304
PROJ_COLS = 2560

A_QB, A_KB, A_VB = 0, 256, 512
A_VA, A_QA, A_KA = 768, 1024, 1536
A_QC, A_KC, A_VC = 2048, 2304, 2432
A_QD0, A_QD1, A_KD, A_VD = 2560, 2816, 3072, 3328
ACT_COLS = 3584
DIL_COLS = 768

(G_NORM1, G_MLA_QN, G_MLA_KVN, G_MLA_Q, G_MLA_K, G_DIL_Q, G_DIL_K, G_GQA_Q, G_GQA_K,
 G_DIFF_Q0, G_DIFF_Q1, G_DIFF_K) = range(12)
GAIN_ROWS = 16

TM_PROJ = 512
TQ = 512
TQ_FULL = 512
TM_OUT = 512
TM_MOE = 256
DIL_QBLK = 128

PAIRS_PER_GROUP = 6
N_CLASSES = 4 * PAIRS_PER_GROUP
CLS_ROWS = 32
XS_COLS = D_MODEL + LANES


def _bf(x):
    return x.astype(BF16)


def _dot(a, b):
    return jnp.dot(a, b, preferred_element_type=F32)


def _dot_nt(a, b):
    return lax.dot_general(a, b, (((1,), (1,)), ((), ())), preferred_element_type=F32)


def _rms(x, width):
    return x * lax.rsqrt(jnp.sum(x * x, axis=-1, keepdims=True) * (1.0 / width) + NORM_EPS)


def _group_sumsq(x, g):
    x2 = x * x
    hi = _bf(x2)
    lo = _bf(x2 - hi.astype(F32))
    w = g.shape[0]
    outs = []
    for c in range(x.shape[1] // w):
        sl = slice(w * c, w * (c + 1))
        outs.append(_dot(hi[:, sl], g) + _dot(lo[:, sl], g))
    return outs[0] if len(outs) == 1 else jnp.concatenate(outs, axis=1)


def _group_rms(x, g, group):
    return x * lax.rsqrt(_group_sumsq(x, g) * (1.0 / group) + NORM_EPS)


def _rope(x, c, s_next, s_prev, half):
    n = x.shape[1]
    return x * c + pltpu.roll(x, n - half, axis=1) * s_next + pltpu.roll(x, half, axis=1) * s_prev


def _inproj_kernel(x_ref, gv_ref, w_ref, wuq_ref, wuk_ref, wuv_ref, rope_ref, g32_ref, g64_ref, g128_ref,
                   o_ref, o4_ref, o16_ref, stage):
    def gain(row, width):
        return gv_ref[row:row + 1, 0:width]

    x = x_ref[...]
    h = _rms(x, D_MODEL) * gain(G_NORM1, D_MODEL)
    proj = _dot(_bf(h), w_ref[...])

    g32 = g32_ref[...]
    g64 = g64_ref[...]
    g128 = g128_ref[...]
    m_c = rope_ref[:, 0:128]
    m_sn = rope_ref[:, 128:256]
    m_sp = rope_ref[:, 256:384]
    a_c = rope_ref[:, 384:512]
    a_sn = rope_ref[:, 512:640]
    a_sp = rope_ref[:, 640:768]

    cq = _rms(proj[:, P_CQ:P_CQ + MLA_Q_RANK], MLA_Q_RANK) * gain(G_MLA_QN, MLA_Q_RANK)
    ckv = _rms(proj[:, P_CKV:P_CKV + MLA_KV_RANK], MLA_KV_RANK) * gain(G_MLA_KVN, MLA_KV_RANK)
    q = _dot(_bf(cq), wuq_ref[...])
    k_nope = _dot(_bf(ckv), wuk_ref[...])
    v = _dot(_bf(ckv), wuv_ref[...])
    k_rope = pltpu.roll(proj[:, P_KR:P_KR + LANES], MLA_NOPE_DIM, axis=1)
    k = k_nope + jnp.concatenate([k_rope] * N_HEADS, axis=1)
    qn = _group_rms(q, g128, MLA_QK_DIM) * gain(G_MLA_Q, 512)
    kn = _group_rms(k, g128, MLA_QK_DIM) * gain(G_MLA_K, 512)
    for g in range(N_HEADS):
        sl = slice(LANES * g, LANES * (g + 1))
        o_ref[:, A_QA + LANES * g:A_QA + LANES * (g + 1)] = _bf(_rope(qn[:, sl], m_c, m_sn, m_sp, MLA_ROPE_DIM // 2))
        o_ref[:, A_KA + LANES * g:A_KA + LANES * (g + 1)] = _bf(_rope(kn[:, sl], m_c, m_sn, m_sp, MLA_ROPE_DIM // 2))
    o_ref[:, A_VA:A_VA + 256] = _bf(v)

    qb = _group_rms(proj[:, P_BQ:P_BQ + 256], g64, HEAD_DIM) * gain(G_DIL_Q, 256)
    kb = _group_rms(proj[:, P_BK:P_BK + 256], g64, HEAD_DIM) * gain(G_DIL_K, 256)
    vb = proj[:, P_BV:P_BV + 256]
    o_ref[:, A_QB:A_QB + 256] = _bf(qb)
    o_ref[:, A_KB:A_KB + 256] = _bf(kb)
    o_ref[:, A_VB:A_VB + 256] = _bf(vb)
    for c, val in enumerate((qb, kb, vb)):
        stage[2 * c] = val[:, 0:LANES]
        stage[2 * c + 1] = val[:, LANES:2 * LANES]
    n_chunks = DIL_COLS // LANES
    for d, ref in ((4, o4_ref), (16, o16_ref)):
        n = x.shape[0] // d
        for r in range(d):
            ref[0, r] = _bf(jnp.concatenate(
                [stage[c, pl.ds(r, n, stride=d), :] for c in range(n_chunks)], axis=1))

    qc = _group_rms(proj[:, P_CQ2:P_CQ2 + 256], g64, HEAD_DIM) * gain(G_GQA_Q, 256)
    for g in range(2):
        sl = slice(LANES * g, LANES * (g + 1))
        o_ref[:, A_QC + LANES * g:A_QC + LANES * (g + 1)] = _bf(_rope(qc[:, sl], a_c, a_sn, a_sp, HEAD_DIM // 4))
    kc = _group_rms(proj[:, P_CK2:P_CK2 + 128], g64[0:128, 0:128], HEAD_DIM) * gain(G_GQA_K, 128)
    o_ref[:, A_KC:A_KC + 128] = _bf(_rope(kc, a_c, a_sn, a_sp, HEAD_DIM // 4))
    o_ref[:, A_VC:A_VC + 128] = _bf(proj[:, P_CV2:P_CV2 + 128])

    dqn = _group_rms(proj[:, P_DQ:P_DQ + 256], g32, DIFF_QK_DIM)
    o_ref[:, A_QD0:A_QD0 + 256] = _bf(dqn * gain(G_DIFF_Q0, 256))
    o_ref[:, A_QD1:A_QD1 + 256] = _bf(dqn * gain(G_DIFF_Q1, 256))
    o_ref[:, A_KD:A_KD + 256] = _bf(_group_rms(proj[:, P_DK:P_DK + 256], g32, DIFF_QK_DIM) * gain(G_DIFF_K, 256))
    o_ref[:, A_VD:A_VD + 256] = _bf(proj[:, P_DV:P_DV + 256])


def _inproj(x2d, gv, w_in, wuq, wuk, wuv, rope, g32, g64, g128):
    t = x2d.shape[0]
    tm = TM_PROJ
    n_pos = SEQ // tm
    const = lambda i: (0, 0)
    return pl.pallas_call(
        _inproj_kernel,
        grid=(t // tm,),
        in_specs=[
            pl.BlockSpec((tm, D_MODEL), lambda i: (i, 0)),
            pl.BlockSpec((GAIN_ROWS, D_MODEL), const),
            pl.BlockSpec((D_MODEL, PROJ_COLS), const),
            pl.BlockSpec((MLA_Q_RANK, 512), const),
            pl.BlockSpec((MLA_KV_RANK, 512), const),
            pl.BlockSpec((MLA_KV_RANK, 256), const),
            pl.BlockSpec((tm, 768), lambda i: (i % n_pos, 0)),
            pl.BlockSpec((MXU_DIM, MXU_DIM), const),
            pl.BlockSpec((MXU_DIM, MXU_DIM), const),
            pl.BlockSpec((MXU_DIM, MXU_DIM), const),
        ],
        out_specs=[
            pl.BlockSpec((tm, ACT_COLS), lambda i: (i, 0)),
            pl.BlockSpec((1, 4, tm // 4, DIL_COLS), lambda i: (i // n_pos, 0, i % n_pos, 0)),
            pl.BlockSpec((1, 16, tm // 16, DIL_COLS), lambda i: (i // n_pos, 0, i % n_pos, 0)),
        ],
        out_shape=[
            jax.ShapeDtypeStruct((t, ACT_COLS), BF16),
            jax.ShapeDtypeStruct((t // SEQ, 4, SEQ // 4, DIL_COLS), BF16),
            jax.ShapeDtypeStruct((t // SEQ, 16, SEQ // 16, DIL_COLS), BF16),
        ],
        scratch_shapes=[pltpu.VMEM((DIL_COLS // LANES, tm, LANES), F32)],
        compiler_params=pltpu.CompilerParams(dimension_semantics=("parallel",), vmem_limit_bytes=VMEM_LIMIT_BYTES),
        name="inproj_prep",
    )(x2d, gv, w_in, wuq, wuk, wuv, rope, g32, g64, g128)


def _softmax_pv(s, v):
    m = jnp.max(s, axis=-1, keepdims=True)
    e = jnp.exp2(s - m)
    l = jnp.sum(e, axis=-1, keepdims=True)
    return _dot(_bf(e), v) / l


def _heads_one_ahead(scores_of, finish):
    pending = scores_of(0)
    for h in range(N_HEADS):
        s = pending
        if h + 1 < N_HEADS:
            pending = scores_of(h + 1)
        finish(h, s)


def _attn_mla_kernel(q_ref, k_ref, v_ref, o_ref):
    def scores_of(h):
        sl = slice(LANES * h, LANES * (h + 1))
        return _dot_nt(q_ref[:, sl], k_ref[:, sl])

    def finish(h, s):
        hs = slice(HEAD_DIM * h, HEAD_DIM * (h + 1))
        o_ref[:, hs] = _softmax_pv(s, v_ref[:, hs])
    _heads_one_ahead(scores_of, finish)


def _attn_gqa_kernel(q_ref, k_ref, v_ref, o_ref):
    group = lambda h: slice(HEAD_DIM * (h // 2), HEAD_DIM * (h // 2 + 1))

    def scores_of(h):
        return _dot_nt(q_ref[:, HEAD_DIM * h:HEAD_DIM * (h + 1)], k_ref[:, group(h)])

    def finish(h, s):
        o_ref[:, HEAD_DIM * h:HEAD_DIM * (h + 1)] = _softmax_pv(s, v_ref[:, group(h)])
    _heads_one_ahead(scores_of, finish)


def _attn_full(act, batch, kernel, q_col, q_w, k_col, k_w, v_col, v_w, name):
    t = act.shape[0]
    tq = TQ_FULL
    nq = SEQ // tq
    return pl.pallas_call(
        kernel,
        grid=(batch, nq),
        in_specs=[
            pl.BlockSpec((tq, q_w), lambda b, i: (b * nq + i, q_col // q_w)),
            pl.BlockSpec((SEQ, k_w), lambda b, i: (b, k_col // k_w)),
            pl.BlockSpec((SEQ, v_w), lambda b, i: (b, v_col // v_w)),
        ],
        out_specs=pl.BlockSpec((tq, 256), lambda b, i: (b * nq + i, 0)),
        out_shape=jax.ShapeDtypeStruct((t, 256), F32),
        compiler_params=pltpu.CompilerParams(
            dimension_semantics=("parallel", "parallel"), vmem_limit_bytes=VMEM_LIMIT_BYTES),
        name=name,
    )(act, act, act)


def _attn_diff_kernel(lam_ref, q0_ref, q1_ref, k_ref, v_ref, win_ref, sg_ref, o_ref, bias_ref):
    @pl.when(pl.program_id(1) == 0)
    def _():
        for h in range(N_HEADS):
            w = jnp.broadcast_to(win_ref[0, h:h + 1, :], (TQ, SEQ + TQ))
            bias_ref[h] = pltpu.roll(w, 0, axis=1, stride=1, stride_axis=0)[:, TQ:TQ + SEQ]

    lam = lam_ref[0]

    def scores_of(h):
        hs = slice(HEAD_DIM * h, HEAD_DIM * (h + 1))
        k = k_ref[:, hs]
        return _dot_nt(q0_ref[:, hs], k), _dot_nt(q1_ref[:, hs], k)

    def finish(h, scores):
        hs = slice(HEAD_DIM * h, HEAD_DIM * (h + 1))
        bias = bias_ref[h]
        s0, s1 = scores[0] + bias, scores[1] + bias
        e0 = jnp.exp2(s0 - jnp.max(s0, axis=-1, keepdims=True))
        e1 = jnp.exp2(s1 - jnp.max(s1, axis=-1, keepdims=True))
        r0 = 1.0 / jnp.sum(e0, axis=-1, keepdims=True)
        r1 = lam / jnp.sum(e1, axis=-1, keepdims=True)
        v = v_ref[:, hs]
        o = r0 * _dot(_bf(e0), v) - r1 * _dot(_bf(e1), v)
        o_ref[:, hs] = _rms(o, HEAD_DIM) * sg_ref[:, hs]
    _heads_one_ahead(scores_of, finish)


def _attn_diff(act, batch, lam, bias_win, sub_gain):
    t = act.shape[0]
    nq = SEQ // TQ
    return pl.pallas_call(
        _attn_diff_kernel,
        grid=(nq, batch),
        in_specs=[
            pl.BlockSpec(memory_space=pltpu.SMEM),
            pl.BlockSpec((TQ, 256), lambda i, b: (b * nq + i, A_QD0 // 256)),
            pl.BlockSpec((TQ, 256), lambda i, b: (b * nq + i, A_QD1 // 256)),
            pl.BlockSpec((SEQ, 256), lambda i, b: (b, A_KD // 256)),
            pl.BlockSpec((SEQ, 256), lambda i, b: (b, A_VD // 256)),
            pl.BlockSpec((1, N_HEADS, SEQ + TQ), lambda i, b: (i, 0, 0)),
            pl.BlockSpec((1, 256), lambda i, b: (0, 0)),
        ],
        out_specs=pl.BlockSpec((TQ, 256), lambda i, b: (b * nq + i, 0)),
        out_shape=jax.ShapeDtypeStruct((t, 256), F32),
        scratch_shapes=[pltpu.VMEM((N_HEADS, TQ, SEQ), F32)],
        compiler_params=pltpu.CompilerParams(
            dimension_semantics=("arbitrary", "arbitrary"), vmem_limit_bytes=VMEM_LIMIT_BYTES),
        name="attn_diff",
    )(lam, act, act, act, act, bias_win, sub_gain)


def _dil_branch(q_of, k_of, v_of, bias_of, n_seq):
    outs, lses = [], []
    for h in range(N_HEADS):
        hs = slice(HEAD_DIM * h, HEAD_DIM * (h + 1))
        q, k, v = q_of(hs), k_of(hs), v_of(hs)
        n = q.shape[1]
        if n == DIL_QBLK:
            kw, vw = k, v
            bias = bias_of(h)[:, DIL_HALF:DIL_HALF + DIL_QBLK]
            edge = None
        else:
            n_blk = n // DIL_QBLK

            def windows(x):
                zeros = jnp.zeros((n_seq, DIL_HALF, HEAD_DIM), BF16)
                xp = jnp.concatenate([zeros, x, zeros], axis=1)
                first = xp[:, 0:n].reshape(n_seq * n_blk, DIL_QBLK, HEAD_DIM)
                second = xp[:, DIL_QBLK:DIL_QBLK + n].reshape(n_seq * n_blk, DIL_QBLK, HEAD_DIM)
                return jnp.concatenate([first, second], axis=1)
            kw, vw = windows(k), windows(v)
            q = q.reshape(n_seq * n_blk, DIL_QBLK, HEAD_DIM)
            bias = bias_of(h)
            blk = lax.broadcasted_iota(jnp.int32, (n_seq * n_blk, 1, 2 * DIL_QBLK), 0) % n_blk
            col = lax.broadcasted_iota(jnp.int32, (n_seq * n_blk, 1, 2 * DIL_QBLK), 2)
            key_pos = col - DIL_HALF + blk * DIL_QBLK
            edge = jnp.where(jnp.logical_and(key_pos >= 0, key_pos < n), 0.0, NEG_INF)
        s = jnp.einsum("bqe,bke->bqk", q, kw, preferred_element_type=F32) + bias[None]
        if edge is not None:
            s = s + edge
        m = jnp.max(s, axis=-1, keepdims=True)
        e = jnp.exp2(s - m)
        l = jnp.sum(e, axis=-1, keepdims=True)
        o = jnp.einsum("bqk,bke->bqe", _bf(e), vw, preferred_element_type=F32) / l
        lse = jnp.broadcast_to(m + jnp.log2(l), o.shape)
        outs.append(o.reshape(n_seq, n, HEAD_DIM))
        lses.append(lse.reshape(n_seq, n, HEAD_DIM))
    cat = lambda xs: [jnp.concatenate(xs[2 * c:2 * c + 2], axis=2) for c in range(2)]
    return cat(outs), cat(lses)


def _dil_kernel(n1_ref, r4_ref, r16_ref, bias_ref, o_ref, o1, l1, o4, l4, out):
    col_of = lambda base: (lambda hs: slice(base + hs.start, base + hs.stop))
    q_cols, k_cols, v_cols = col_of(0), col_of(256), col_of(512)

    oc, lc = _dil_branch(lambda hs: n1_ref[:, q_cols(hs)][None], lambda hs: n1_ref[:, k_cols(hs)][None],
                         lambda hs: n1_ref[:, v_cols(hs)][None], lambda h: bias_ref[0, h], 1)
    for c in range(2):
        o1[c] = oc[c][0]
        l1[c] = lc[c][0]

    oc, lc = _dil_branch(lambda hs: r4_ref[0, :, :, q_cols(hs)], lambda hs: r4_ref[0, :, :, k_cols(hs)],
                         lambda hs: r4_ref[0, :, :, v_cols(hs)], lambda h: bias_ref[1, h], 4)
    for c in range(2):
        for r in range(4):
            tok = pl.ds(r, SEQ // 4, stride=4)
            o4[c, tok, :] = oc[c][r]
            l4[c, tok, :] = lc[c][r]

    oc, lc = _dil_branch(lambda hs: r16_ref[0, :, :, q_cols(hs)], lambda hs: r16_ref[0, :, :, k_cols(hs)],
                         lambda hs: r16_ref[0, :, :, v_cols(hs)], lambda h: bias_ref[2, h], 16)
    toks = [pl.ds(r, SEQ // 16, stride=16) for r in range(16)]
    for c in range(2):
        gather = lambda ref: jnp.stack([ref[c, tok, :] for tok in toks])
        la, lb = gather(l1), gather(l4)
        mx = jnp.maximum(jnp.maximum(la, lb), lc[c])
        wa, wb, wc = jnp.exp2(la - mx), jnp.exp2(lb - mx), jnp.exp2(lc[c] - mx)
        den = wa + wb + wc
        res = (wa / den) * gather(o1) + (wb / den) * gather(o4) + (wc / den) * oc[c]
        for r, tok in enumerate(toks):
            out[c, tok, :] = res[r]
    o_ref[:, 0:LANES] = out[0]
    o_ref[:, LANES:2 * LANES] = out[1]


def _attn_dilated(act, act4, act16, batch, bias_tabs):
    t = act.shape[0]
    nat = pltpu.VMEM((2, SEQ, LANES), F32)
    return pl.pallas_call(
        _dil_kernel,
        grid=(batch,),
        in_specs=[
            pl.BlockSpec((SEQ, DIL_COLS), lambda b: (b, 0)),
            pl.BlockSpec((1, 4, SEQ // 4, DIL_COLS), lambda b: (b, 0, 0, 0)),
            pl.BlockSpec((1, 16, SEQ // 16, DIL_COLS), lambda b: (b, 0, 0, 0)),
            pl.BlockSpec((3, N_HEADS, DIL_QBLK, 2 * DIL_QBLK), lambda b: (0, 0, 0, 0)),
        ],
        out_specs=pl.BlockSpec((SEQ, 256), lambda b: (b, 0)),
        out_shape=jax.ShapeDtypeStruct((t, 256), F32),
        scratch_shapes=[nat, nat, nat, nat, nat],
        compiler_params=pltpu.CompilerParams(dimension_semantics=("parallel",), vmem_limit_bytes=VMEM_LIMIT_BYTES),
        name="attn_dilated",
    )(act, act4, act16, bias_tabs)


def _route(lt):
    g = [lt[i:i + 1, :] for i in range(4)]
    gmax = jnp.maximum(jnp.maximum(g[0], g[1]), jnp.maximum(g[2], g[3]))
    gsum = sum(jnp.exp(gi - gmax) for gi in g)
    g_w = 1.0 / gsum
    gidx = jnp.where(g[0] == gmax, 0, jnp.where(g[1] == gmax, 1, jnp.where(g[2] == gmax, 2, 3)))
    el = []
    for j in range(EXPERTS_PER_GROUP):
        acc = jnp.zeros_like(g[0])
        for i in range(4):
            r = 4 + EXPERTS_PER_GROUP * i + j
            acc = jnp.where(gidx == i, lt[r:r + 1, :], acc)
        el.append(acc)
    emax = jnp.maximum(jnp.maximum(el[0], el[1]), jnp.maximum(el[2], el[3]))
    ee = [jnp.exp(e - emax) for e in el]
    esum = ee[0] + ee[1] + ee[2] + ee[3]
    p = [e / esum for e in ee]
    p1 = jnp.maximum(jnp.maximum(p[0], p[1]), jnp.maximum(p[2], p[3]))
    i1 = jnp.where(p[0] == p1, 0, jnp.where(p[1] == p1, 1, jnp.where(p[2] == p1, 2, 3)))
    pm = [jnp.where(i1 == j, -1.0, p[j]) for j in range(4)]
    p2 = jnp.maximum(jnp.maximum(pm[0], pm[1]), jnp.maximum(pm[2], pm[3]))
    i2 = jnp.where(pm[0] == p2, 0, jnp.where(pm[1] == p2, 1, jnp.where(pm[2] == p2, 2, 3)))
    den = p1 + p2
    return gidx, i1, i2, g_w * (p1 / den), g_w * (p2 / den)


def _pair_offset(lo):
    return jnp.where(lo == 0, 0, jnp.where(lo == 1, 3, 5))


def _outproj_kernel(ya_ref, yb_ref, yc_ref, yd_ref, x_ref, cv_ref, w_ref, wrh_ref, wrl_ref, rb_ref, tri_ref,
                    x1_ref, hp_ref, ri_ref, cnt_out_ref, cnt_ref):
    tm = x_ref.shape[0]

    @pl.when(pl.program_id(0) == 0)
    def _():
        cnt_ref[...] = jnp.zeros_like(cnt_ref)

    mixed = jnp.concatenate(
        [_rms(ya_ref[...], 256), _rms(yb_ref[...], 256), _rms(yc_ref[...], 256), yd_ref[...]], axis=1)
    mixed = mixed * cv_ref[0:1, :]
    x1 = x_ref[...] + _dot(_bf(mixed), w_ref[...])
    x1_ref[...] = x1
    h2 = _rms(x1, D_MODEL) * cv_ref[1:2, :]
    h_hi = _bf(h2)
    h_rt = h_hi.astype(F32)
    h_lo = _bf(h2 - h_rt)
    logits = _dot(h_hi, wrh_ref[...]) + _dot(h_hi, wrl_ref[...]) + _dot(h_lo, wrh_ref[...]) + rb_ref[...]
    gidx, i1, i2, wa, wb = _route(logits.T)

    lo, hi = jnp.minimum(i1, i2), jnp.maximum(i1, i2)
    first_is_lo = i1 < i2
    w_lo, w_hi = jnp.where(first_is_lo, wa, wb), jnp.where(first_is_lo, wb, wa)
    cls = gidx * PAIRS_PER_GROUP + _pair_offset(lo) + (hi - lo - 1)

    rows = lax.broadcasted_iota(jnp.int32, (CLS_ROWS, tm), 0)
    onehot = rows == cls
    oh = jnp.where(onehot, 1.0, 0.0)
    prefix = _dot(_bf(oh), tri_ref[...])
    before = cnt_ref[:, 0:1] + prefix
    rank = jnp.sum(jnp.where(onehot, before, 0.0), axis=0, keepdims=True)
    cnt_ref[...] = cnt_ref[...] + jnp.sum(oh, axis=1, keepdims=True)
    cnt_out_ref[...] = cnt_ref[...]
    ri_ref[...] = jnp.concatenate([cls, rank.astype(jnp.int32), jnp.zeros((6, tm), jnp.int32)], axis=0)

    hp_ref[:, 0:D_MODEL] = h_rt
    rows_w = lax.broadcasted_iota(jnp.int32, (LANES, tm), 0)
    w_t = jnp.where(rows_w == 0, w_lo, jnp.where(rows_w == 1, w_hi, 0.0))
    hp_ref[:, D_MODEL:XS_COLS] = w_t.T


def _outproj(ya, yb, yc, yd, x2d, cv, w_out, wr_hi, wr_lo, rb, tri):
    t = x2d.shape[0]
    tm = TM_OUT
    row = lambda w: pl.BlockSpec((tm, w), lambda i: (i, 0))
    const = lambda shape: pl.BlockSpec(shape, lambda i: (0, 0))
    return pl.pallas_call(
        _outproj_kernel,
        grid=(t // tm,),
        in_specs=[row(256), row(256), row(256), row(256), row(D_MODEL), const((8, D_MODEL)),
                  const((D_MODEL, D_MODEL)), const((D_MODEL, LANES)), const((D_MODEL, LANES)), const((1, LANES)),
                  const((tm, tm))],
        out_specs=[row(D_MODEL), row(XS_COLS), pl.BlockSpec((8, tm), lambda i: (0, i)), const((CLS_ROWS, LANES))],
        out_shape=[jax.ShapeDtypeStruct((t, D_MODEL), F32), jax.ShapeDtypeStruct((t, XS_COLS), F32),
                   jax.ShapeDtypeStruct((8, t), jnp.int32), jax.ShapeDtypeStruct((CLS_ROWS, LANES), F32)],
        scratch_shapes=[pltpu.VMEM((CLS_ROWS, LANES), F32)],
        compiler_params=pltpu.CompilerParams(dimension_semantics=("arbitrary",), vmem_limit_bytes=VMEM_LIMIT_BYTES),
        name="outproj_router",
    )(ya, yb, yc, yd, x2d, cv, w_out, wr_hi, wr_lo, rb, tri)


def _moe_plan(ri, counts, n_tiles):
    cls, rank = ri[0], ri[1]
    cnt = counts[:N_CLASSES, 0].astype(jnp.int32)
    padded = ((cnt + TM_MOE - 1) // TM_MOE) * TM_MOE
    ends = jnp.cumsum(padded)
    offs = ends - padded
    classes = jnp.arange(N_CLASSES, dtype=jnp.int32)
    dest = jnp.sum(jnp.where(cls[:, None] == classes[None, :], offs[None, :], 0), axis=1) + rank
    tile_start = jnp.arange(n_tiles, dtype=jnp.int32) * TM_MOE
    tile_cls = jnp.sum((tile_start[:, None] >= ends[None, :]).astype(jnp.int32), axis=1)
    tile_cls = jnp.minimum(tile_cls, N_CLASSES - 1)
    group, pair = tile_cls // PAIRS_PER_GROUP, tile_cls % PAIRS_PER_GROUP
    lo = (pair >= 3).astype(jnp.int32) + (pair >= 5).astype(jnp.int32)
    hi = pair - _pair_offset(lo) + lo + 1
    n_active = (ends[-1] // TM_MOE).reshape(1)
    return dest, group * EXPERTS_PER_GROUP + lo, group * EXPERTS_PER_GROUP + hi, n_active


def _row_copy(src_ref, src_row, dst_ref, dst_row, sem):
    return pltpu.make_async_copy(src_ref.at[pl.ds(src_row, 1), :], dst_ref.at[pl.ds(dst_row, 1), :], sem)


def _dispatch_kernel(dest_ref, h_ref, xs_in_ref, xs_ref, sem):
    del xs_in_ref
    tm = h_ref.shape[0]

    for r in range(tm):
        _row_copy(h_ref, r, xs_ref, dest_ref[0, 0, r], sem).start()

    pltpu.make_async_copy(h_ref, xs_ref.at[pl.ds(0, tm), :], sem).wait()


def _dispatch(dest, hp, n_rows):
    t = hp.shape[0]
    tm = TM_MOE
    return pl.pallas_call(
        _dispatch_kernel,
        grid=(t // tm,),
        in_specs=[
            pl.BlockSpec((1, 1, tm), lambda i: (i, 0, 0), memory_space=pltpu.SMEM),
            pl.BlockSpec((tm, XS_COLS), lambda i: (i, 0)),
            pl.BlockSpec(memory_space=pl.ANY),
        ],
        out_specs=pl.BlockSpec(memory_space=pl.ANY),
        out_shape=jax.ShapeDtypeStruct((n_rows, XS_COLS), F32),
        scratch_shapes=[pltpu.SemaphoreType.DMA],
        input_output_aliases={2: 0},
        compiler_params=pltpu.CompilerParams(dimension_semantics=("arbitrary",)),
        name="moe_dispatch",
    )(dest.reshape(t // tm, 1, tm), hp, jnp.zeros((n_rows, XS_COLS), F32))


def _moe_group_kernel(elo_ref, ehi_ref, nact_ref, xs_ref, wg0, wu0, wd0, wg1, wu1, wd1, o_ref):
    del elo_ref, ehi_ref
    active = pl.program_id(0) < nact_ref[0]

    @pl.when(active)
    def _():
        x = _bf(xs_ref[:, 0:D_MODEL])
        gate = xs_ref[:, D_MODEL:XS_COLS]

        def expert(wg, wu, wd):
            a = _dot(x, wg[0])
            hid = a * (1.0 / (1.0 + jnp.exp(-a))) * _dot(x, wu[0])
            return _dot(_bf(hid), wd[0])
        o_ref[...] = gate[:, 0:1] * expert(wg0, wu0, wd0) + gate[:, 1:2] * expert(wg1, wu1, wd1)

    @pl.when(jnp.logical_not(active))
    def _():
        o_ref[...] = jnp.zeros_like(o_ref)


def _moe_group(e_lo, e_hi, n_active, xs, wg, wu, wd):
    n_rows = xs.shape[0]
    tm = TM_MOE
    w_in = lambda which: pl.BlockSpec((1, D_MODEL, D_FF), lambda i, lo, hi, na: ((lo, hi)[which][i], 0, 0))
    w_dn = lambda which: pl.BlockSpec((1, D_FF, D_MODEL), lambda i, lo, hi, na: ((lo, hi)[which][i], 0, 0))
    return pl.pallas_call(
        _moe_group_kernel,
        grid_spec=pltpu.PrefetchScalarGridSpec(
            num_scalar_prefetch=3,
            grid=(n_rows // tm,),
            in_specs=[pl.BlockSpec((tm, XS_COLS), lambda i, lo, hi, na: (i, 0)),
                      w_in(0), w_in(0), w_dn(0), w_in(1), w_in(1), w_dn(1)],
            out_specs=pl.BlockSpec((tm, D_MODEL), lambda i, lo, hi, na: (i, 0)),
        ),
        out_shape=jax.ShapeDtypeStruct((n_rows, D_MODEL), F32),
        compiler_params=pltpu.CompilerParams(dimension_semantics=("arbitrary",), vmem_limit_bytes=VMEM_LIMIT_BYTES),
        name="moe_group",
    )(e_lo, e_hi, n_active, xs, wg, wu, wd, wg, wu, wd)


def _combine_kernel(dest_ref, x1_ref, y_ref, o_ref, buf, sem):
    tm = x1_ref.shape[0]

    for r in range(tm):
        _row_copy(y_ref, dest_ref[0, 0, r], buf, r, sem).start()

    pltpu.make_async_copy(y_ref.at[pl.ds(0, tm), :], buf, sem).wait()
    o_ref[...] = x1_ref[...] + buf[...]


def _combine(dest, x1, y):
    t = x1.shape[0]
    tm = TM_MOE
    return pl.pallas_call(
        _combine_kernel,
        grid=(t // tm,),
        in_specs=[
            pl.BlockSpec((1, 1, tm), lambda i: (i, 0, 0), memory_space=pltpu.SMEM),
            pl.BlockSpec((tm, D_MODEL), lambda i: (i, 0)),
            pl.BlockSpec(memory_space=pl.ANY),
        ],
        out_specs=pl.BlockSpec((tm, D_MODEL), lambda i: (i, 0)),
        out_shape=jax.ShapeDtypeStruct((t, D_MODEL), F32),
        scratch_shapes=[pltpu.VMEM((tm, D_MODEL), F32), pltpu.SemaphoreType.DMA],
        compiler_params=pltpu.CompilerParams(dimension_semantics=("arbitrary",)),
        name="moe_combine",
    )(dest.reshape(t // tm, 1, tm), x1, y)


def _rope_angles(pos, dim):
    inv = 1.0 / (ROPE_THETA ** (jnp.arange(0, dim, 2, dtype=F32) / dim))
    return pos.astype(F32)[:, None] * inv[None, :]


def _rope_tables():
    pos = jnp.arange(SEQ, dtype=jnp.int32)
    z = lambda w: jnp.zeros((SEQ, w), F32)
    ang = _rope_angles(pos, MLA_ROPE_DIM)
    c, s = jnp.cos(ang), jnp.sin(ang)
    m_c = jnp.concatenate([jnp.ones((SEQ, 64), F32), c, c, z(32)], axis=1)
    m_sn = jnp.concatenate([z(64), -s, z(48)], axis=1)
    m_sp = jnp.concatenate([z(80), s, z(32)], axis=1)
    row_pos = pos // GRID_W
    col_pos = pos - row_pos * GRID_W
    ra, ca = _rope_angles(row_pos, HEAD_DIM // 2), _rope_angles(col_pos, HEAD_DIM // 2)
    rc, rs, cc, cs = jnp.cos(ra), jnp.sin(ra), jnp.cos(ca), jnp.sin(ca)
    a_c = jnp.concatenate([rc, rc, cc, cc] * 2, axis=1)
    a_sn = jnp.concatenate([-rs, z(16), -cs, z(16)] * 2, axis=1)
    a_sp = jnp.concatenate([z(16), rs, z(16), cs] * 2, axis=1)
    return jnp.concatenate([m_c, m_sn, m_sp, a_c, a_sn, a_sp], axis=1)


def _rel_bucket(rel):
    half = NUM_BUCKETS // 2
    max_exact = half // 2
    n = jnp.abs(rel)
    nf = jnp.maximum(n, 1).astype(F32)
    log_ratio = jnp.log(nf / max_exact) / math.log(REL_MAX_DISTANCE / max_exact)
    large = jnp.minimum(max_exact + (log_ratio * (half - max_exact)).astype(jnp.int32), half - 1)
    return jnp.where(rel > 0, half, 0) + jnp.where(n < max_exact, n, large)


def _bias_line(table, rel):
    return table[_rel_bucket(rel)].T


def _toeplitz(line, rows, cols, first):
    n = line.shape[-1]
    padded = jnp.pad(line, [(0, 0)] * (line.ndim - 1) + [(0, 1)])
    flat = jnp.tile(padded, rows)[..., :rows * n]
    skew = flat.reshape(line.shape[:-1] + (rows, n))
    return skew[..., first:first + cols]


def _dil_bias(table):
    reach = DIL_QBLK + DIL_HALF - 1
    steps = jnp.arange(-reach, reach + 1, dtype=jnp.int32)
    band = jnp.abs(steps) <= DIL_HALF
    lines = jnp.stack([jnp.where(band[None], _bias_line(table, steps * d), NEG_INF) for _, d in DIL_PATTERNS])
    return _toeplitz(lines, DIL_QBLK, 2 * DIL_QBLK, DIL_QBLK - 1)


def _diff_bias_windows(table):
    rel = jnp.arange(2 * SEQ, dtype=jnp.int32) - SEQ
    line = _bias_line(table, rel)
    starts = [SEQ - (i + 1) * TQ for i in range(SEQ // TQ)]
    return jnp.stack([line[:, s0:s0 + SEQ + TQ] for s0 in starts])


def _block_ones(group):
    idx = np.arange(MXU_DIM) // group
    return jnp.asarray(idx[:, None] == idx[None, :], dtype=BF16)


def _pad_row(v):
    return jnp.pad(v.astype(F32), (0, D_MODEL - v.shape[0]))


def _layer_params(layer, norm1_g, w_in, mla_q_norm_g, mla_kv_norm_g, mla_w_uq, mla_w_ukv, mla_qk_g, dil_qk_g,
                  gqa_qk_g, diff_qk_g, diff_lambda, diff_subln_g, mix_beta, w_out, norm2_g, router_group_w,
                  router_group_b, router_expert_w, router_expert_b):
    w = w_in[layer]
    w_p = _bf(jnp.concatenate([w[:, :416], jnp.zeros((D_MODEL, 96), F32), w[:, 416:]], axis=1))
    uq = mla_w_uq[layer].reshape(MLA_Q_RANK, N_HEADS, MLA_QK_DIM)
    wuq = _bf(jnp.pad(uq, ((0, 0), (0, 0), (0, LANES - MLA_QK_DIM))).reshape(MLA_Q_RANK, 512))
    ukv = mla_w_ukv[layer].reshape(MLA_KV_RANK, N_HEADS, 2 * MLA_NOPE_DIM)
    wuk = _bf(jnp.pad(ukv[:, :, :MLA_NOPE_DIM], ((0, 0), (0, 0), (0, LANES - MLA_NOPE_DIM))).reshape(MLA_KV_RANK, 512))
    wuv = _bf(ukv[:, :, MLA_NOPE_DIM:].reshape(MLA_KV_RANK, 256))

    pad96 = lambda g: jnp.tile(jnp.pad(g, (0, LANES - MLA_QK_DIM)), N_HEADS)
    m0 = jnp.tile(jnp.concatenate([jnp.ones(32, F32), jnp.zeros(32, F32)]), N_HEADS)
    dq = jnp.tile(diff_qk_g[layer, 0], 2 * N_HEADS) * (DIFF_QK_DIM ** -0.5 * LOG2E)
    rows = [
        norm1_g[layer], mla_q_norm_g[layer], mla_kv_norm_g[layer],
        pad96(mla_qk_g[layer, 0]) * (MLA_QK_DIM ** -0.5 * LOG2E), pad96(mla_qk_g[layer, 1]),
        jnp.tile(dil_qk_g[layer, 0], N_HEADS) * (HEAD_DIM ** -0.5 * LOG2E), jnp.tile(dil_qk_g[layer, 1], N_HEADS),
        jnp.tile(gqa_qk_g[layer, 0], N_HEADS) * (HEAD_DIM ** -0.5 * LOG2E), jnp.tile(gqa_qk_g[layer, 1], 2),
        dq * m0, dq * (1.0 - m0), jnp.tile(diff_qk_g[layer, 1], 2 * N_HEADS),
    ]
    gv = jnp.stack([_pad_row(r) for r in rows] + [jnp.zeros(D_MODEL, F32)] * (GAIN_ROWS - len(rows)))

    lambda_init = 0.8 - 0.6 * math.exp(-0.3 * layer)
    lv = diff_lambda[layer].astype(F32)
    lam = (jnp.exp(jnp.sum(lv[0] * lv[1])) - jnp.exp(jnp.sum(lv[2] * lv[3])) + lambda_init).reshape(1)
    sub_gain = (jnp.tile(diff_subln_g[layer], N_HEADS) * (1.0 - lambda_init)).reshape(1, 256)

    cv = jnp.stack([mix_beta[layer], norm2_g[layer]] + [jnp.zeros(D_MODEL, F32)] * 6)
    wr = jnp.concatenate([router_group_w[layer], router_expert_w[layer],
                          jnp.zeros((D_MODEL, LANES - 4 - N_EXPERTS), F32)], axis=1)
    wr_hi = _bf(wr)
    wr_lo = _bf(wr - wr_hi.astype(F32))
    rb = jnp.concatenate([router_group_b[layer], router_expert_b[layer],
                          jnp.zeros(LANES - 4 - N_EXPERTS, F32)]).reshape(1, LANES)
    return dict(w_in=w_p, wuq=wuq, wuk=wuk, wuv=wuv, gv=gv, lam=lam, sub_gain=sub_gain, cv=cv,
                w_out=_bf(w_out[layer]), wr_hi=wr_hi, wr_lo=wr_lo, rb=rb)


def kernel(x, rel_bias, norm1_g, w_in, mla_q_norm_g, mla_kv_norm_g, mla_w_uq, mla_w_ukv, mla_qk_g, dil_qk_g, gqa_qk_g, diff_qk_g, diff_lambda, diff_subln_g, mix_beta, w_out, norm2_g, router_group_w, router_group_b, router_expert_w, router_expert_b, expert_w_gate, expert_w_up, expert_w_down):
    batch, seq, d_model = x.shape
    assert seq == SEQ and d_model == D_MODEL
    depth = w_in.shape[0]
    t = batch * seq

    rope = _rope_tables()
    g32, g64, g128 = _block_ones(32), _block_ones(64), _block_ones(128)
    tri = jnp.asarray(np.arange(TM_OUT)[:, None] < np.arange(TM_OUT)[None, :], dtype=BF16)
    dil_bias = _dil_bias(rel_bias[:, :N_HEADS] * LOG2E)
    diff_bias = _diff_bias_windows(rel_bias[:, N_HEADS:] * LOG2E)

    x2d = x.reshape(t, D_MODEL)
    for layer in range(depth):
        p = _layer_params(layer, norm1_g, w_in, mla_q_norm_g, mla_kv_norm_g, mla_w_uq, mla_w_ukv, mla_qk_g,
                          dil_qk_g, gqa_qk_g, diff_qk_g, diff_lambda, diff_subln_g, mix_beta, w_out, norm2_g,
                          router_group_w, router_group_b, router_expert_w, router_expert_b)
        act, act4, act16 = _inproj(x2d, p["gv"], p["w_in"], p["wuq"], p["wuk"], p["wuv"], rope, g32, g64, g128)
        ya = _attn_full(act, batch, _attn_mla_kernel, A_QA, 512, A_KA, 512, A_VA, 256, "attn_mla")
        yb = _attn_dilated(act, act4, act16, batch, dil_bias)
        yc = _attn_full(act, batch, _attn_gqa_kernel, A_QC, 256, A_KC, 128, A_VC, 128, "attn_gqa")
        yd = _attn_diff(act, batch, p["lam"], diff_bias, p["sub_gain"])
        x1, hp, ri, counts = _outproj(ya, yb, yc, yd, x2d, p["cv"], p["w_out"], p["wr_hi"], p["wr_lo"], p["rb"], tri)
        n_tiles = t // TM_MOE + N_CLASSES
        dest, e_lo, e_hi, n_active = _moe_plan(ri, counts, n_tiles)
        xs = _dispatch(dest, hp, n_tiles * TM_MOE)
        y = _moe_group(e_lo, e_hi, n_active, xs, _bf(expert_w_gate[layer]), _bf(expert_w_up[layer]),
                       _bf(expert_w_down[layer]))
        x2d = _combine(dest, x1, y)
    return x2d.reshape(batch, seq, D_MODEL)
```

```python
import functools
import math

import jax
import jax.numpy as jnp
import numpy as np
from jax import lax
from jax.experimental import pallas as pl
from jax.experimental.pallas import tpu as pltpu

F32 = jnp.float32
BF16 = jnp.bfloat16

D_MODEL = 1024
SEQ = 2048
HEAD_DIM = 64
GRID_W = 64
ROPE_THETA = 10000.0
LOG2E = 1.0 / math.log(2.0)
NORM_EPS = 1e-6
NEG_INF = -1e30
NUM_BUCKETS = 32
REL_MAX_DISTANCE = 1024

N_HEADS = 4
MLA_NOPE_DIM = 64
MLA_ROPE_DIM = 32
MLA_QK_DIM = MLA_NOPE_DIM + MLA_ROPE_DIM
MLA_Q_RANK = 256
MLA_KV_RANK = 128
DIL_PATTERNS = ((128, 1), (512, 4), (2048, 16))
DIL_HALF = 64
DIFF_QK_DIM = 32
N_EXPERTS = 16
EXPERTS_PER_GROUP = 4
D_FF = 512

LANES = 128
MXU_DIM = 256
VMEM_LIMIT_BYTES = 56 * 1024 * 1024

P_CQ, P_CKV, P_KR = 0, 256, 384
P_BQ, P_BK, P_BV = 512, 768, 1024
P_CQ2, P_CK2, P_CV2 = 1280, 1536, 1664
P_DQ, P_DK, P_DV = 1792, 2048, 2304
PROJ_COLS = 2560

A_QB, A_KB, A_VB = 0, 256, 512
A_VA, A_QA, A_KA = 768, 1024, 1536
A_QC, A_KC, A_VC = 2048, 2304, 2432
A_QD0, A_QD1, A_KD, A_VD = 2560, 2816, 3072, 3328
ACT_COLS = 3584
DIL_COLS = 768

(G_NORM1, G_MLA_QN, G_MLA_KVN, G_MLA_Q, G_MLA_K, G_DIL_Q, G_DIL_K, G_GQA_Q, G_GQA_K,
 G_DIFF_Q0, G_DIFF_Q1, G_DIFF_K) = range(12)
GAIN_ROWS = 16

TM_PROJ = 512
TQ = 256
TQ_FULL = 512
TM_OUT = 512
TM_MOE = 256
DIL_QBLK = 128

PAIRS_PER_GROUP = 6
N_CLASSES = 4 * PAIRS_PER_GROUP
CLS_ROWS = 32
ROW_TILE_SUBLANES = D_MODEL // LANES
XS_SUBLANES = ROW_TILE_SUBLANES + 1


def _bf(x):
    return x.astype(BF16)


def _dot(a, b):
    return jnp.dot(a, b, preferred_element_type=F32)


def _dot_nt(a, b):
    return lax.dot_general(a, b, (((1,), (1,)), ((), ())), preferred_element_type=F32)


def _rms(x, width):
    return x * lax.rsqrt(jnp.sum(x * x, axis=-1, keepdims=True) * (1.0 / width) + NORM_EPS)


def _group_sumsq(x, g):
    x2 = x * x
    hi = _bf(x2)
    lo = _bf(x2 - hi.astype(F32))
    w = g.shape[0]
    outs = []
    for c in range(x.shape[1] // w):
        sl = slice(w * c, w * (c + 1))
        outs.append(_dot(hi[:, sl], g) + _dot(lo[:, sl], g))
    return outs[0] if len(outs) == 1 else jnp.concatenate(outs, axis=1)


def _group_rms(x, g, group):
    return x * lax.rsqrt(_group_sumsq(x, g) * (1.0 / group) + NORM_EPS)


def _rope(x, c, s_next, s_prev, half):
    n = x.shape[1]
    return x * c + pltpu.roll(x, n - half, axis=1) * s_next + pltpu.roll(x, half, axis=1) * s_prev


def _inproj_kernel(x_ref, gv_ref, w_ref, wuq_ref, wuk_ref, wuv_ref, rope_ref, g32_ref, g64_ref, g128_ref,
                   o_ref, o4_ref, o16_ref, stage):
    def gain(row, width):
        return gv_ref[row:row + 1, 0:width]

    x = x_ref[...]
    h = _rms(x, D_MODEL) * gain(G_NORM1, D_MODEL)
    proj = _dot(_bf(h), w_ref[...])

    g32 = g32_ref[...]
    g64 = g64_ref[...]
    g128 = g128_ref[...]
    m_c = rope_ref[:, 0:128]
    m_sn = rope_ref[:, 128:256]
    m_sp = rope_ref[:, 256:384]
    a_c = rope_ref[:, 384:512]
    a_sn = rope_ref[:, 512:640]
    a_sp = rope_ref[:, 640:768]

    cq = _rms(proj[:, P_CQ:P_CQ + MLA_Q_RANK], MLA_Q_RANK) * gain(G_MLA_QN, MLA_Q_RANK)
    ckv = _rms(proj[:, P_CKV:P_CKV + MLA_KV_RANK], MLA_KV_RANK) * gain(G_MLA_KVN, MLA_KV_RANK)
    q = _dot(_bf(cq), wuq_ref[...])
    k_nope = _dot(_bf(ckv), wuk_ref[...])
    v = _dot(_bf(ckv), wuv_ref[...])
    k_rope = pltpu.roll(proj[:, P_KR:P_KR + LANES], MLA_NOPE_DIM, axis=1)
    k = k_nope + jnp.concatenate([k_rope] * N_HEADS, axis=1)
    qn = _group_rms(q, g128, MLA_QK_DIM) * gain(G_MLA_Q, 512)
    kn = _group_rms(k, g128, MLA_QK_DIM) * gain(G_MLA_K, 512)
    for g in range(N_HEADS):
        sl = slice(LANES * g, LANES * (g + 1))
        o_ref[:, A_QA + LANES * g:A_QA + LANES * (g + 1)] = _bf(_rope(qn[:, sl], m_c, m_sn, m_sp, MLA_ROPE_DIM // 2))
        o_ref[:, A_KA + LANES * g:A_KA + LANES * (g + 1)] = _bf(_rope(kn[:, sl], m_c, m_sn, m_sp, MLA_ROPE_DIM // 2))
    o_ref[:, A_VA:A_VA + 256] = _bf(v)

    qb = _group_rms(proj[:, P_BQ:P_BQ + 256], g64, HEAD_DIM) * gain(G_DIL_Q, 256)
    kb = _group_rms(proj[:, P_BK:P_BK + 256], g64, HEAD_DIM) * gain(G_DIL_K, 256)
    vb = proj[:, P_BV:P_BV + 256]
    o_ref[:, A_QB:A_QB + 256] = _bf(qb)
    o_ref[:, A_KB:A_KB + 256] = _bf(kb)
    o_ref[:, A_VB:A_VB + 256] = _bf(vb)
    for c, val in enumerate((qb, kb, vb)):
        stage[2 * c] = val[:, 0:LANES]
        stage[2 * c + 1] = val[:, LANES:2 * LANES]
    n_chunks = DIL_COLS // LANES
    for d, ref in ((4, o4_ref), (16, o16_ref)):
        n = x.shape[0] // d
        for r in range(d):
            ref[0, r] = _bf(jnp.concatenate(
                [stage[c, pl.ds(r, n, stride=d), :] for c in range(n_chunks)], axis=1))

    qc = _group_rms(proj[:, P_CQ2:P_CQ2 + 256], g64, HEAD_DIM) * gain(G_GQA_Q, 256)
    for g in range(2):
        sl = slice(LANES * g, LANES * (g + 1))
        o_ref[:, A_QC + LANES * g:A_QC + LANES * (g + 1)] = _bf(_rope(qc[:, sl], a_c, a_sn, a_sp, HEAD_DIM // 4))
    kc = _group_rms(proj[:, P_CK2:P_CK2 + 128], g64[0:128, 0:128], HEAD_DIM) * gain(G_GQA_K, 128)
    o_ref[:, A_KC:A_KC + 128] = _bf(_rope(kc, a_c, a_sn, a_sp, HEAD_DIM // 4))
    o_ref[:, A_VC:A_VC + 128] = _bf(proj[:, P_CV2:P_CV2 + 128])

    dqn = _group_rms(proj[:, P_DQ:P_DQ + 256], g32, DIFF_QK_DIM)
    o_ref[:, A_QD0:A_QD0 + 256] = _bf(dqn * gain(G_DIFF_Q0, 256))
    o_ref[:, A_QD1:A_QD1 + 256] = _bf(dqn * gain(G_DIFF_Q1, 256))
    o_ref[:, A_KD:A_KD + 256] = _bf(_group_rms(proj[:, P_DK:P_DK + 256], g32, DIFF_QK_DIM) * gain(G_DIFF_K, 256))
    o_ref[:, A_VD:A_VD + 256] = _bf(proj[:, P_DV:P_DV + 256])


def _inproj(x2d, gv, w_in, wuq, wuk, wuv, rope, g32, g64, g128):
    t = x2d.shape[0]
    tm = TM_PROJ
    n_pos = SEQ // tm
    const = lambda i: (0, 0)
    return pl.pallas_call(
        _inproj_kernel,
        grid=(t // tm,),
        in_specs=[
            pl.BlockSpec((tm, D_MODEL), lambda i: (i, 0)),
            pl.BlockSpec((GAIN_ROWS, D_MODEL), const),
            pl.BlockSpec((D_MODEL, PROJ_COLS), const),
            pl.BlockSpec((MLA_Q_RANK, 512), const),
            pl.BlockSpec((MLA_KV_RANK, 512), const),
            pl.BlockSpec((MLA_KV_RANK, 256), const),
            pl.BlockSpec((tm, 768), lambda i: (i % n_pos, 0)),
            pl.BlockSpec((MXU_DIM, MXU_DIM), const),
            pl.BlockSpec((MXU_DIM, MXU_DIM), const),
            pl.BlockSpec((MXU_DIM, MXU_DIM), const),
        ],
        out_specs=[
            pl.BlockSpec((tm, ACT_COLS), lambda i: (i, 0)),
            pl.BlockSpec((1, 4, tm // 4, DIL_COLS), lambda i: (i // n_pos, 0, i % n_pos, 0)),
            pl.BlockSpec((1, 16, tm // 16, DIL_COLS), lambda i: (i // n_pos, 0, i % n_pos, 0)),
        ],
        out_shape=[
            jax.ShapeDtypeStruct((t, ACT_COLS), BF16),
            jax.ShapeDtypeStruct((t // SEQ, 4, SEQ // 4, DIL_COLS), BF16),
            jax.ShapeDtypeStruct((t // SEQ, 16, SEQ // 16, DIL_COLS), BF16),
        ],
        scratch_shapes=[pltpu.VMEM((DIL_COLS // LANES, tm, LANES), F32)],
        compiler_params=pltpu.CompilerParams(dimension_semantics=("parallel",), vmem_limit_bytes=VMEM_LIMIT_BYTES),
        name="inproj_prep",
    )(x2d, gv, w_in, wuq, wuk, wuv, rope, g32, g64, g128)


def _softmax_pv(s, v):
    m = jnp.max(s, axis=-1, keepdims=True)
    e = jnp.exp2(s - m)
    l = jnp.sum(e, axis=-1, keepdims=True)
    return _dot(_bf(e), v) / l


def _heads_one_ahead(scores_of, finish):
    pending = scores_of(0)
    for h in range(N_HEADS):
        s = pending
        if h + 1 < N_HEADS:
            pending = scores_of(h + 1)
        finish(h, s)


def _attn_mla_kernel(q_ref, k_ref, v_ref, o_ref):
    def scores_of(h):
        sl = slice(LANES * h, LANES * (h + 1))
        return _dot_nt(q_ref[:, sl], k_ref[:, sl])

    def finish(h, s):
        hs = slice(HEAD_DIM * h, HEAD_DIM * (h + 1))
        o_ref[:, hs] = _softmax_pv(s, v_ref[:, hs])
    _heads_one_ahead(scores_of, finish)


def _attn_gqa_kernel(q_ref, k_ref, v_ref, o_ref):
    group = lambda h: slice(HEAD_DIM * (h // 2), HEAD_DIM * (h // 2 + 1))

    def scores_of(h):
        return _dot_nt(q_ref[:, HEAD_DIM * h:HEAD_DIM * (h + 1)], k_ref[:, group(h)])

    def finish(h, s):
        o_ref[:, HEAD_DIM * h:HEAD_DIM * (h + 1)] = _softmax_pv(s, v_ref[:, group(h)])
    _heads_one_ahead(scores_of, finish)


def _attn_full(act, batch, kernel, q_col, q_w, k_col, k_w, v_col, v_w, name):
    t = act.shape[0]
    tq = TQ_FULL
    nq = SEQ // tq
    return pl.pallas_call(
        kernel,
        grid=(batch, nq),
        in_specs=[
            pl.BlockSpec((tq, q_w), lambda b, i: (b * nq + i, q_col // q_w)),
            pl.BlockSpec((SEQ, k_w), lambda b, i: (b, k_col // k_w)),
            pl.BlockSpec((SEQ, v_w), lambda b, i: (b, v_col // v_w)),
        ],
        out_specs=pl.BlockSpec((tq, 256), lambda b, i: (b * nq + i, 0)),
        out_shape=jax.ShapeDtypeStruct((t, 256), F32),
        compiler_params=pltpu.CompilerParams(
            dimension_semantics=("parallel", "parallel"), vmem_limit_bytes=VMEM_LIMIT_BYTES),
        name=name,
    )(act, act, act)


def _attn_diff_kernel(lam_ref, q0_ref, q1_ref, k_ref, v_ref, win_ref, sg_ref, o_ref, bias_ref):
    @pl.when(pl.program_id(1) == 0)
    def _():
        for h in range(N_HEADS):
            w = jnp.broadcast_to(win_ref[0, h:h + 1, :], (TQ, SEQ + TQ))
            bias_ref[h] = pltpu.roll(w, 0, axis=1, stride=1, stride_axis=0)[:, TQ:TQ + SEQ]

    lam = lam_ref[0]

    def scores_of(h):
        hs = slice(HEAD_DIM * h, HEAD_DIM * (h + 1))
        k = k_ref[:, hs]
        return _dot_nt(q0_ref[:, hs], k), _dot_nt(q1_ref[:, hs], k)

    def finish(h, scores):
        hs = slice(HEAD_DIM * h, HEAD_DIM * (h + 1))
        bias = bias_ref[h]
        s0, s1 = scores[0] + bias, scores[1] + bias
        e0 = jnp.exp2(s0 - jnp.max(s0, axis=-1, keepdims=True))
        e1 = jnp.exp2(s1 - jnp.max(s1, axis=-1, keepdims=True))
        r0 = 1.0 / jnp.sum(e0, axis=-1, keepdims=True)
        r1 = lam / jnp.sum(e1, axis=-1, keepdims=True)
        v = v_ref[:, hs]
        o = r0 * _dot(_bf(e0), v) - r1 * _dot(_bf(e1), v)
        o_ref[:, hs] = _rms(o, HEAD_DIM) * sg_ref[:, hs]
    _heads_one_ahead(scores_of, finish)


def _attn_diff(act, batch, lam, bias_win, sub_gain):
    t = act.shape[0]
    nq = SEQ // TQ
    return pl.pallas_call(
        _attn_diff_kernel,
        grid=(nq, batch),
        in_specs=[
            pl.BlockSpec(memory_space=pltpu.SMEM),
            pl.BlockSpec((TQ, 256), lambda i, b: (b * nq + i, A_QD0 // 256)),
            pl.BlockSpec((TQ, 256), lambda i, b: (b * nq + i, A_QD1 // 256)),
            pl.BlockSpec((SEQ, 256), lambda i, b: (b, A_KD // 256)),
            pl.BlockSpec((SEQ, 256), lambda i, b: (b, A_VD // 256)),
            pl.BlockSpec((1, N_HEADS, SEQ + TQ), lambda i, b: (i, 0, 0)),
            pl.BlockSpec((1, 256), lambda i, b: (0, 0)),
        ],
        out_specs=pl.BlockSpec((TQ, 256), lambda i, b: (b * nq + i, 0)),
        out_shape=jax.ShapeDtypeStruct((t, 256), F32),
        scratch_shapes=[pltpu.VMEM((N_HEADS, TQ, SEQ), F32)],
        compiler_params=pltpu.CompilerParams(
            dimension_semantics=("arbitrary", "arbitrary"), vmem_limit_bytes=VMEM_LIMIT_BYTES),
        name="attn_diff",
    )(lam, act, act, act, act, bias_win, sub_gain)


def _dil_branch(q_of, k_of, v_of, bias_of, n_seq):
    outs, lses = [], []
    for h in range(N_HEADS):
        hs = slice(HEAD_DIM * h, HEAD_DIM * (h + 1))
        q, k, v = q_of(hs), k_of(hs), v_of(hs)
        n = q.shape[1]
        if n == DIL_QBLK:
            kw, vw = k, v
            bias = bias_of(h)[:, DIL_HALF:DIL_HALF + DIL_QBLK]
            edge = None
        else:
            n_blk = n // DIL_QBLK

            def windows(x):
                zeros = jnp.zeros((n_seq, DIL_HALF, HEAD_DIM), BF16)
                xp = jnp.concatenate([zeros, x, zeros], axis=1)
                first = xp[:, 0:n].reshape(n_seq * n_blk, DIL_QBLK, HEAD_DIM)
                second = xp[:, DIL_QBLK:DIL_QBLK + n].reshape(n_seq * n_blk, DIL_QBLK, HEAD_DIM)
                return jnp.concatenate([first, second], axis=1)
            kw, vw = windows(k), windows(v)
            q = q.reshape(n_seq * n_blk, DIL_QBLK, HEAD_DIM)
            bias = bias_of(h)
            blk = lax.broadcasted_iota(jnp.int32, (n_seq * n_blk, 1, 2 * DIL_QBLK), 0) % n_blk
            col = lax.broadcasted_iota(jnp.int32, (n_seq * n_blk, 1, 2 * DIL_QBLK), 2)
            key_pos = col - DIL_HALF + blk * DIL_QBLK
            edge = jnp.where(jnp.logical_and(key_pos >= 0, key_pos < n), 0.0, NEG_INF)
        s = jnp.einsum("bqe,bke->bqk", q, kw, preferred_element_type=F32) + bias[None]
        if edge is not None:
            s = s + edge
        m = jnp.max(s, axis=-1, keepdims=True)
        e = jnp.exp2(s - m)
        l = jnp.sum(e, axis=-1, keepdims=True)
        o = jnp.einsum("bqk,bke->bqe", _bf(e), vw, preferred_element_type=F32) / l
        lse = jnp.broadcast_to(m + jnp.log2(l), o.shape)
        outs.append(o.reshape(n_seq, n, HEAD_DIM))
        lses.append(lse.reshape(n_seq, n, HEAD_DIM))
    cat = lambda xs: [jnp.concatenate(xs[2 * c:2 * c + 2], axis=2) for c in range(2)]
    return cat(outs), cat(lses)


def _dil_kernel(n1_ref, r4_ref, r16_ref, bias_ref, o_ref, o1, l1, o4, l4, out):
    col_of = lambda base: (lambda hs: slice(base + hs.start, base + hs.stop))
    q_cols, k_cols, v_cols = col_of(0), col_of(256), col_of(512)

    oc, lc = _dil_branch(lambda hs: n1_ref[:, q_cols(hs)][None], lambda hs: n1_ref[:, k_cols(hs)][None],
                         lambda hs: n1_ref[:, v_cols(hs)][None], lambda h: bias_ref[0, h], 1)
    for c in range(2):
        o1[c] = oc[c][0]
        l1[c] = lc[c][0]

    oc, lc = _dil_branch(lambda hs: r4_ref[0, :, :, q_cols(hs)], lambda hs: r4_ref[0, :, :, k_cols(hs)],
                         lambda hs: r4_ref[0, :, :, v_cols(hs)], lambda h: bias_ref[1, h], 4)
    for c in range(2):
        for r in range(4):
            tok = pl.ds(r, SEQ // 4, stride=4)
            o4[c, tok, :] = oc[c][r]
            l4[c, tok, :] = lc[c][r]

    oc, lc = _dil_branch(lambda hs: r16_ref[0, :, :, q_cols(hs)], lambda hs: r16_ref[0, :, :, k_cols(hs)],
                         lambda hs: r16_ref[0, :, :, v_cols(hs)], lambda h: bias_ref[2, h], 16)
    toks = [pl.ds(r, SEQ // 16, stride=16) for r in range(16)]
    for c in range(2):
        gather = lambda ref: jnp.stack([ref[c, tok, :] for tok in toks])
        la, lb = gather(l1), gather(l4)
        mx = jnp.maximum(jnp.maximum(la, lb), lc[c])
        wa, wb, wc = jnp.exp2(la - mx), jnp.exp2(lb - mx), jnp.exp2(lc[c] - mx)
        den = wa + wb + wc
        res = (wa / den) * gather(o1) + (wb / den) * gather(o4) + (wc / den) * oc[c]
        for r, tok in enumerate(toks):
            out[c, tok, :] = res[r]
    o_ref[:, 0:LANES] = out[0]
    o_ref[:, LANES:2 * LANES] = out[1]


def _attn_dilated(act, act4, act16, batch, bias_tabs):
    t = act.shape[0]
    nat = pltpu.VMEM((2, SEQ, LANES), F32)
    return pl.pallas_call(
        _dil_kernel,
        grid=(batch,),
        in_specs=[
            pl.BlockSpec((SEQ, DIL_COLS), lambda b: (b, 0)),
            pl.BlockSpec((1, 4, SEQ // 4, DIL_COLS), lambda b: (b, 0, 0, 0)),
            pl.BlockSpec((1, 16, SEQ // 16, DIL_COLS), lambda b: (b, 0, 0, 0)),
            pl.BlockSpec((3, N_HEADS, DIL_QBLK, 2 * DIL_QBLK), lambda b: (0, 0, 0, 0)),
        ],
        out_specs=pl.BlockSpec((SEQ, 256), lambda b: (b, 0)),
        out_shape=jax.ShapeDtypeStruct((t, 256), F32),
        scratch_shapes=[nat, nat, nat, nat, nat],
        compiler_params=pltpu.CompilerParams(dimension_semantics=("parallel",), vmem_limit_bytes=VMEM_LIMIT_BYTES),
        name="attn_dilated",
    )(act, act4, act16, bias_tabs)


def _route(lt):
    g = [lt[i:i + 1, :] for i in range(4)]
    gmax = jnp.maximum(jnp.maximum(g[0], g[1]), jnp.maximum(g[2], g[3]))
    gsum = sum(jnp.exp(gi - gmax) for gi in g)
    g_w = 1.0 / gsum
    gidx = jnp.where(g[0] == gmax, 0, jnp.where(g[1] == gmax, 1, jnp.where(g[2] == gmax, 2, 3)))
    el = []
    for j in range(EXPERTS_PER_GROUP):
        acc = jnp.zeros_like(g[0])
        for i in range(4):
            r = 4 + EXPERTS_PER_GROUP * i + j
            acc = jnp.where(gidx == i, lt[r:r + 1, :], acc)
        el.append(acc)
    emax = jnp.maximum(jnp.maximum(el[0], el[1]), jnp.maximum(el[2], el[3]))
    ee = [jnp.exp(e - emax) for e in el]
    esum = ee[0] + ee[1] + ee[2] + ee[3]
    p = [e / esum for e in ee]
    p1 = jnp.maximum(jnp.maximum(p[0], p[1]), jnp.maximum(p[2], p[3]))
    i1 = jnp.where(p[0] == p1, 0, jnp.where(p[1] == p1, 1, jnp.where(p[2] == p1, 2, 3)))
    pm = [jnp.where(i1 == j, -1.0, p[j]) for j in range(4)]
    p2 = jnp.maximum(jnp.maximum(pm[0], pm[1]), jnp.maximum(pm[2], pm[3]))
    i2 = jnp.where(pm[0] == p2, 0, jnp.where(pm[1] == p2, 1, jnp.where(pm[2] == p2, 2, 3)))
    den = p1 + p2
    return gidx, i1, i2, g_w * (p1 / den), g_w * (p2 / den)


def _pair_offset(lo):
    return jnp.where(lo == 0, 0, jnp.where(lo == 1, 3, 5))


def _outproj_kernel(ya_ref, yb_ref, yc_ref, yd_ref, x_ref, cv_ref, w_ref, wrh_ref, wrl_ref, rb_ref, tri_ref,
                    x1_ref, hp_ref, ri_ref, cnt_out_ref, cnt_ref):
    tm = x_ref.shape[0]

    @pl.when(pl.program_id(0) == 0)
    def _():
        cnt_ref[...] = jnp.zeros_like(cnt_ref)

    mixed = jnp.concatenate(
        [_rms(ya_ref[...], 256), _rms(yb_ref[...], 256), _rms(yc_ref[...], 256), yd_ref[...]], axis=1)
    mixed = mixed * cv_ref[0:1, :]
    x1 = x_ref[...] + _dot(_bf(mixed), w_ref[...])
    x1_ref[...] = x1
    h2 = _rms(x1, D_MODEL) * cv_ref[1:2, :]
    h_hi = _bf(h2)
    h_rt = h_hi.astype(F32)
    h_lo = _bf(h2 - h_rt)
    logits = _dot(h_hi, wrh_ref[...]) + _dot(h_hi, wrl_ref[...]) + _dot(h_lo, wrh_ref[...]) + rb_ref[...]
    gidx, i1, i2, wa, wb = _route(logits.T)

    lo, hi = jnp.minimum(i1, i2), jnp.maximum(i1, i2)
    first_is_lo = i1 < i2
    w_lo, w_hi = jnp.where(first_is_lo, wa, wb), jnp.where(first_is_lo, wb, wa)
    cls = gidx * PAIRS_PER_GROUP + _pair_offset(lo) + (hi - lo - 1)

    rows = lax.broadcasted_iota(jnp.int32, (CLS_ROWS, tm), 0)
    onehot = rows == cls
    oh = jnp.where(onehot, 1.0, 0.0)
    prefix = _dot(_bf(oh), tri_ref[...])
    before = cnt_ref[:, 0:1] + prefix
    rank = jnp.sum(jnp.where(onehot, before, 0.0), axis=0, keepdims=True)
    cnt_ref[...] = cnt_ref[...] + jnp.sum(oh, axis=1, keepdims=True)
    cnt_out_ref[...] = cnt_ref[...]
    ri_ref[...] = jnp.concatenate([cls, rank.astype(jnp.int32), jnp.zeros((6, tm), jnp.int32)], axis=0)

    rows_w = lax.broadcasted_iota(jnp.int32, (LANES, tm), 0)
    w_t = jnp.where(rows_w == 0, w_lo, jnp.where(rows_w == 1, w_hi, 0.0))
    chunks = [h_rt[:, LANES * c:LANES * (c + 1)] for c in range(ROW_TILE_SUBLANES)] + [w_t.T]
    hp_ref[...] = pltpu.einshape("cml->mcl", jnp.stack(chunks))


def _outproj(ya, yb, yc, yd, x2d, cv, w_out, wr_hi, wr_lo, rb, tri):
    t = x2d.shape[0]
    tm = TM_OUT
    row = lambda w: pl.BlockSpec((tm, w), lambda i: (i, 0))
    const = lambda shape: pl.BlockSpec(shape, lambda i: (0, 0))
    return pl.pallas_call(
        _outproj_kernel,
        grid=(t // tm,),
        in_specs=[row(256), row(256), row(256), row(256), row(D_MODEL), const((8, D_MODEL)),
                  const((D_MODEL, D_MODEL)), const((D_MODEL, LANES)), const((D_MODEL, LANES)), const((1, LANES)),
                  const((tm, tm))],
        out_specs=[row(D_MODEL), pl.BlockSpec((tm, XS_SUBLANES, LANES), lambda i: (i, 0, 0)),
                   pl.BlockSpec((8, tm), lambda i: (0, i)), const((CLS_ROWS, LANES))],
        out_shape=[jax.ShapeDtypeStruct((t, D_MODEL), F32), jax.ShapeDtypeStruct((t, XS_SUBLANES, LANES), F32),
                   jax.ShapeDtypeStruct((8, t), jnp.int32), jax.ShapeDtypeStruct((CLS_ROWS, LANES), F32)],
        scratch_shapes=[pltpu.VMEM((CLS_ROWS, LANES), F32)],
        compiler_params=pltpu.CompilerParams(dimension_semantics=("arbitrary",), vmem_limit_bytes=VMEM_LIMIT_BYTES),
        name="outproj_router",
    )(ya, yb, yc, yd, x2d, cv, w_out, wr_hi, wr_lo, rb, tri)


def _moe_plan(ri, counts, n_tiles):
    cls, rank = ri[0], ri[1]
    cnt = counts[:N_CLASSES, 0].astype(jnp.int32)
    padded = ((cnt + TM_MOE - 1) // TM_MOE) * TM_MOE
    ends = jnp.cumsum(padded)
    offs = ends - padded
    classes = jnp.arange(N_CLASSES, dtype=jnp.int32)
    dest = jnp.sum(jnp.where(cls[:, None] == classes[None, :], offs[None, :], 0), axis=1) + rank
    tile_start = jnp.arange(n_tiles, dtype=jnp.int32) * TM_MOE
    tile_cls = jnp.sum((tile_start[:, None] >= ends[None, :]).astype(jnp.int32), axis=1)
    tile_cls = jnp.minimum(tile_cls, N_CLASSES - 1)
    group, pair = tile_cls // PAIRS_PER_GROUP, tile_cls % PAIRS_PER_GROUP
    lo = (pair >= 3).astype(jnp.int32) + (pair >= 5).astype(jnp.int32)
    hi = pair - _pair_offset(lo) + lo + 1
    n_active = (ends[-1] // TM_MOE).reshape(1)
    return dest, group * EXPERTS_PER_GROUP + lo, group * EXPERTS_PER_GROUP + hi, n_active


def _rows_to_tiles(x):
    chunks = jnp.stack([x[:, LANES * c:LANES * (c + 1)] for c in range(ROW_TILE_SUBLANES)])
    return pltpu.einshape("cml->mcl", chunks)


def _dispatch_kernel(dest_ref, h_ref, xs_in_ref, xs_ref, sem):
    del xs_in_ref
    tm = h_ref.shape[0]

    for r in range(tm):
        pltpu.make_async_copy(h_ref.at[r], xs_ref.at[dest_ref[0, 0, r]], sem).start()

    pltpu.make_async_copy(h_ref, xs_ref.at[pl.ds(0, tm)], sem).wait()


def _dispatch(dest, hp, n_rows):
    t = hp.shape[0]
    tm = TM_MOE
    return pl.pallas_call(
        _dispatch_kernel,
        grid=(t // tm,),
        in_specs=[
            pl.BlockSpec((1, 1, tm), lambda i: (i, 0, 0), memory_space=pltpu.SMEM),
            pl.BlockSpec((tm, XS_SUBLANES, LANES), lambda i: (i, 0, 0)),
            pl.BlockSpec(memory_space=pl.ANY),
        ],
        out_specs=pl.BlockSpec(memory_space=pl.ANY),
        out_shape=jax.ShapeDtypeStruct((n_rows, XS_SUBLANES, LANES), F32),
        scratch_shapes=[pltpu.SemaphoreType.DMA],
        input_output_aliases={2: 0},
        compiler_params=pltpu.CompilerParams(dimension_semantics=("arbitrary",)),
        name="moe_dispatch",
    )(dest.reshape(t // tm, 1, tm), hp, jnp.zeros((n_rows, XS_SUBLANES, LANES), F32))


def _moe_group_kernel(elo_ref, ehi_ref, nact_ref, xs_ref, wg0, wu0, wd0, wg1, wu1, wd1, o_ref):
    del elo_ref, ehi_ref
    active = pl.program_id(0) < nact_ref[0]

    @pl.when(active)
    def _():
        chunks = pltpu.einshape("mcl->cml", xs_ref[...])
        x = _bf(jnp.concatenate([chunks[c] for c in range(ROW_TILE_SUBLANES)], axis=1))
        gate = chunks[ROW_TILE_SUBLANES]

        def expert(wg, wu, wd):
            a = _dot(x, wg[0])
            hid = a * (1.0 / (1.0 + jnp.exp(-a))) * _dot(x, wu[0])
            return _dot(_bf(hid), wd[0])
        o = gate[:, 0:1] * expert(wg0, wu0, wd0) + gate[:, 1:2] * expert(wg1, wu1, wd1)
        o_ref[...] = _rows_to_tiles(o)

    @pl.when(jnp.logical_not(active))
    def _():
        o_ref[...] = jnp.zeros_like(o_ref)


def _moe_group(e_lo, e_hi, n_active, xs, wg, wu, wd):
    n_rows = xs.shape[0]
    tm = TM_MOE
    w_in = lambda which: pl.BlockSpec((1, D_MODEL, D_FF), lambda i, lo, hi, na: ((lo, hi)[which][i], 0, 0))
    w_dn = lambda which: pl.BlockSpec((1, D_FF, D_MODEL), lambda i, lo, hi, na: ((lo, hi)[which][i], 0, 0))
    return pl.pallas_call(
        _moe_group_kernel,
        grid_spec=pltpu.PrefetchScalarGridSpec(
            num_scalar_prefetch=3,
            grid=(n_rows // tm,),
            in_specs=[pl.BlockSpec((tm, XS_SUBLANES, LANES), lambda i, lo, hi, na: (i, 0, 0)),
                      w_in(0), w_in(0), w_dn(0), w_in(1), w_in(1), w_dn(1)],
            out_specs=pl.BlockSpec((tm, ROW_TILE_SUBLANES, LANES), lambda i, lo, hi, na: (i, 0, 0)),
        ),
        out_shape=jax.ShapeDtypeStruct((n_rows, ROW_TILE_SUBLANES, LANES), F32),
        compiler_params=pltpu.CompilerParams(dimension_semantics=("arbitrary",), vmem_limit_bytes=VMEM_LIMIT_BYTES),
        name="moe_group",
    )(e_lo, e_hi, n_active, xs, wg, wu, wd, wg, wu, wd)


def _combine_kernel(dest_ref, x1_ref, y_ref, o_ref, buf, sem):
    tm = x1_ref.shape[0]

    for r in range(tm):
        pltpu.make_async_copy(y_ref.at[dest_ref[0, 0, r]], buf.at[r], sem).start()

    pltpu.make_async_copy(y_ref.at[pl.ds(0, tm)], buf, sem).wait()
    chunks = pltpu.einshape("mcl->cml", buf[...])
    for c in range(ROW_TILE_SUBLANES):
        cols = slice(LANES * c, LANES * (c + 1))
        o_ref[:, cols] = x1_ref[:, cols] + chunks[c]


def _combine(dest, x1, y):
    t = x1.shape[0]
    tm = TM_MOE
    return pl.pallas_call(
        _combine_kernel,
        grid=(t // tm,),
        in_specs=[
            pl.BlockSpec((1, 1, tm), lambda i: (i, 0, 0), memory_space=pltpu.SMEM),
            pl.BlockSpec((tm, D_MODEL), lambda i: (i, 0)),
            pl.BlockSpec(memory_space=pl.ANY),
        ],
        out_specs=pl.BlockSpec((tm, D_MODEL), lambda i: (i, 0)),
        out_shape=jax.ShapeDtypeStruct((t, D_MODEL), F32),
        scratch_shapes=[pltpu.VMEM((tm, ROW_TILE_SUBLANES, LANES), F32), pltpu.SemaphoreType.DMA],
        compiler_params=pltpu.CompilerParams(dimension_semantics=("arbitrary",)),
        name="moe_combine",
    )(dest.reshape(t // tm, 1, tm), x1, y)


def _rope_angles(pos, dim):
    inv = 1.0 / (ROPE_THETA ** (jnp.arange(0, dim, 2, dtype=F32) / dim))
    return pos.astype(F32)[:, None] * inv[None, :]


def _rope_tables():
    pos = jnp.arange(SEQ, dtype=jnp.int32)
    z = lambda w: jnp.zeros((SEQ, w), F32)
    ang = _rope_angles(pos, MLA_ROPE_DIM)
    c, s = jnp.cos(ang), jnp.sin(ang)
    m_c = jnp.concatenate([jnp.ones((SEQ, 64), F32), c, c, z(32)], axis=1)
    m_sn = jnp.concatenate([z(64), -s, z(48)], axis=1)
    m_sp = jnp.concatenate([z(80), s, z(32)], axis=1)
    row_pos = pos // GRID_W
    col_pos = pos - row_pos * GRID_W
    ra, ca = _rope_angles(row_pos, HEAD_DIM // 2), _rope_angles(col_pos, HEAD_DIM // 2)
    rc, rs, cc, cs = jnp.cos(ra), jnp.sin(ra), jnp.cos(ca), jnp.sin(ca)
    a_c = jnp.concatenate([rc, rc, cc, cc] * 2, axis=1)
    a_sn = jnp.concatenate([-rs, z(16), -cs, z(16)] * 2, axis=1)
    a_sp = jnp.concatenate([z(16), rs, z(16), cs] * 2, axis=1)
    return jnp.concatenate([m_c, m_sn, m_sp, a_c, a_sn, a_sp], axis=1)


def _rel_bucket(rel):
    half = NUM_BUCKETS // 2
    max_exact = half // 2
    n = jnp.abs(rel)
    nf = jnp.maximum(n, 1).astype(F32)
    log_ratio = jnp.log(nf / max_exact) / math.log(REL_MAX_DISTANCE / max_exact)
    large = jnp.minimum(max_exact + (log_ratio * (half - max_exact)).astype(jnp.int32), half - 1)
    return jnp.where(rel > 0, half, 0) + jnp.where(n < max_exact, n, large)


def _bias_line(table, rel):
    return table[_rel_bucket(rel)].T


def _toeplitz(line, rows, cols, first):
    n = line.shape[-1]
    padded = jnp.pad(line, [(0, 0)] * (line.ndim - 1) + [(0, 1)])
    flat = jnp.tile(padded, rows)[..., :rows * n]
    skew = flat.reshape(line.shape[:-1] + (rows, n))
    return skew[..., first:first + cols]


def _dil_bias(table):
    reach = DIL_QBLK + DIL_HALF - 1
    steps = jnp.arange(-reach, reach + 1, dtype=jnp.int32)
    band = jnp.abs(steps) <= DIL_HALF
    lines = jnp.stack([jnp.where(band[None], _bias_line(table, steps * d), NEG_INF) for _, d in DIL_PATTERNS])
    return _toeplitz(lines, DIL_QBLK, 2 * DIL_QBLK, DIL_QBLK - 1)


def _diff_bias_windows(table):
    rel = jnp.arange(2 * SEQ, dtype=jnp.int32) - SEQ
    line = _bias_line(table, rel)
    starts = [SEQ - (i + 1) * TQ for i in range(SEQ // TQ)]
    return jnp.stack([line[:, s0:s0 + SEQ + TQ] for s0 in starts])


def _block_ones(group):
    idx = np.arange(MXU_DIM) // group
    return jnp.asarray(idx[:, None] == idx[None, :], dtype=BF16)


def _pad_row(v):
    return jnp.pad(v.astype(F32), (0, D_MODEL - v.shape[0]))


def _layer_params(layer, norm1_g, w_in, mla_q_norm_g, mla_kv_norm_g, mla_w_uq, mla_w_ukv, mla_qk_g, dil_qk_g,
                  gqa_qk_g, diff_qk_g, diff_lambda, diff_subln_g, mix_beta, w_out, norm2_g, router_group_w,
                  router_group_b, router_expert_w, router_expert_b):
    w = w_in[layer]
    w_p = _bf(jnp.concatenate([w[:, :416], jnp.zeros((D_MODEL, 96), F32), w[:, 416:]], axis=1))
    uq = mla_w_uq[layer].reshape(MLA_Q_RANK, N_HEADS, MLA_QK_DIM)
    wuq = _bf(jnp.pad(uq, ((0, 0), (0, 0), (0, LANES - MLA_QK_DIM))).reshape(MLA_Q_RANK, 512))
    ukv = mla_w_ukv[layer].reshape(MLA_KV_RANK, N_HEADS, 2 * MLA_NOPE_DIM)
    wuk = _bf(jnp.pad(ukv[:, :, :MLA_NOPE_DIM], ((0, 0), (0, 0), (0, LANES - MLA_NOPE_DIM))).reshape(MLA_KV_RANK, 512))
    wuv = _bf(ukv[:, :, MLA_NOPE_DIM:].reshape(MLA_KV_RANK, 256))

    pad96 = lambda g: jnp.tile(jnp.pad(g, (0, LANES - MLA_QK_DIM)), N_HEADS)
    m0 = jnp.tile(jnp.concatenate([jnp.ones(32, F32), jnp.zeros(32, F32)]), N_HEADS)
    dq = jnp.tile(diff_qk_g[layer, 0], 2 * N_HEADS) * (DIFF_QK_DIM ** -0.5 * LOG2E)
    rows = [
        norm1_g[layer], mla_q_norm_g[layer], mla_kv_norm_g[layer],
        pad96(mla_qk_g[layer, 0]) * (MLA_QK_DIM ** -0.5 * LOG2E), pad96(mla_qk_g[layer, 1]),
        jnp.tile(dil_qk_g[layer, 0], N_HEADS) * (HEAD_DIM ** -0.5 * LOG2E), jnp.tile(dil_qk_g[layer, 1], N_HEADS),
        jnp.tile(gqa_qk_g[layer, 0], N_HEADS) * (HEAD_DIM ** -0.5 * LOG2E), jnp.tile(gqa_qk_g[layer, 1], 2),
        dq * m0, dq * (1.0 - m0), jnp.tile(diff_qk_g[layer, 1], 2 * N_HEADS),
    ]
    gv = jnp.stack([_pad_row(r) for r in rows] + [jnp.zeros(D_MODEL, F32)] * (GAIN_ROWS - len(rows)))

    lambda_init = 0.8 - 0.6 * math.exp(-0.3 * layer)
    lv = diff_lambda[layer].astype(F32)
    lam = (jnp.exp(jnp.sum(lv[0] * lv[1])) - jnp.exp(jnp.sum(lv[2] * lv[3])) + lambda_init).reshape(1)
    sub_gain = (jnp.tile(diff_subln_g[layer], N_HEADS) * (1.0 - lambda_init)).reshape(1, 256)

    cv = jnp.stack([mix_beta[layer], norm2_g[layer]] + [jnp.zeros(D_MODEL, F32)] * 6)
    wr = jnp.concatenate([router_group_w[layer], router_expert_w[layer],
                          jnp.zeros((D_MODEL, LANES - 4 - N_EXPERTS), F32)], axis=1)
    wr_hi = _bf(wr)
    wr_lo = _bf(wr - wr_hi.astype(F32))
    rb = jnp.concatenate([router_group_b[layer], router_expert_b[layer],
                          jnp.zeros(LANES - 4 - N_EXPERTS, F32)]).reshape(1, LANES)
    return dict(w_in=w_p, wuq=wuq, wuk=wuk, wuv=wuv, gv=gv, lam=lam, sub_gain=sub_gain, cv=cv,
                w_out=_bf(w_out[layer]), wr_hi=wr_hi, wr_lo=wr_lo, rb=rb)


def kernel(x, rel_bias, norm1_g, w_in, mla_q_norm_g, mla_kv_norm_g, mla_w_uq, mla_w_ukv, mla_qk_g, dil_qk_g, gqa_qk_g, diff_qk_g, diff_lambda, diff_subln_g, mix_beta, w_out, norm2_g, router_group_w, router_group_b, router_expert_w, router_expert_b, expert_w_gate, expert_w_up, expert_w_down):
    batch, seq, d_model = x.shape
    assert seq == SEQ and d_model == D_MODEL
    depth = w_in.shape[0]
    t = batch * seq

    rope = _rope_tables()
    g32, g64, g128 = _block_ones(32), _block_ones(64), _block_ones(128)
    tri = jnp.asarray(np.arange(TM_OUT)[:, None] < np.arange(TM_OUT)[None, :], dtype=BF16)
    dil_bias = _dil_bias(rel_bias[:, :N_HEADS] * LOG2E)
    diff_bias = _diff_bias_windows(rel_bias[:, N_HEADS:] * LOG2E)

    x2d = x.reshape(t, D_MODEL)
    for layer in range(depth):
        p = _layer_params(layer, norm1_g, w_in, mla_q_norm_g, mla_kv_norm_g, mla_w_uq, mla_w_ukv, mla_qk_g,
                          dil_qk_g, gqa_qk_g, diff_qk_g, diff_lambda, diff_subln_g, mix_beta, w_out, norm2_g,
                          router_group_w, router_group_b, router_expert_w, router_expert_b)
        act, act4, act16 = _inproj(x2d, p["gv"], p["w_in"], p["wuq"], p["wuk"], p["wuv"], rope, g32, g64, g128)
        ya = _attn_full(act, batch, _attn_mla_kernel, A_QA, 512, A_KA, 512, A_VA, 256, "attn_mla")
        yb = _attn_dilated(act, act4, act16, batch, dil_bias)
        yc = _attn_full(act, batch, _attn_gqa_kernel, A_QC, 256, A_KC, 128, A_VC, 128, "attn_gqa")
        yd = _attn_diff(act, batch, p["lam"], diff_bias, p["sub_gain"])
        x1, hp, ri, counts = _outproj(ya, yb, yc, yd, x2d, p["cv"], p["w_out"], p["wr_hi"], p["wr_lo"], p["rb"], tri)
        n_tiles = t // TM_MOE + N_CLASSES
        dest, e_lo, e_hi, n_active = _moe_plan(ri, counts, n_tiles)
        xs = _dispatch(dest, hp, n_tiles * TM_MOE)
        y = _moe_group(e_lo, e_hi, n_active, xs, _bf(expert_w_gate[layer]), _bf(expert_w_up[layer]),
                       _bf(expert_w_down[layer]))
        x2d = _combine(dest, x1, y)
    return x2d.reshape(batch, seq, D_MODEL)
```

```python
import functools
import math

import jax
import jax.numpy as jnp
import numpy as np
from jax import lax
from jax.experimental import pallas as pl
from jax.experimental.pallas import tpu as pltpu

F32 = jnp.float32
BF16 = jnp.bfloat16

D_MODEL = 1024
SEQ = 2048
HEAD_DIM = 64
GRID_W = 64
ROPE_THETA = 10000.0
LOG2E = 1.0 / math.log(2.0)
NORM_EPS = 1e-6
NEG_INF = -1e30
NUM_BUCKETS = 32
REL_MAX_DISTANCE = 1024

N_HEADS = 4
MLA_NOPE_DIM = 64
MLA_ROPE_DIM = 32
MLA_QK_DIM = MLA_NOPE_DIM + MLA_ROPE_DIM
MLA_Q_RANK = 256
MLA_KV_RANK = 128
DIL_PATTERNS = ((128, 1), (512, 4), (2048, 16))
DIL_HALF = 64
DIFF_QK_DIM = 32
N_EXPERTS = 16
EXPERTS_PER_GROUP = 4
D_FF = 512

LANES = 128
MXU_DIM = 256
VMEM_LIMIT_BYTES = 56 * 1024 * 1024

P_CQ, P_CKV, P_KR = 0, 256, 384
P_BQ, P_BK, P_BV = 512, 768, 1024
P_CQ2, P_CK2, P_CV2 = 1280, 1536, 1664
P_DQ, P_DK, P_DV = 1792, 2048, 2304
PROJ_COLS = 2560

A_QB, A_KB, A_VB = 0, 256, 512
A_VA, A_QA, A_KA = 768, 1024, 1536
A_QC, A_KC, A_VC = 2048, 2304, 2432
A_QD0, A_QD1, A_KD, A_VD = 2560, 2816, 3072, 3328
ACT_COLS = 3584
DIL_COLS = 768

(G_NORM1, G_MLA_QN, G_MLA_KVN, G_MLA_Q, G_MLA_K, G_DIL_Q, G_DIL_K, G_GQA_Q, G_GQA_K,
 G_DIFF_Q0, G_DIFF_Q1, G_DIFF_K) = range(12)
GAIN_ROWS = 16

TM_PROJ = 512
TQ = 256
TQ_FULL = 512
TM_OUT = 512
TM_MOE = 256
DIL_QBLK = 128

PAIRS_PER_GROUP = 6
N_CLASSES = 4 * PAIRS_PER_GROUP
CLS_ROWS = 32
XS_COLS = D_MODEL + LANES


def _bf(x):
    return x.astype(BF16)


def _dot(a, b):
    return jnp.dot(a, b, preferred_element_type=F32)


def _dot_nt(a, b):
    return lax.dot_general(a, b, (((1,), (1,)), ((), ())), preferred_element_type=F32)


def _rms(x, width):
    return x * lax.rsqrt(jnp.sum(x * x, axis=-1, keepdims=True) * (1.0 / width) + NORM_EPS)


def _group_sumsq(x, g):
    x2 = x * x
    hi = _bf(x2)
    lo = _bf(x2 - hi.astype(F32))
    w = g.shape[0]
    outs = []
    for c in range(x.shape[1] // w):
        sl = slice(w * c, w * (c + 1))
        outs.append(_dot(hi[:, sl], g) + _dot(lo[:, sl], g))
    return outs[0] if len(outs) == 1 else jnp.concatenate(outs, axis=1)


def _group_rms(x, g, group):
    return x * lax.rsqrt(_group_sumsq(x, g) * (1.0 / group) + NORM_EPS)


def _rope(x, c, s_next, s_prev, half):
    n = x.shape[1]
    return x * c + pltpu.roll(x, n - half, axis=1) * s_next + pltpu.roll(x, half, axis=1) * s_prev


def _inproj_kernel(x_ref, gv_ref, w_ref, wuq_ref, wuk_ref, wuv_ref, rope_ref, g32_ref, g64_ref, g128_ref,
                   o_ref, o4_ref, o16_ref, stage):
    def gain(row, width):
        return gv_ref[row:row + 1, 0:width]

    x = x_ref[...]
    h = _rms(x, D_MODEL) * gain(G_NORM1, D_MODEL)
    proj = _dot(_bf(h), w_ref[...])

    g32 = g32_ref[...]
    g64 = g64_ref[...]
    g128 = g128_ref[...]
    m_c = rope_ref[:, 0:128]
    m_sn = rope_ref[:, 128:256]
    m_sp = rope_ref[:, 256:384]
    a_c = rope_ref[:, 384:512]
    a_sn = rope_ref[:, 512:640]
    a_sp = rope_ref[:, 640:768]

    cq = _rms(proj[:, P_CQ:P_CQ + MLA_Q_RANK], MLA_Q_RANK) * gain(G_MLA_QN, MLA_Q_RANK)
    ckv = _rms(proj[:, P_CKV:P_CKV + MLA_KV_RANK], MLA_KV_RANK) * gain(G_MLA_KVN, MLA_KV_RANK)
    q = _dot(_bf(cq), wuq_ref[...])
    k_nope = _dot(_bf(ckv), wuk_ref[...])
    v = _dot(_bf(ckv), wuv_ref[...])
    k_rope = pltpu.roll(proj[:, P_KR:P_KR + LANES], MLA_NOPE_DIM, axis=1)
    k = k_nope + jnp.concatenate([k_rope] * N_HEADS, axis=1)
    qn = _group_rms(q, g128, MLA_QK_DIM) * gain(G_MLA_Q, 512)
    kn = _group_rms(k, g128, MLA_QK_DIM) * gain(G_MLA_K, 512)
    for g in range(N_HEADS):
        sl = slice(LANES * g, LANES * (g + 1))
        o_ref[:, A_QA + LANES * g:A_QA + LANES * (g + 1)] = _bf(_rope(qn[:, sl], m_c, m_sn, m_sp, MLA_ROPE_DIM // 2))
        o_ref[:, A_KA + LANES * g:A_KA + LANES * (g + 1)] = _bf(_rope(kn[:, sl], m_c, m_sn, m_sp, MLA_ROPE_DIM // 2))
    o_ref[:, A_VA:A_VA + 256] = _bf(v)

    qb = _group_rms(proj[:, P_BQ:P_BQ + 256], g64, HEAD_DIM) * gain(G_DIL_Q, 256)
    kb = _group_rms(proj[:, P_BK:P_BK + 256], g64, HEAD_DIM) * gain(G_DIL_K, 256)
    vb = proj[:, P_BV:P_BV + 256]
    o_ref[:, A_QB:A_QB + 256] = _bf(qb)
    o_ref[:, A_KB:A_KB + 256] = _bf(kb)
    o_ref[:, A_VB:A_VB + 256] = _bf(vb)
    for c, val in enumerate((qb, kb, vb)):
        stage[2 * c] = val[:, 0:LANES]
        stage[2 * c + 1] = val[:, LANES:2 * LANES]
    n_chunks = DIL_COLS // LANES
    for d, ref in ((4, o4_ref), (16, o16_ref)):
        n = x.shape[0] // d
        for r in range(d):
            ref[0, r] = _bf(jnp.concatenate(
                [stage[c, pl.ds(r, n, stride=d), :] for c in range(n_chunks)], axis=1))

    qc = _group_rms(proj[:, P_CQ2:P_CQ2 + 256], g64, HEAD_DIM) * gain(G_GQA_Q, 256)
    for g in range(2):
        sl = slice(LANES * g, LANES * (g + 1))
        o_ref[:, A_QC + LANES * g:A_QC + LANES * (g + 1)] = _bf(_rope(qc[:, sl], a_c, a_sn, a_sp, HEAD_DIM // 4))
    kc = _group_rms(proj[:, P_CK2:P_CK2 + 128], g64[0:128, 0:128], HEAD_DIM) * gain(G_GQA_K, 128)
    o_ref[:, A_KC:A_KC + 128] = _bf(_rope(kc, a_c, a_sn, a_sp, HEAD_DIM // 4))
    o_ref[:, A_VC:A_VC + 128] = _bf(proj[:, P_CV2:P_CV2 + 128])

    dqn = _group_rms(proj[:, P_DQ:P_DQ + 256], g32, DIFF_QK_DIM)
    o_ref[:, A_QD0:A_QD0 + 256] = _bf(dqn * gain(G_DIFF_Q0, 256))
    o_ref[:, A_QD1:A_QD1 + 256] = _bf(dqn * gain(G_DIFF_Q1, 256))
    o_ref[:, A_KD:A_KD + 256] = _bf(_group_rms(proj[:, P_DK:P_DK + 256], g32, DIFF_QK_DIM) * gain(G_DIFF_K, 256))
    o_ref[:, A_VD:A_VD + 256] = _bf(proj[:, P_DV:P_DV + 256])


def _inproj(x2d, gv, w_in, wuq, wuk, wuv, rope, g32, g64, g128):
    t = x2d.shape[0]
    tm = TM_PROJ
    n_pos = SEQ // tm
    const = lambda i: (0, 0)
    return pl.pallas_call(
        _inproj_kernel,
        grid=(t // tm,),
        in_specs=[
            pl.BlockSpec((tm, D_MODEL), lambda i: (i, 0)),
            pl.BlockSpec((GAIN_ROWS, D_MODEL), const),
            pl.BlockSpec((D_MODEL, PROJ_COLS), const),
            pl.BlockSpec((MLA_Q_RANK, 512), const),
            pl.BlockSpec((MLA_KV_RANK, 512), const),
            pl.BlockSpec((MLA_KV_RANK, 256), const),
            pl.BlockSpec((tm, 768), lambda i: (i % n_pos, 0)),
            pl.BlockSpec((MXU_DIM, MXU_DIM), const),
            pl.BlockSpec((MXU_DIM, MXU_DIM), const),
            pl.BlockSpec((MXU_DIM, MXU_DIM), const),
        ],
        out_specs=[
            pl.BlockSpec((tm, ACT_COLS), lambda i: (i, 0)),
            pl.BlockSpec((1, 4, tm // 4, DIL_COLS), lambda i: (i // n_pos, 0, i % n_pos, 0)),
            pl.BlockSpec((1, 16, tm // 16, DIL_COLS), lambda i: (i // n_pos, 0, i % n_pos, 0)),
        ],
        out_shape=[
            jax.ShapeDtypeStruct((t, ACT_COLS), BF16),
            jax.ShapeDtypeStruct((t // SEQ, 4, SEQ // 4, DIL_COLS), BF16),
            jax.ShapeDtypeStruct((t // SEQ, 16, SEQ // 16, DIL_COLS), BF16),
        ],
        scratch_shapes=[pltpu.VMEM((DIL_COLS // LANES, tm, LANES), F32)],
        compiler_params=pltpu.CompilerParams(dimension_semantics=("parallel",), vmem_limit_bytes=VMEM_LIMIT_BYTES),
        name="inproj_prep",
    )(x2d, gv, w_in, wuq, wuk, wuv, rope, g32, g64, g128)


def _softmax_pv(s, v):
    m = jnp.max(s, axis=-1, keepdims=True)
    e = jnp.exp2(s - m)
    l = jnp.sum(e, axis=-1, keepdims=True)
    return _dot(_bf(e), v) / l


def _heads_one_ahead(scores_of, finish):
    pending = scores_of(0)
    for h in range(N_HEADS):
        s = pending
        if h + 1 < N_HEADS:
            pending = scores_of(h + 1)
        finish(h, s)


def _attn_mla_kernel(q_ref, k_ref, v_ref, o_ref):
    def scores_of(h):
        sl = slice(LANES * h, LANES * (h + 1))
        return _dot_nt(q_ref[:, sl], k_ref[:, sl])

    def finish(h, s):
        hs = slice(HEAD_DIM * h, HEAD_DIM * (h + 1))
        o_ref[:, hs] = _softmax_pv(s, v_ref[:, hs])
    _heads_one_ahead(scores_of, finish)


def _attn_gqa_kernel(q_ref, k_ref, v_ref, o_ref):
    group = lambda h: slice(HEAD_DIM * (h // 2), HEAD_DIM * (h // 2 + 1))

    def scores_of(h):
        return _dot_nt(q_ref[:, HEAD_DIM * h:HEAD_DIM * (h + 1)], k_ref[:, group(h)])

    def finish(h, s):
        o_ref[:, HEAD_DIM * h:HEAD_DIM * (h + 1)] = _softmax_pv(s, v_ref[:, group(h)])
    _heads_one_ahead(scores_of, finish)


def _attn_full(act, batch, kernel, q_col, q_w, k_col, k_w, v_col, v_w, name):
    t = act.shape[0]
    tq = TQ_FULL
    nq = SEQ // tq
    return pl.pallas_call(
        kernel,
        grid=(batch, nq),
        in_specs=[
            pl.BlockSpec((tq, q_w), lambda b, i: (b * nq + i, q_col // q_w)),
            pl.BlockSpec((SEQ, k_w), lambda b, i: (b, k_col // k_w)),
            pl.BlockSpec((SEQ, v_w), lambda b, i: (b, v_col // v_w)),
        ],
        out_specs=pl.BlockSpec((tq, 256), lambda b, i: (b * nq + i, 0)),
        out_shape=jax.ShapeDtypeStruct((t, 256), F32),
        compiler_params=pltpu.CompilerParams(
            dimension_semantics=("parallel", "parallel"), vmem_limit_bytes=VMEM_LIMIT_BYTES),
        name=name,
    )(act, act, act)


def _attn_diff_kernel(lam_ref, q0_ref, q1_ref, k_ref, v_ref, win_ref, sg_ref, o_ref, bias_ref):
    @pl.when(pl.program_id(1) == 0)
    def _():
        for h in range(N_HEADS):
            w = jnp.broadcast_to(win_ref[0, h:h + 1, :], (TQ, SEQ + TQ))
            bias_ref[h] = pltpu.roll(w, 0, axis=1, stride=1, stride_axis=0)[:, TQ:TQ + SEQ]

    lam = lam_ref[0]

    def scores_of(h):
        hs = slice(HEAD_DIM * h, HEAD_DIM * (h + 1))
        k = k_ref[:, hs]
        return _dot_nt(q0_ref[:, hs], k), _dot_nt(q1_ref[:, hs], k)

    def finish(h, scores):
        hs = slice(HEAD_DIM * h, HEAD_DIM * (h + 1))
        bias = bias_ref[h]
        s0, s1 = scores[0] + bias, scores[1] + bias
        e0 = jnp.exp2(s0 - jnp.max(s0, axis=-1, keepdims=True))
        e1 = jnp.exp2(s1 - jnp.max(s1, axis=-1, keepdims=True))
        r0 = 1.0 / jnp.sum(e0, axis=-1, keepdims=True)
        r1 = lam / jnp.sum(e1, axis=-1, keepdims=True)
        v = v_ref[:, hs]
        o = r0 * _dot(_bf(e0), v) - r1 * _dot(_bf(e1), v)
        o_ref[:, hs] = _rms(o, HEAD_DIM) * sg_ref[:, hs]
    _heads_one_ahead(scores_of, finish)


def _attn_diff(act, batch, lam, bias_win, sub_gain):
    t = act.shape[0]
    nq = SEQ // TQ
    return pl.pallas_call(
        _attn_diff_kernel,
        grid=(nq, batch),
        in_specs=[
            pl.BlockSpec(memory_space=pltpu.SMEM),
            pl.BlockSpec((TQ, 256), lambda i, b: (b * nq + i, A_QD0 // 256)),
            pl.BlockSpec((TQ, 256), lambda i, b: (b * nq + i, A_QD1 // 256)),
            pl.BlockSpec((SEQ, 256), lambda i, b: (b, A_KD // 256)),
            pl.BlockSpec((SEQ, 256), lambda i, b: (b, A_VD // 256)),
            pl.BlockSpec((1, N_HEADS, SEQ + TQ), lambda i, b: (i, 0, 0)),
            pl.BlockSpec((1, 256), lambda i, b: (0, 0)),
        ],
        out_specs=pl.BlockSpec((TQ, 256), lambda i, b: (b * nq + i, 0)),
        out_shape=jax.ShapeDtypeStruct((t, 256), F32),
        scratch_shapes=[pltpu.VMEM((N_HEADS, TQ, SEQ), F32)],
        compiler_params=pltpu.CompilerParams(
            dimension_semantics=("arbitrary", "arbitrary"), vmem_limit_bytes=VMEM_LIMIT_BYTES),
        name="attn_diff",
    )(lam, act, act, act, act, bias_win, sub_gain)


def _dil_branch(q_of, k_of, v_of, bias_of, n_seq):
    outs, lses = [], []
    for h in range(N_HEADS):
        hs = slice(HEAD_DIM * h, HEAD_DIM * (h + 1))
        q, k, v = q_of(hs), k_of(hs), v_of(hs)
        n = q.shape[1]
        if n == DIL_QBLK:
            kw, vw = k, v
            bias = bias_of(h)[:, DIL_HALF:DIL_HALF + DIL_QBLK]
            edge = None
        else:
            n_blk = n // DIL_QBLK

            def windows(x):
                zeros = jnp.zeros((n_seq, DIL_HALF, HEAD_DIM), BF16)
                xp = jnp.concatenate([zeros, x, zeros], axis=1)
                first = xp[:, 0:n].reshape(n_seq * n_blk, DIL_QBLK, HEAD_DIM)
                second = xp[:, DIL_QBLK:DIL_QBLK + n].reshape(n_seq * n_blk, DIL_QBLK, HEAD_DIM)
                return jnp.concatenate([first, second], axis=1)
            kw, vw = windows(k), windows(v)
            q = q.reshape(n_seq * n_blk, DIL_QBLK, HEAD_DIM)
            bias = bias_of(h)
            blk = lax.broadcasted_iota(jnp.int32, (n_seq * n_blk, 1, 2 * DIL_QBLK), 0) % n_blk
            col = lax.broadcasted_iota(jnp.int32, (n_seq * n_blk, 1, 2 * DIL_QBLK), 2)
            key_pos = col - DIL_HALF + blk * DIL_QBLK
            edge = jnp.where(jnp.logical_and(key_pos >= 0, key_pos < n), 0.0, NEG_INF)
        s = jnp.einsum("bqe,bke->bqk", q, kw, preferred_element_type=F32) + bias[None]
        if edge is not None:
            s = s + edge
        m = jnp.max(s, axis=-1, keepdims=True)
        e = jnp.exp2(s - m)
        l = jnp.sum(e, axis=-1, keepdims=True)
        o = jnp.einsum("bqk,bke->bqe", _bf(e), vw, preferred_element_type=F32) / l
        lse = jnp.broadcast_to(m + jnp.log2(l), o.shape)
        outs.append(o.reshape(n_seq, n, HEAD_DIM))
        lses.append(lse.reshape(n_seq, n, HEAD_DIM))
    cat = lambda xs: [jnp.concatenate(xs[2 * c:2 * c + 2], axis=2) for c in range(2)]
    return cat(outs), cat(lses)


def _dil_kernel(n1_ref, r4_ref, r16_ref, bias_ref, o_ref, o1, l1, o4, l4, out):
    col_of = lambda base: (lambda hs: slice(base + hs.start, base + hs.stop))
    q_cols, k_cols, v_cols = col_of(0), col_of(256), col_of(512)

    oc, lc = _dil_branch(lambda hs: n1_ref[:, q_cols(hs)][None], lambda hs: n1_ref[:, k_cols(hs)][None],
                         lambda hs: n1_ref[:, v_cols(hs)][None], lambda h: bias_ref[0, h], 1)
    for c in range(2):
        o1[c] = oc[c][0]
        l1[c] = lc[c][0]

    oc, lc = _dil_branch(lambda hs: r4_ref[0, :, :, q_cols(hs)], lambda hs: r4_ref[0, :, :, k_cols(hs)],
                         lambda hs: r4_ref[0, :, :, v_cols(hs)], lambda h: bias_ref[1, h], 4)
    for c in range(2):
        for r in range(4):
            tok = pl.ds(r, SEQ // 4, stride=4)
            o4[c, tok, :] = oc[c][r]
            l4[c, tok, :] = lc[c][r]

    oc, lc = _dil_branch(lambda hs: r16_ref[0, :, :, q_cols(hs)], lambda hs: r16_ref[0, :, :, k_cols(hs)],
                         lambda hs: r16_ref[0, :, :, v_cols(hs)], lambda h: bias_ref[2, h], 16)
    toks = [pl.ds(r, SEQ // 16, stride=16) for r in range(16)]
    for c in range(2):
        gather = lambda ref: jnp.stack([ref[c, tok, :] for tok in toks])
        la, lb = gather(l1), gather(l4)
        mx = jnp.maximum(jnp.maximum(la, lb), lc[c])
        wa, wb, wc = jnp.exp2(la - mx), jnp.exp2(lb - mx), jnp.exp2(lc[c] - mx)
        den = wa + wb + wc
        res = (wa / den) * gather(o1) + (wb / den) * gather(o4) + (wc / den) * oc[c]
        for r, tok in enumerate(toks):
            out[c, tok, :] = res[r]
    o_ref[:, 0:LANES] = out[0]
    o_ref[:, LANES:2 * LANES] = out[1]


def _attn_dilated(act, act4, act16, batch, bias_tabs):
    t = act.shape[0]
    nat = pltpu.VMEM((2, SEQ, LANES), F32)
    return pl.pallas_call(
        _dil_kernel,
        grid=(batch,),
        in_specs=[
            pl.BlockSpec((SEQ, DIL_COLS), lambda b: (b, 0)),
            pl.BlockSpec((1, 4, SEQ // 4, DIL_COLS), lambda b: (b, 0, 0, 0)),
            pl.BlockSpec((1, 16, SEQ // 16, DIL_COLS), lambda b: (b, 0, 0, 0)),
            pl.BlockSpec((3, N_HEADS, DIL_QBLK, 2 * DIL_QBLK), lambda b: (0, 0, 0, 0)),
        ],
        out_specs=pl.BlockSpec((SEQ, 256), lambda b: (b, 0)),
        out_shape=jax.ShapeDtypeStruct((t, 256), F32),
        scratch_shapes=[nat, nat, nat, nat, nat],
        compiler_params=pltpu.CompilerParams(dimension_semantics=("parallel",), vmem_limit_bytes=VMEM_LIMIT_BYTES),
        name="attn_dilated",
    )(act, act4, act16, bias_tabs)


def _route(lt):
    g = [lt[i:i + 1, :] for i in range(4)]
    gmax = jnp.maximum(jnp.maximum(g[0], g[1]), jnp.maximum(g[2], g[3]))
    gsum = sum(jnp.exp(gi - gmax) for gi in g)
    g_w = 1.0 / gsum
    gidx = jnp.where(g[0] == gmax, 0, jnp.where(g[1] == gmax, 1, jnp.where(g[2] == gmax, 2, 3)))
    el = []
    for j in range(EXPERTS_PER_GROUP):
        acc = jnp.zeros_like(g[0])
        for i in range(4):
            r = 4 + EXPERTS_PER_GROUP * i + j
            acc = jnp.where(gidx == i, lt[r:r + 1, :], acc)
        el.append(acc)
    emax = jnp.maximum(jnp.maximum(el[0], el[1]), jnp.maximum(el[2], el[3]))
    ee = [jnp.exp(e - emax) for e in el]
    esum = ee[0] + ee[1] + ee[2] + ee[3]
    p = [e / esum for e in ee]
    p1 = jnp.maximum(jnp.maximum(p[0], p[1]), jnp.maximum(p[2], p[3]))
    i1 = jnp.where(p[0] == p1, 0, jnp.where(p[1] == p1, 1, jnp.where(p[2] == p1, 2, 3)))
    pm = [jnp.where(i1 == j, -1.0, p[j]) for j in range(4)]
    p2 = jnp.maximum(jnp.maximum(pm[0], pm[1]), jnp.maximum(pm[2], pm[3]))
    i2 = jnp.where(pm[0] == p2, 0, jnp.where(pm[1] == p2, 1, jnp.where(pm[2] == p2, 2, 3)))
    den = p1 + p2
    return gidx, i1, i2, g_w * (p1 / den), g_w * (p2 / den)


def _pair_offset(lo):
    return jnp.where(lo == 0, 0, jnp.where(lo == 1, 3, 5))


def _outproj_kernel(ya_ref, yb_ref, yc_ref, yd_ref, x_ref, cv_ref, w_ref, wrh_ref, wrl_ref, rb_ref, tri_ref,
                    x1_ref, hp_ref, ri_ref, cnt_out_ref, cnt_ref):
    tm = x_ref.shape[0]

    @pl.when(pl.program_id(0) == 0)
    def _():
        cnt_ref[...] = jnp.zeros_like(cnt_ref)

    mixed = jnp.concatenate(
        [_rms(ya_ref[...], 256), _rms(yb_ref[...], 256), _rms(yc_ref[...], 256), yd_ref[...]], axis=1)
    mixed = mixed * cv_ref[0:1, :]
    x1 = x_ref[...] + _dot(_bf(mixed), w_ref[...])
    x1_ref[...] = x1
    h2 = _rms(x1, D_MODEL) * cv_ref[1:2, :]
    h_hi = _bf(h2)
    h_rt = h_hi.astype(F32)
    h_lo = _bf(h2 - h_rt)
    logits = _dot(h_hi, wrh_ref[...]) + _dot(h_hi, wrl_ref[...]) + _dot(h_lo, wrh_ref[...]) + rb_ref[...]
    gidx, i1, i2, wa, wb = _route(logits.T)

    lo, hi = jnp.minimum(i1, i2), jnp.maximum(i1, i2)
    first_is_lo = i1 < i2
    w_lo, w_hi = jnp.where(first_is_lo, wa, wb), jnp.where(first_is_lo, wb, wa)
    cls = gidx * PAIRS_PER_GROUP + _pair_offset(lo) + (hi - lo - 1)

    rows = lax.broadcasted_iota(jnp.int32, (CLS_ROWS, tm), 0)
    onehot = rows == cls
    oh = jnp.where(onehot, 1.0, 0.0)
    prefix = _dot(_bf(oh), tri_ref[...])
    before = cnt_ref[:, 0:1] + prefix
    rank = jnp.sum(jnp.where(onehot, before, 0.0), axis=0, keepdims=True)
    cnt_ref[...] = cnt_ref[...] + jnp.sum(oh, axis=1, keepdims=True)
    cnt_out_ref[...] = cnt_ref[...]
    ri_ref[...] = jnp.concatenate([cls, rank.astype(jnp.int32), jnp.zeros((6, tm), jnp.int32)], axis=0)

    hp_ref[:, 0:D_MODEL] = h_rt
    rows_w = lax.broadcasted_iota(jnp.int32, (LANES, tm), 0)
    w_t = jnp.where(rows_w == 0, w_lo, jnp.where(rows_w == 1, w_hi, 0.0))
    hp_ref[:, D_MODEL:XS_COLS] = w_t.T


def _outproj(ya, yb, yc, yd, x2d, cv, w_out, wr_hi, wr_lo, rb, tri):
    t = x2d.shape[0]
    tm = TM_OUT
    row = lambda w: pl.BlockSpec((tm, w), lambda i: (i, 0))
    const = lambda shape: pl.BlockSpec(shape, lambda i: (0, 0))
    return pl.pallas_call(
        _outproj_kernel,
        grid=(t // tm,),
        in_specs=[row(256), row(256), row(256), row(256), row(D_MODEL), const((8, D_MODEL)),
                  const((D_MODEL, D_MODEL)), const((D_MODEL, LANES)), const((D_MODEL, LANES)), const((1, LANES)),
                  const((tm, tm))],
        out_specs=[row(D_MODEL), row(XS_COLS), pl.BlockSpec((8, tm), lambda i: (0, i)), const((CLS_ROWS, LANES))],
        out_shape=[jax.ShapeDtypeStruct((t, D_MODEL), F32), jax.ShapeDtypeStruct((t, XS_COLS), F32),
                   jax.ShapeDtypeStruct((8, t), jnp.int32), jax.ShapeDtypeStruct((CLS_ROWS, LANES), F32)],
        scratch_shapes=[pltpu.VMEM((CLS_ROWS, LANES), F32)],
        compiler_params=pltpu.CompilerParams(dimension_semantics=("arbitrary",), vmem_limit_bytes=VMEM_LIMIT_BYTES),
        name="outproj_router",
    )(ya, yb, yc, yd, x2d, cv, w_out, wr_hi, wr_lo, rb, tri)


def _moe_plan(ri, counts, n_tiles):
    cls, rank = ri[0], ri[1]
    cnt = counts[:N_CLASSES, 0].astype(jnp.int32)
    padded = ((cnt + TM_MOE - 1) // TM_MOE) * TM_MOE
    ends = jnp.cumsum(padded)
    offs = ends - padded
    classes = jnp.arange(N_CLASSES, dtype=jnp.int32)
    dest = jnp.sum(jnp.where(cls[:, None] == classes[None, :], offs[None, :], 0), axis=1) + rank
    tile_start = jnp.arange(n_tiles, dtype=jnp.int32) * TM_MOE
    tile_cls = jnp.sum((tile_start[:, None] >= ends[None, :]).astype(jnp.int32), axis=1)
    tile_cls = jnp.minimum(tile_cls, N_CLASSES - 1)
    group, pair = tile_cls // PAIRS_PER_GROUP, tile_cls % PAIRS_PER_GROUP
    lo = (pair >= 3).astype(jnp.int32) + (pair >= 5).astype(jnp.int32)
    hi = pair - _pair_offset(lo) + lo + 1
    n_active = (ends[-1] // TM_MOE).reshape(1)
    return dest, group * EXPERTS_PER_GROUP + lo, group * EXPERTS_PER_GROUP + hi, n_active


def _dispatch_kernel(dest_ref, h_ref, xs_in_ref, xs_ref, stage, sems):
    del xs_in_ref
    tm = h_ref.shape[0]
    i = pl.program_id(0)
    slot = i % 2

    def wait_slot(s):
        pltpu.make_async_copy(stage.at[s], xs_ref.at[pl.ds(0, tm), :], sems.at[s]).wait()

    @pl.when(i >= 2)
    def _():
        wait_slot(slot)

    stage[slot] = h_ref[...]
    for r in range(tm):
        pltpu.make_async_copy(stage.at[slot, pl.ds(r, 1), :], xs_ref.at[pl.ds(dest_ref[0, 0, r], 1), :],
                              sems.at[slot]).start()

    @pl.when(i == pl.num_programs(0) - 1)
    def _():
        wait_slot(1 - slot)
        wait_slot(slot)


def _dispatch(dest, hp, n_rows):
    t = hp.shape[0]
    tm = TM_MOE
    assert t // tm >= 2
    return pl.pallas_call(
        _dispatch_kernel,
        grid=(t // tm,),
        in_specs=[
            pl.BlockSpec((1, 1, tm), lambda i: (i, 0, 0), memory_space=pltpu.SMEM),
            pl.BlockSpec((tm, XS_COLS), lambda i: (i, 0)),
            pl.BlockSpec(memory_space=pl.ANY),
        ],
        out_specs=pl.BlockSpec(memory_space=pl.ANY),
        out_shape=jax.ShapeDtypeStruct((n_rows, XS_COLS), F32),
        scratch_shapes=[pltpu.VMEM((2, tm, XS_COLS), F32), pltpu.SemaphoreType.DMA((2,))],
        input_output_aliases={2: 0},
        compiler_params=pltpu.CompilerParams(dimension_semantics=("arbitrary",)),
        name="moe_dispatch",
    )(dest.reshape(t // tm, 1, tm), hp, jnp.zeros((n_rows, XS_COLS), F32))


def _moe_group_kernel(elo_ref, ehi_ref, nact_ref, xs_ref, wg0, wu0, wd0, wg1, wu1, wd1, o_ref):
    del elo_ref, ehi_ref
    active = pl.program_id(0) < nact_ref[0]

    @pl.when(active)
    def _():
        x = _bf(xs_ref[:, 0:D_MODEL])
        gate = xs_ref[:, D_MODEL:XS_COLS]

        def expert(wg, wu, wd):
            a = _dot(x, wg[0])
            hid = a * (1.0 / (1.0 + jnp.exp(-a))) * _dot(x, wu[0])
            return _dot(_bf(hid), wd[0])
        o_ref[...] = gate[:, 0:1] * expert(wg0, wu0, wd0) + gate[:, 1:2] * expert(wg1, wu1, wd1)

    @pl.when(jnp.logical_not(active))
    def _():
        o_ref[...] = jnp.zeros_like(o_ref)


def _moe_group(e_lo, e_hi, n_active, xs, wg, wu, wd):
    n_rows = xs.shape[0]
    tm = TM_MOE
    w_in = lambda which: pl.BlockSpec((1, D_MODEL, D_FF), lambda i, lo, hi, na: ((lo, hi)[which][i], 0, 0))
    w_dn = lambda which: pl.BlockSpec((1, D_FF, D_MODEL), lambda i, lo, hi, na: ((lo, hi)[which][i], 0, 0))
    return pl.pallas_call(
        _moe_group_kernel,
        grid_spec=pltpu.PrefetchScalarGridSpec(
            num_scalar_prefetch=3,
            grid=(n_rows // tm,),
            in_specs=[pl.BlockSpec((tm, XS_COLS), lambda i, lo, hi, na: (i, 0)),
                      w_in(0), w_in(0), w_dn(0), w_in(1), w_in(1), w_dn(1)],
            out_specs=pl.BlockSpec((tm, D_MODEL), lambda i, lo, hi, na: (i, 0)),
        ),
        out_shape=jax.ShapeDtypeStruct((n_rows, D_MODEL), F32),
        compiler_params=pltpu.CompilerParams(dimension_semantics=("arbitrary",), vmem_limit_bytes=VMEM_LIMIT_BYTES),
        name="moe_group",
    )(e_lo, e_hi, n_active, xs, wg, wu, wd, wg, wu, wd)


def _combine_kernel(dest_ref, dest_next_ref, x1_ref, y_ref, o_ref, buf, sems):
    tm = x1_ref.shape[0]
    i = pl.program_id(0)
    slot = i % 2

    def gather(idx_ref, s):
        for r in range(tm):
            pltpu.make_async_copy(y_ref.at[pl.ds(idx_ref[0, 0, r], 1), :], buf.at[s, pl.ds(r, 1), :],
                                  sems.at[s]).start()

    @pl.when(i == 0)
    def _():
        gather(dest_ref, 0)

    @pl.when(i + 1 < pl.num_programs(0))
    def _():
        gather(dest_next_ref, 1 - slot)

    pltpu.make_async_copy(y_ref.at[pl.ds(0, tm), :], buf.at[slot], sems.at[slot]).wait()
    o_ref[...] = x1_ref[...] + buf[slot]


def _combine(dest, x1, y):
    t = x1.shape[0]
    tm = TM_MOE
    n = t // tm
    dest3 = dest.reshape(n, 1, tm)
    return pl.pallas_call(
        _combine_kernel,
        grid=(n,),
        in_specs=[
            pl.BlockSpec((1, 1, tm), lambda i: (i, 0, 0), memory_space=pltpu.SMEM),
            pl.BlockSpec((1, 1, tm), lambda i: (jnp.minimum(i + 1, n - 1), 0, 0), memory_space=pltpu.SMEM),
            pl.BlockSpec((tm, D_MODEL), lambda i: (i, 0)),
            pl.BlockSpec(memory_space=pl.ANY),
        ],
        out_specs=pl.BlockSpec((tm, D_MODEL), lambda i: (i, 0)),
        out_shape=jax.ShapeDtypeStruct((t, D_MODEL), F32),
        scratch_shapes=[pltpu.VMEM((2, tm, D_MODEL), F32), pltpu.SemaphoreType.DMA((2,))],
        compiler_params=pltpu.CompilerParams(dimension_semantics=("arbitrary",)),
        name="moe_combine",
    )(dest3, dest3, x1, y)


def _rope_angles(pos, dim):
    inv = 1.0 / (ROPE_THETA ** (jnp.arange(0, dim, 2, dtype=F32) / dim))
    return pos.astype(F32)[:, None] * inv[None, :]


def _rope_tables():
    pos = jnp.arange(SEQ, dtype=jnp.int32)
    z = lambda w: jnp.zeros((SEQ, w), F32)
    ang = _rope_angles(pos, MLA_ROPE_DIM)
    c, s = jnp.cos(ang), jnp.sin(ang)
    m_c = jnp.concatenate([jnp.ones((SEQ, 64), F32), c, c, z(32)], axis=1)
    m_sn = jnp.concatenate([z(64), -s, z(48)], axis=1)
    m_sp = jnp.concatenate([z(80), s, z(32)], axis=1)
    row_pos = pos // GRID_W
    col_pos = pos - row_pos * GRID_W
    ra, ca = _rope_angles(row_pos, HEAD_DIM // 2), _rope_angles(col_pos, HEAD_DIM // 2)
    rc, rs, cc, cs = jnp.cos(ra), jnp.sin(ra), jnp.cos(ca), jnp.sin(ca)
    a_c = jnp.concatenate([rc, rc, cc, cc] * 2, axis=1)
    a_sn = jnp.concatenate([-rs, z(16), -cs, z(16)] * 2, axis=1)
    a_sp = jnp.concatenate([z(16), rs, z(16), cs] * 2, axis=1)
    return jnp.concatenate([m_c, m_sn, m_sp, a_c, a_sn, a_sp], axis=1)


def _rel_bucket(rel):
    half = NUM_BUCKETS // 2
    max_exact = half // 2
    n = jnp.abs(rel)
    nf = jnp.maximum(n, 1).astype(F32)
    log_ratio = jnp.log(nf / max_exact) / math.log(REL_MAX_DISTANCE / max_exact)
    large = jnp.minimum(max_exact + (log_ratio * (half - max_exact)).astype(jnp.int32), half - 1)
    return jnp.where(rel > 0, half, 0) + jnp.where(n < max_exact, n, large)


def _bias_line(table, rel):
    return table[_rel_bucket(rel)].T


def _toeplitz(line, rows, cols, first):
    n = line.shape[-1]
    padded = jnp.pad(line, [(0, 0)] * (line.ndim - 1) + [(0, 1)])
    flat = jnp.tile(padded, rows)[..., :rows * n]
    skew = flat.reshape(line.shape[:-1] + (rows, n))
    return skew[..., first:first + cols]


def _dil_bias(table):
    reach = DIL_QBLK + DIL_HALF - 1
    steps = jnp.arange(-reach, reach + 1, dtype=jnp.int32)
    band = jnp.abs(steps) <= DIL_HALF
    lines = jnp.stack([jnp.where(band[None], _bias_line(table, steps * d), NEG_INF) for _, d in DIL_PATTERNS])
    return _toeplitz(lines, DIL_QBLK, 2 * DIL_QBLK, DIL_QBLK - 1)


def _diff_bias_windows(table):
    rel = jnp.arange(2 * SEQ, dtype=jnp.int32) - SEQ
    line = _bias_line(table, rel)
    starts = [SEQ - (i + 1) * TQ for i in range(SEQ // TQ)]
    return jnp.stack([line[:, s0:s0 + SEQ + TQ] for s0 in starts])


def _block_ones(group):
    idx = np.arange(MXU_DIM) // group
    return jnp.asarray(idx[:, None] == idx[None, :], dtype=BF16)


def _pad_row(v):
    return jnp.pad(v.astype(F32), (0, D_MODEL - v.shape[0]))


def _layer_params(layer, norm1_g, w_in, mla_q_norm_g, mla_kv_norm_g, mla_w_uq, mla_w_ukv, mla_qk_g, dil_qk_g,
                  gqa_qk_g, diff_qk_g, diff_lambda, diff_subln_g, mix_beta, w_out, norm2_g, router_group_w,
                  router_group_b, router_expert_w, router_expert_b):
    w = w_in[layer]
    w_p = _bf(jnp.concatenate([w[:, :416], jnp.zeros((D_MODEL, 96), F32), w[:, 416:]], axis=1))
    uq = mla_w_uq[layer].reshape(MLA_Q_RANK, N_HEADS, MLA_QK_DIM)
    wuq = _bf(jnp.pad(uq, ((0, 0), (0, 0), (0, LANES - MLA_QK_DIM))).reshape(MLA_Q_RANK, 512))
    ukv = mla_w_ukv[layer].reshape(MLA_KV_RANK, N_HEADS, 2 * MLA_NOPE_DIM)
    wuk = _bf(jnp.pad(ukv[:, :, :MLA_NOPE_DIM], ((0, 0), (0, 0), (0, LANES - MLA_NOPE_DIM))).reshape(MLA_KV_RANK, 512))
    wuv = _bf(ukv[:, :, MLA_NOPE_DIM:].reshape(MLA_KV_RANK, 256))

    pad96 = lambda g: jnp.tile(jnp.pad(g, (0, LANES - MLA_QK_DIM)), N_HEADS)
    m0 = jnp.tile(jnp.concatenate([jnp.ones(32, F32), jnp.zeros(32, F32)]), N_HEADS)
    dq = jnp.tile(diff_qk_g[layer, 0], 2 * N_HEADS) * (DIFF_QK_DIM ** -0.5 * LOG2E)
    rows = [
        norm1_g[layer], mla_q_norm_g[layer], mla_kv_norm_g[layer],
        pad96(mla_qk_g[layer, 0]) * (MLA_QK_DIM ** -0.5 * LOG2E), pad96(mla_qk_g[layer, 1]),
        jnp.tile(dil_qk_g[layer, 0], N_HEADS) * (HEAD_DIM ** -0.5 * LOG2E), jnp.tile(dil_qk_g[layer, 1], N_HEADS),
        jnp.tile(gqa_qk_g[layer, 0], N_HEADS) * (HEAD_DIM ** -0.5 * LOG2E), jnp.tile(gqa_qk_g[layer, 1], 2),
        dq * m0, dq * (1.0 - m0), jnp.tile(diff_qk_g[layer, 1], 2 * N_HEADS),
    ]
    gv = jnp.stack([_pad_row(r) for r in rows] + [jnp.zeros(D_MODEL, F32)] * (GAIN_ROWS - len(rows)))

    lambda_init = 0.8 - 0.6 * math.exp(-0.3 * layer)
    lv = diff_lambda[layer].astype(F32)
    lam = (jnp.exp(jnp.sum(lv[0] * lv[1])) - jnp.exp(jnp.sum(lv[2] * lv[3])) + lambda_init).reshape(1)
    sub_gain = (jnp.tile(diff_subln_g[layer], N_HEADS) * (1.0 - lambda_init)).reshape(1, 256)

    cv = jnp.stack([mix_beta[layer], norm2_g[layer]] + [jnp.zeros(D_MODEL, F32)] * 6)
    wr = jnp.concatenate([router_group_w[layer], router_expert_w[layer],
                          jnp.zeros((D_MODEL, LANES - 4 - N_EXPERTS), F32)], axis=1)
    wr_hi = _bf(wr)
    wr_lo = _bf(wr - wr_hi.astype(F32))
    rb = jnp.concatenate([router_group_b[layer], router_expert_b[layer],
                          jnp.zeros(LANES - 4 - N_EXPERTS, F32)]).reshape(1, LANES)
    return dict(w_in=w_p, wuq=wuq, wuk=wuk, wuv=wuv, gv=gv, lam=lam, sub_gain=sub_gain, cv=cv,
                w_out=_bf(w_out[layer]), wr_hi=wr_hi, wr_lo=wr_lo, rb=rb)


def kernel(x, rel_bias, norm1_g, w_in, mla_q_norm_g, mla_kv_norm_g, mla_w_uq, mla_w_ukv, mla_qk_g, dil_qk_g, gqa_qk_g, diff_qk_g, diff_lambda, diff_subln_g, mix_beta, w_out, norm2_g, router_group_w, router_group_b, router_expert_w, router_expert_b, expert_w_gate, expert_w_up, expert_w_down):
    batch, seq, d_model = x.shape
    assert seq == SEQ and d_model == D_MODEL
    depth = w_in.shape[0]
    t = batch * seq

    rope = _rope_tables()
    g32, g64, g128 = _block_ones(32), _block_ones(64), _block_ones(128)
    tri = jnp.asarray(np.arange(TM_OUT)[:, None] < np.arange(TM_OUT)[None, :], dtype=BF16)
    dil_bias = _dil_bias(rel_bias[:, :N_HEADS] * LOG2E)
    diff_bias = _diff_bias_windows(rel_bias[:, N_HEADS:] * LOG2E)

    x2d = x.reshape(t, D_MODEL)
    for layer in range(depth):
        p = _layer_params(layer, norm1_g, w_in, mla_q_norm_g, mla_kv_norm_g, mla_w_uq, mla_w_ukv, mla_qk_g,
                          dil_qk_g, gqa_qk_g, diff_qk_g, diff_lambda, diff_subln_g, mix_beta, w_out, norm2_g,
                          router_group_w, router_group_b, router_expert_w, router_expert_b)
        act, act4, act16 = _inproj(x2d, p["gv"], p["w_in"], p["wuq"], p["wuk"], p["wuv"], rope, g32, g64, g128)
        ya = _attn_full(act, batch, _attn_mla_kernel, A_QA, 512, A_KA, 512, A_VA, 256, "attn_mla")
        yb = _attn_dilated(act, act4, act16, batch, dil_bias)
        yc = _attn_full(act, batch, _attn_gqa_kernel, A_QC, 256, A_KC, 128, A_VC, 128, "attn_gqa")
        yd = _attn_diff(act, batch, p["lam"], diff_bias, p["sub_gain"])
        x1, hp, ri, counts = _outproj(ya, yb, yc, yd, x2d, p["cv"], p["w_out"], p["wr_hi"], p["wr_lo"], p["rb"], tri)
        n_tiles = t // TM_MOE + N_CLASSES
        dest, e_lo, e_hi, n_active = _moe_plan(ri, counts, n_tiles)
        xs = _dispatch(dest, hp, n_tiles * TM_MOE)
        y = _moe_group(e_lo, e_hi, n_active, xs, _bf(expert_w_gate[layer]), _bf(expert_w_up[layer]),
                       _bf(expert_w_down[layer]))
        x2d = _combine(dest, x1, y)
    return x2d.reshape(batch, seq, D_MODEL)
```

```python
import functools
import math

import jax
import jax.numpy as jnp
import numpy as np
from jax import lax
from jax.experimental import pallas as pl
from jax.experimental.pallas import tpu as pltpu

F32 = jnp.float32
BF16 = jnp.bfloat16

D_MODEL = 1024
SEQ = 2048
HEAD_DIM = 64
GRID_W = 64
ROPE_THETA = 10000.0
LOG2E = 1.0 / math.log(2.0)
NORM_EPS = 1e-6
NEG_INF = -1e30
NUM_BUCKETS = 32
REL_MAX_DISTANCE = 1024

N_HEADS = 4
MLA_NOPE_DIM = 64
MLA_ROPE_DIM = 32
MLA_QK_DIM = MLA_NOPE_DIM + MLA_ROPE_DIM
MLA_Q_RANK = 256
MLA_KV_RANK = 128
DIL_PATTERNS = ((128, 1), (512, 4), (2048, 16))
DIL_HALF = 64
DIFF_QK_DIM = 32
N_EXPERTS = 16
EXPERTS_PER_GROUP = 4
D_FF = 512

LANES = 128
MXU_DIM = 256
VMEM_LIMIT_BYTES = 56 * 1024 * 1024

P_CQ, P_CKV, P_KR = 0, 256, 384
P_BQ, P_BK, P_BV = 512, 768, 1024
P_CQ2, P_CK2, P_CV2 = 1280, 1536, 1664
P_DQ, P_DK, P_DV = 1792, 2048, 2304
PROJ_COLS = 2560

A_QB, A_KB, A_VB = 0, 256, 512
A_VA, A_QA, A_KA = 768, 1024, 1536
A_QC, A_KC, A_VC = 2048, 2304, 2432
A_QD0, A_QD1, A_KD, A_VD = 2560, 2816, 3072, 3328
ACT_COLS = 3584
DIL_COLS = 768

(G_NORM1, G_MLA_QN, G_MLA_KVN, G_MLA_Q, G_MLA_K, G_DIL_Q, G_DIL_K, G_GQA_Q, G_GQA_K,
 G_DIFF_Q0, G_DIFF_Q1, G_DIFF_K) = range(12)
GAIN_ROWS = 16

TM_PROJ = 512
TQ = 256
TQ_FULL = 512
TM_OUT = 512
TM_MOE = 256
DIL_QBLK = 128

PAIRS_PER_GROUP = 6
N_CLASSES = 4 * PAIRS_PER_GROUP
CLS_ROWS = 32
XS_COLS = D_MODEL + LANES


def _bf(x):
    return x.astype(BF16)


def _dot(a, b):
    return jnp.dot(a, b, preferred_element_type=F32)


def _dot_nt(a, b):
    return lax.dot_general(a, b, (((1,), (1,)), ((), ())), preferred_element_type=F32)


def _rms(x, width):
    return x * lax.rsqrt(jnp.sum(x * x, axis=-1, keepdims=True) * (1.0 / width) + NORM_EPS)


def _group_sumsq(x, g):
    x2 = x * x
    hi = _bf(x2)
    lo = _bf(x2 - hi.astype(F32))
    w = g.shape[0]
    outs = []
    for c in range(x.shape[1] // w):
        sl = slice(w * c, w * (c + 1))
        outs.append(_dot(hi[:, sl], g) + _dot(lo[:, sl], g))
    return outs[0] if len(outs) == 1 else jnp.concatenate(outs, axis=1)


def _group_rms(x, g, group):
    return x * lax.rsqrt(_group_sumsq(x, g) * (1.0 / group) + NORM_EPS)


def _rope(x, c, s_next, s_prev, half):
    n = x.shape[1]
    return x * c + pltpu.roll(x, n - half, axis=1) * s_next + pltpu.roll(x, half, axis=1) * s_prev


def _inproj_kernel(x_ref, gv_ref, w_ref, wuq_ref, wuk_ref, wuv_ref, rope_ref, g32_ref, g64_ref, g128_ref,
                   o_ref, o4_ref, o16_ref, stage):
    def gain(row, width):
        return gv_ref[row:row + 1, 0:width]

    x = x_ref[...]
    h = _rms(x, D_MODEL) * gain(G_NORM1, D_MODEL)
    proj = _dot(_bf(h), w_ref[...])

    g32 = g32_ref[...]
    g64 = g64_ref[...]
    g128 = g128_ref[...]
    m_c = rope_ref[:, 0:128]
    m_sn = rope_ref[:, 128:256]
    m_sp = rope_ref[:, 256:384]
    a_c = rope_ref[:, 384:512]
    a_sn = rope_ref[:, 512:640]
    a_sp = rope_ref[:, 640:768]

    cq = _rms(proj[:, P_CQ:P_CQ + MLA_Q_RANK], MLA_Q_RANK) * gain(G_MLA_QN, MLA_Q_RANK)
    ckv = _rms(proj[:, P_CKV:P_CKV + MLA_KV_RANK], MLA_KV_RANK) * gain(G_MLA_KVN, MLA_KV_RANK)
    q = _dot(_bf(cq), wuq_ref[...])
    k_nope = _dot(_bf(ckv), wuk_ref[...])
    v = _dot(_bf(ckv), wuv_ref[...])
    k_rope = pltpu.roll(proj[:, P_KR:P_KR + LANES], MLA_NOPE_DIM, axis=1)
    k = k_nope + jnp.concatenate([k_rope] * N_HEADS, axis=1)
    qn = _group_rms(q, g128, MLA_QK_DIM) * gain(G_MLA_Q, 512)
    kn = _group_rms(k, g128, MLA_QK_DIM) * gain(G_MLA_K, 512)
    for g in range(N_HEADS):
        sl = slice(LANES * g, LANES * (g + 1))
        o_ref[:, A_QA + LANES * g:A_QA + LANES * (g + 1)] = _bf(_rope(qn[:, sl], m_c, m_sn, m_sp, MLA_ROPE_DIM // 2))
        o_ref[:, A_KA + LANES * g:A_KA + LANES * (g + 1)] = _bf(_rope(kn[:, sl], m_c, m_sn, m_sp, MLA_ROPE_DIM // 2))
    o_ref[:, A_VA:A_VA + 256] = _bf(v)

    qb = _group_rms(proj[:, P_BQ:P_BQ + 256], g64, HEAD_DIM) * gain(G_DIL_Q, 256)
    kb = _group_rms(proj[:, P_BK:P_BK + 256], g64, HEAD_DIM) * gain(G_DIL_K, 256)
    vb = proj[:, P_BV:P_BV + 256]
    o_ref[:, A_QB:A_QB + 256] = _bf(qb)
    o_ref[:, A_KB:A_KB + 256] = _bf(kb)
    o_ref[:, A_VB:A_VB + 256] = _bf(vb)
    for c, val in enumerate((qb, kb, vb)):
        stage[2 * c] = val[:, 0:LANES]
        stage[2 * c + 1] = val[:, LANES:2 * LANES]
    n_chunks = DIL_COLS // LANES
    for d, ref in ((4, o4_ref), (16, o16_ref)):
        n = x.shape[0] // d
        for r in range(d):
            ref[0, r] = _bf(jnp.concatenate(
                [stage[c, pl.ds(r, n, stride=d), :] for c in range(n_chunks)], axis=1))

    qc = _group_rms(proj[:, P_CQ2:P_CQ2 + 256], g64, HEAD_DIM) * gain(G_GQA_Q, 256)
    for g in range(2):
        sl = slice(LANES * g, LANES * (g + 1))
        o_ref[:, A_QC + LANES * g:A_QC + LANES * (g + 1)] = _bf(_rope(qc[:, sl], a_c, a_sn, a_sp, HEAD_DIM // 4))
    kc = _group_rms(proj[:, P_CK2:P_CK2 + 128], g64[0:128, 0:128], HEAD_DIM) * gain(G_GQA_K, 128)
    o_ref[:, A_KC:A_KC + 128] = _bf(_rope(kc, a_c, a_sn, a_sp, HEAD_DIM // 4))
    o_ref[:, A_VC:A_VC + 128] = _bf(proj[:, P_CV2:P_CV2 + 128])

    dqn = _group_rms(proj[:, P_DQ:P_DQ + 256], g32, DIFF_QK_DIM)
    o_ref[:, A_QD0:A_QD0 + 256] = _bf(dqn * gain(G_DIFF_Q0, 256))
    o_ref[:, A_QD1:A_QD1 + 256] = _bf(dqn * gain(G_DIFF_Q1, 256))
    o_ref[:, A_KD:A_KD + 256] = _bf(_group_rms(proj[:, P_DK:P_DK + 256], g32, DIFF_QK_DIM) * gain(G_DIFF_K, 256))
    o_ref[:, A_VD:A_VD + 256] = _bf(proj[:, P_DV:P_DV + 256])


def _inproj(x2d, gv, w_in, wuq, wuk, wuv, rope, g32, g64, g128):
    t = x2d.shape[0]
    tm = TM_PROJ
    n_pos = SEQ // tm
    const = lambda i: (0, 0)
    return pl.pallas_call(
        _inproj_kernel,
        grid=(t // tm,),
        in_specs=[
            pl.BlockSpec((tm, D_MODEL), lambda i: (i, 0)),
            pl.BlockSpec((GAIN_ROWS, D_MODEL), const),
            pl.BlockSpec((D_MODEL, PROJ_COLS), const),
            pl.BlockSpec((MLA_Q_RANK, 512), const),
            pl.BlockSpec((MLA_KV_RANK, 512), const),
            pl.BlockSpec((MLA_KV_RANK, 256), const),
            pl.BlockSpec((tm, 768), lambda i: (i % n_pos, 0)),
            pl.BlockSpec((MXU_DIM, MXU_DIM), const),
            pl.BlockSpec((MXU_DIM, MXU_DIM), const),
            pl.BlockSpec((MXU_DIM, MXU_DIM), const),
        ],
        out_specs=[
            pl.BlockSpec((tm, ACT_COLS), lambda i: (i, 0)),
            pl.BlockSpec((1, 4, tm // 4, DIL_COLS), lambda i: (i // n_pos, 0, i % n_pos, 0)),
            pl.BlockSpec((1, 16, tm // 16, DIL_COLS), lambda i: (i // n_pos, 0, i % n_pos, 0)),
        ],
        out_shape=[
            jax.ShapeDtypeStruct((t, ACT_COLS), BF16),
            jax.ShapeDtypeStruct((t // SEQ, 4, SEQ // 4, DIL_COLS), BF16),
            jax.ShapeDtypeStruct((t // SEQ, 16, SEQ // 16, DIL_COLS), BF16),
        ],
        scratch_shapes=[pltpu.VMEM((DIL_COLS // LANES, tm, LANES), F32)],
        compiler_params=pltpu.CompilerParams(dimension_semantics=("parallel",), vmem_limit_bytes=VMEM_LIMIT_BYTES),
        name="inproj_prep",
    )(x2d, gv, w_in, wuq, wuk, wuv, rope, g32, g64, g128)


def _softmax_pv(s, v):
    m = jnp.max(s, axis=-1, keepdims=True)
    e = jnp.exp2(s - m)
    l = jnp.sum(e, axis=-1, keepdims=True)
    return _dot(_bf(e), v) / l


def _heads_one_ahead(scores_of, finish):
    pending = scores_of(0)
    for h in range(N_HEADS):
        s = pending
        if h + 1 < N_HEADS:
            pending = scores_of(h + 1)
        finish(h, s)


def _attn_mla_kernel(q_ref, k_ref, v_ref, o_ref):
    def scores_of(h):
        sl = slice(LANES * h, LANES * (h + 1))
        return _dot_nt(q_ref[:, sl], k_ref[:, sl])

    def finish(h, s):
        hs = slice(HEAD_DIM * h, HEAD_DIM * (h + 1))
        o_ref[:, hs] = _softmax_pv(s, v_ref[:, hs])
    _heads_one_ahead(scores_of, finish)


def _attn_gqa_kernel(q_ref, k_ref, v_ref, o_ref):
    group = lambda h: slice(HEAD_DIM * (h // 2), HEAD_DIM * (h // 2 + 1))

    def scores_of(h):
        return _dot_nt(q_ref[:, HEAD_DIM * h:HEAD_DIM * (h + 1)], k_ref[:, group(h)])

    def finish(h, s):
        o_ref[:, HEAD_DIM * h:HEAD_DIM * (h + 1)] = _softmax_pv(s, v_ref[:, group(h)])
    _heads_one_ahead(scores_of, finish)


def _attn_full(act, batch, kernel, q_col, q_w, k_col, k_w, v_col, v_w, name):
    t = act.shape[0]
    tq = TQ_FULL
    nq = SEQ // tq
    return pl.pallas_call(
        kernel,
        grid=(batch, nq),
        in_specs=[
            pl.BlockSpec((tq, q_w), lambda b, i: (b * nq + i, q_col // q_w)),
            pl.BlockSpec((SEQ, k_w), lambda b, i: (b, k_col // k_w)),
            pl.BlockSpec((SEQ, v_w), lambda b, i: (b, v_col // v_w)),
        ],
        out_specs=pl.BlockSpec((tq, 256), lambda b, i: (b * nq + i, 0)),
        out_shape=jax.ShapeDtypeStruct((t, 256), F32),
        compiler_params=pltpu.CompilerParams(
            dimension_semantics=("parallel", "parallel"), vmem_limit_bytes=VMEM_LIMIT_BYTES),
        name=name,
    )(act, act, act)


def _attn_diff_kernel(lam_ref, q0_ref, q1_ref, k_ref, v_ref, win_ref, sg_ref, o_ref, bias_ref):
    @pl.when(pl.program_id(1) == 0)
    def _():
        for h in range(N_HEADS):
            w = jnp.broadcast_to(win_ref[0, h:h + 1, :], (TQ, SEQ + TQ))
            bias_ref[h] = pltpu.roll(w, 0, axis=1, stride=1, stride_axis=0)[:, TQ:TQ + SEQ]

    lam = lam_ref[0]

    def scores_of(h):
        hs = slice(HEAD_DIM * h, HEAD_DIM * (h + 1))
        k = k_ref[:, hs]
        return _dot_nt(q0_ref[:, hs], k), _dot_nt(q1_ref[:, hs], k)

    def finish(h, scores):
        hs = slice(HEAD_DIM * h, HEAD_DIM * (h + 1))
        bias = bias_ref[h]
        s0, s1 = scores[0] + bias, scores[1] + bias
        e0 = jnp.exp2(s0 - jnp.max(s0, axis=-1, keepdims=True))
        e1 = jnp.exp2(s1 - jnp.max(s1, axis=-1, keepdims=True))
        r0 = 1.0 / jnp.sum(e0, axis=-1, keepdims=True)
        r1 = lam / jnp.sum(e1, axis=-1, keepdims=True)
        v = v_ref[:, hs]
        o = r0 * _dot(_bf(e0), v) - r1 * _dot(_bf(e1), v)
        o_ref[:, hs] = _rms(o, HEAD_DIM) * sg_ref[:, hs]
    _heads_one_ahead(scores_of, finish)


def _attn_diff(act, batch, lam, bias_win, sub_gain):
    t = act.shape[0]
    nq = SEQ // TQ
    return pl.pallas_call(
        _attn_diff_kernel,
        grid=(nq, batch),
        in_specs=[
            pl.BlockSpec(memory_space=pltpu.SMEM),
            pl.BlockSpec((TQ, 256), lambda i, b: (b * nq + i, A_QD0 // 256)),
            pl.BlockSpec((TQ, 256), lambda i, b: (b * nq + i, A_QD1 // 256)),
            pl.BlockSpec((SEQ, 256), lambda i, b: (b, A_KD // 256)),
            pl.BlockSpec((SEQ, 256), lambda i, b: (b, A_VD // 256)),
            pl.BlockSpec((1, N_HEADS, SEQ + TQ), lambda i, b: (i, 0, 0)),
            pl.BlockSpec((1, 256), lambda i, b: (0, 0)),
        ],
        out_specs=pl.BlockSpec((TQ, 256), lambda i, b: (b * nq + i, 0)),
        out_shape=jax.ShapeDtypeStruct((t, 256), F32),
        scratch_shapes=[pltpu.VMEM((N_HEADS, TQ, SEQ), F32)],
        compiler_params=pltpu.CompilerParams(
            dimension_semantics=("arbitrary", "arbitrary"), vmem_limit_bytes=VMEM_LIMIT_BYTES),
        name="attn_diff",
    )(lam, act, act, act, act, bias_win, sub_gain)


def _dil_branch(q_of, k_of, v_of, bias_of, n_seq):
    outs, lses = [], []
    for h in range(N_HEADS):
        hs = slice(HEAD_DIM * h, HEAD_DIM * (h + 1))
        q, k, v = q_of(hs), k_of(hs), v_of(hs)
        n = q.shape[1]
        if n == DIL_QBLK:
            kw, vw = k, v
            bias = bias_of(h)[:, DIL_HALF:DIL_HALF + DIL_QBLK]
            edge = None
        else:
            n_blk = n // DIL_QBLK

            def windows(x):
                zeros = jnp.zeros((n_seq, DIL_HALF, HEAD_DIM), BF16)
                xp = jnp.concatenate([zeros, x, zeros], axis=1)
                first = xp[:, 0:n].reshape(n_seq * n_blk, DIL_QBLK, HEAD_DIM)
                second = xp[:, DIL_QBLK:DIL_QBLK + n].reshape(n_seq * n_blk, DIL_QBLK, HEAD_DIM)
                return jnp.concatenate([first, second], axis=1)
            kw, vw = windows(k), windows(v)
            q = q.reshape(n_seq * n_blk, DIL_QBLK, HEAD_DIM)
            bias = bias_of(h)
            blk = lax.broadcasted_iota(jnp.int32, (n_seq * n_blk, 1, 2 * DIL_QBLK), 0) % n_blk
            col = lax.broadcasted_iota(jnp.int32, (n_seq * n_blk, 1, 2 * DIL_QBLK), 2)
            key_pos = col - DIL_HALF + blk * DIL_QBLK
            edge = jnp.where(jnp.logical_and(key_pos >= 0, key_pos < n), 0.0, NEG_INF)
        s = jnp.einsum("bqe,bke->bqk", q, kw, preferred_element_type=F32) + bias[None]
        if edge is not None:
            s = s + edge
        m = jnp.max(s, axis=-1, keepdims=True)
        e = jnp.exp2(s - m)
        l = jnp.sum(e, axis=-1, keepdims=True)
        o = jnp.einsum("bqk,bke->bqe", _bf(e), vw, preferred_element_type=F32) / l
        lse = jnp.broadcast_to(m + jnp.log2(l), o.shape)
        outs.append(o.reshape(n_seq, n, HEAD_DIM))
        lses.append(lse.reshape(n_seq, n, HEAD_DIM))
    cat = lambda xs: [jnp.concatenate(xs[2 * c:2 * c + 2], axis=2) for c in range(2)]
    return cat(outs), cat(lses)


def _dil_kernel(n1_ref, r4_ref, r16_ref, bias_ref, o_ref, o1, l1, o4, l4, out):
    col_of = lambda base: (lambda hs: slice(base + hs.start, base + hs.stop))
    q_cols, k_cols, v_cols = col_of(0), col_of(256), col_of(512)

    oc, lc = _dil_branch(lambda hs: n1_ref[:, q_cols(hs)][None], lambda hs: n1_ref[:, k_cols(hs)][None],
                         lambda hs: n1_ref[:, v_cols(hs)][None], lambda h: bias_ref[0, h], 1)
    for c in range(2):
        o1[c] = oc[c][0]
        l1[c] = lc[c][0]

    oc, lc = _dil_branch(lambda hs: r4_ref[0, :, :, q_cols(hs)], lambda hs: r4_ref[0, :, :, k_cols(hs)],
                         lambda hs: r4_ref[0, :, :, v_cols(hs)], lambda h: bias_ref[1, h], 4)
    for c in range(2):
        for r in range(4):
            tok = pl.ds(r, SEQ // 4, stride=4)
            o4[c, tok, :] = oc[c][r]
            l4[c, tok, :] = lc[c][r]

    oc, lc = _dil_branch(lambda hs: r16_ref[0, :, :, q_cols(hs)], lambda hs: r16_ref[0, :, :, k_cols(hs)],
                         lambda hs: r16_ref[0, :, :, v_cols(hs)], lambda h: bias_ref[2, h], 16)
    toks = [pl.ds(r, SEQ // 16, stride=16) for r in range(16)]
    for c in range(2):
        gather = lambda ref: jnp.stack([ref[c, tok, :] for tok in toks])
        la, lb = gather(l1), gather(l4)
        mx = jnp.maximum(jnp.maximum(la, lb), lc[c])
        wa, wb, wc = jnp.exp2(la - mx), jnp.exp2(lb - mx), jnp.exp2(lc[c] - mx)
        den = wa + wb + wc
        res = (wa / den) * gather(o1) + (wb / den) * gather(o4) + (wc / den) * oc[c]
        for r, tok in enumerate(toks):
            out[c, tok, :] = res[r]
    o_ref[:, 0:LANES] = out[0]
    o_ref[:, LANES:2 * LANES] = out[1]


def _attn_dilated(act, act4, act16, batch, bias_tabs):
    t = act.shape[0]
    nat = pltpu.VMEM((2, SEQ, LANES), F32)
    return pl.pallas_call(
        _dil_kernel,
        grid=(batch,),
        in_specs=[
            pl.BlockSpec((SEQ, DIL_COLS), lambda b: (b, 0)),
            pl.BlockSpec((1, 4, SEQ // 4, DIL_COLS), lambda b: (b, 0, 0, 0)),
            pl.BlockSpec((1, 16, SEQ // 16, DIL_COLS), lambda b: (b, 0, 0, 0)),
            pl.BlockSpec((3, N_HEADS, DIL_QBLK, 2 * DIL_QBLK), lambda b: (0, 0, 0, 0)),
        ],
        out_specs=pl.BlockSpec((SEQ, 256), lambda b: (b, 0)),
        out_shape=jax.ShapeDtypeStruct((t, 256), F32),
        scratch_shapes=[nat, nat, nat, nat, nat],
        compiler_params=pltpu.CompilerParams(dimension_semantics=("parallel",), vmem_limit_bytes=VMEM_LIMIT_BYTES),
        name="attn_dilated",
    )(act, act4, act16, bias_tabs)


def _route(lt):
    g = [lt[i:i + 1, :] for i in range(4)]
    gmax = jnp.maximum(jnp.maximum(g[0], g[1]), jnp.maximum(g[2], g[3]))
    gsum = sum(jnp.exp(gi - gmax) for gi in g)
    g_w = 1.0 / gsum
    gidx = jnp.where(g[0] == gmax, 0, jnp.where(g[1] == gmax, 1, jnp.where(g[2] == gmax, 2, 3)))
    el = []
    for j in range(EXPERTS_PER_GROUP):
        acc = jnp.zeros_like(g[0])
        for i in range(4):
            r = 4 + EXPERTS_PER_GROUP * i + j
            acc = jnp.where(gidx == i, lt[r:r + 1, :], acc)
        el.append(acc)
    emax = jnp.maximum(jnp.maximum(el[0], el[1]), jnp.maximum(el[2], el[3]))
    ee = [jnp.exp(e - emax) for e in el]
    esum = ee[0] + ee[1] + ee[2] + ee[3]
    p = [e / esum for e in ee]
    p1 = jnp.maximum(jnp.maximum(p[0], p[1]), jnp.maximum(p[2], p[3]))
    i1 = jnp.where(p[0] == p1, 0, jnp.where(p[1] == p1, 1, jnp.where(p[2] == p1, 2, 3)))
    pm = [jnp.where(i1 == j, -1.0, p[j]) for j in range(4)]
    p2 = jnp.maximum(jnp.maximum(pm[0], pm[1]), jnp.maximum(pm[2], pm[3]))
    i2 = jnp.where(pm[0] == p2, 0, jnp.where(pm[1] == p2, 1, jnp.where(pm[2] == p2, 2, 3)))
    den = p1 + p2
    return gidx, i1, i2, g_w * (p1 / den), g_w * (p2 / den)


def _pair_offset(lo):
    return jnp.where(lo == 0, 0, jnp.where(lo == 1, 3, 5))


def _outproj_kernel(ya_ref, yb_ref, yc_ref, yd_ref, x_ref, cv_ref, w_ref, wrh_ref, wrl_ref, rb_ref, tri_ref,
                    x1_ref, hp_ref, ri_ref, cnt_out_ref, cnt_ref):
    tm = x_ref.shape[0]

    @pl.when(pl.program_id(0) == 0)
    def _():
        cnt_ref[...] = jnp.zeros_like(cnt_ref)

    mixed = jnp.concatenate(
        [_rms(ya_ref[...], 256), _rms(yb_ref[...], 256), _rms(yc_ref[...], 256), yd_ref[...]], axis=1)
    mixed = mixed * cv_ref[0:1, :]
    x1 = x_ref[...] + _dot(_bf(mixed), w_ref[...])
    x1_ref[...] = x1
    h2 = _rms(x1, D_MODEL) * cv_ref[1:2, :]
    h_hi = _bf(h2)
    h_rt = h_hi.astype(F32)
    h_lo = _bf(h2 - h_rt)
    logits = _dot(h_hi, wrh_ref[...]) + _dot(h_hi, wrl_ref[...]) + _dot(h_lo, wrh_ref[...]) + rb_ref[...]
    gidx, i1, i2, wa, wb = _route(logits.T)

    lo, hi = jnp.minimum(i1, i2), jnp.maximum(i1, i2)
    first_is_lo = i1 < i2
    w_lo, w_hi = jnp.where(first_is_lo, wa, wb), jnp.where(first_is_lo, wb, wa)
    cls = gidx * PAIRS_PER_GROUP + _pair_offset(lo) + (hi - lo - 1)

    rows = lax.broadcasted_iota(jnp.int32, (CLS_ROWS, tm), 0)
    onehot = rows == cls
    oh = jnp.where(onehot, 1.0, 0.0)
    prefix = _dot(_bf(oh), tri_ref[...])
    before = cnt_ref[:, 0:1] + prefix
    rank = jnp.sum(jnp.where(onehot, before, 0.0), axis=0, keepdims=True)
    cnt_ref[...] = cnt_ref[...] + jnp.sum(oh, axis=1, keepdims=True)
    cnt_out_ref[...] = cnt_ref[...]
    ri_ref[...] = jnp.concatenate([cls, rank.astype(jnp.int32), jnp.zeros((6, tm), jnp.int32)], axis=0)

    hp_ref[:, 0:D_MODEL] = h_rt
    rows_w = lax.broadcasted_iota(jnp.int32, (LANES, tm), 0)
    w_t = jnp.where(rows_w == 0, w_lo, jnp.where(rows_w == 1, w_hi, 0.0))
    hp_ref[:, D_MODEL:XS_COLS] = w_t.T


def _outproj(ya, yb, yc, yd, x2d, cv, w_out, wr_hi, wr_lo, rb, tri):
    t = x2d.shape[0]
    tm = TM_OUT
    row = lambda w: pl.BlockSpec((tm, w), lambda i: (i, 0))
    const = lambda shape: pl.BlockSpec(shape, lambda i: (0, 0))
    return pl.pallas_call(
        _outproj_kernel,
        grid=(t // tm,),
        in_specs=[row(256), row(256), row(256), row(256), row(D_MODEL), const((8, D_MODEL)),
                  const((D_MODEL, D_MODEL)), const((D_MODEL, LANES)), const((D_MODEL, LANES)), const((1, LANES)),
                  const((tm, tm))],
        out_specs=[row(D_MODEL), row(XS_COLS), pl.BlockSpec((8, tm), lambda i: (0, i)), const((CLS_ROWS, LANES))],
        out_shape=[jax.ShapeDtypeStruct((t, D_MODEL), F32), jax.ShapeDtypeStruct((t, XS_COLS), F32),
                   jax.ShapeDtypeStruct((8, t), jnp.int32), jax.ShapeDtypeStruct((CLS_ROWS, LANES), F32)],
        scratch_shapes=[pltpu.VMEM((CLS_ROWS, LANES), F32)],
        compiler_params=pltpu.CompilerParams(dimension_semantics=("arbitrary",), vmem_limit_bytes=VMEM_LIMIT_BYTES),
        name="outproj_router",
    )(ya, yb, yc, yd, x2d, cv, w_out, wr_hi, wr_lo, rb, tri)


def _moe_plan(ri, counts, n_tiles):
    cls, rank = ri[0], ri[1]
    cnt = counts[:N_CLASSES, 0].astype(jnp.int32)
    padded = ((cnt + TM_MOE - 1) // TM_MOE) * TM_MOE
    ends = jnp.cumsum(padded)
    offs = ends - padded
    classes = jnp.arange(N_CLASSES, dtype=jnp.int32)
    dest = jnp.sum(jnp.where(cls[:, None] == classes[None, :], offs[None, :], 0), axis=1) + rank
    tile_start = jnp.arange(n_tiles, dtype=jnp.int32) * TM_MOE
    tile_cls = jnp.sum((tile_start[:, None] >= ends[None, :]).astype(jnp.int32), axis=1)
    tile_cls = jnp.minimum(tile_cls, N_CLASSES - 1)
    group, pair = tile_cls // PAIRS_PER_GROUP, tile_cls % PAIRS_PER_GROUP
    lo = (pair >= 3).astype(jnp.int32) + (pair >= 5).astype(jnp.int32)
    hi = pair - _pair_offset(lo) + lo + 1
    n_active = (ends[-1] // TM_MOE).reshape(1)
    return dest, group * EXPERTS_PER_GROUP + lo, group * EXPERTS_PER_GROUP + hi, n_active


def _dispatch_kernel(dest_ref, h_ref, xs_in_ref, xs_ref, stage, sems):
    del xs_in_ref
    tm = h_ref.shape[0]
    i = pl.program_id(0)
    slot = i % 2

    def wait_slot(s):
        pltpu.make_async_copy(stage.at[s], xs_ref.at[pl.ds(0, tm), :], sems.at[s]).wait()

    @pl.when(i >= 2)
    def _():
        wait_slot(slot)

    stage[slot] = h_ref[...]
    for r in range(tm):
        pltpu.make_async_copy(stage.at[slot, pl.ds(r, 1), :], xs_ref.at[pl.ds(dest_ref[0, 0, r], 1), :],
                              sems.at[slot]).start()

    @pl.when(i == pl.num_programs(0) - 1)
    def _():
        wait_slot(1 - slot)
        wait_slot(slot)


def _dispatch(dest, hp, n_rows):
    t = hp.shape[0]
    tm = TM_MOE
    assert t // tm >= 2
    return pl.pallas_call(
        _dispatch_kernel,
        grid=(t // tm,),
        in_specs=[
            pl.BlockSpec((1, 1, tm), lambda i: (i, 0, 0), memory_space=pltpu.SMEM),
            pl.BlockSpec((tm, XS_COLS), lambda i: (i, 0)),
            pl.BlockSpec(memory_space=pl.ANY),
        ],
        out_specs=pl.BlockSpec(memory_space=pl.ANY),
        out_shape=jax.ShapeDtypeStruct((n_rows, XS_COLS), F32),
        scratch_shapes=[pltpu.VMEM((2, tm, XS_COLS), F32), pltpu.SemaphoreType.DMA((2,))],
        input_output_aliases={2: 0},
        compiler_params=pltpu.CompilerParams(dimension_semantics=("arbitrary",)),
        name="moe_dispatch",
    )(dest.reshape(t // tm, 1, tm), hp, jnp.zeros((n_rows, XS_COLS), F32))


def _moe_group_kernel(elo_ref, ehi_ref, nact_ref, xs_ref, wg0, wu0, wd0, wg1, wu1, wd1, o_ref, wg_s, wu_s, wd_s):
    i = pl.program_id(0)
    active = i < nact_ref[0]
    prev = jnp.maximum(i - 1, 0)
    changed = jnp.logical_or(i == 0, jnp.logical_or(elo_ref[i] != elo_ref[prev], ehi_ref[i] != ehi_ref[prev]))

    @pl.when(jnp.logical_and(active, changed))
    def _():
        for slot, (wg, wu, wd) in enumerate(((wg0, wu0, wd0), (wg1, wu1, wd1))):
            wg_s[slot] = _bf(wg[0, 0])
            wu_s[slot] = _bf(wu[0, 0])
            wd_s[slot] = _bf(wd[0, 0])

    @pl.when(active)
    def _():
        x = _bf(xs_ref[:, 0:D_MODEL])
        gate = xs_ref[:, D_MODEL:XS_COLS]

        def expert(slot):
            a = _dot(x, wg_s[slot])
            hid = a * (1.0 / (1.0 + jnp.exp(-a))) * _dot(x, wu_s[slot])
            return _dot(_bf(hid), wd_s[slot])
        o_ref[...] = gate[:, 0:1] * expert(0) + gate[:, 1:2] * expert(1)

    @pl.when(jnp.logical_not(active))
    def _():
        o_ref[...] = jnp.zeros_like(o_ref)


def _moe_group(layer, e_lo, e_hi, n_active, xs, wg, wu, wd):
    n_rows = xs.shape[0]
    tm = TM_MOE
    w_in = lambda which: pl.BlockSpec((1, 1, D_MODEL, D_FF), lambda i, lo, hi, na: (layer, (lo, hi)[which][i], 0, 0))
    w_dn = lambda which: pl.BlockSpec((1, 1, D_FF, D_MODEL), lambda i, lo, hi, na: (layer, (lo, hi)[which][i], 0, 0))
    return pl.pallas_call(
        _moe_group_kernel,
        grid_spec=pltpu.PrefetchScalarGridSpec(
            num_scalar_prefetch=3,
            grid=(n_rows // tm,),
            in_specs=[pl.BlockSpec((tm, XS_COLS), lambda i, lo, hi, na: (i, 0)),
                      w_in(0), w_in(0), w_dn(0), w_in(1), w_in(1), w_dn(1)],
            out_specs=pl.BlockSpec((tm, D_MODEL), lambda i, lo, hi, na: (i, 0)),
            scratch_shapes=[pltpu.VMEM((2, D_MODEL, D_FF), BF16), pltpu.VMEM((2, D_MODEL, D_FF), BF16),
                            pltpu.VMEM((2, D_FF, D_MODEL), BF16)],
        ),
        out_shape=jax.ShapeDtypeStruct((n_rows, D_MODEL), F32),
        compiler_params=pltpu.CompilerParams(dimension_semantics=("arbitrary",), vmem_limit_bytes=VMEM_LIMIT_BYTES),
        name="moe_group",
    )(e_lo, e_hi, n_active, xs, wg, wu, wd, wg, wu, wd)


def _combine_kernel(dest_ref, dest_next_ref, x1_ref, y_ref, o_ref, buf, sems):
    tm = x1_ref.shape[0]
    i = pl.program_id(0)
    slot = i % 2

    def gather(idx_ref, s):
        for r in range(tm):
            pltpu.make_async_copy(y_ref.at[pl.ds(idx_ref[0, 0, r], 1), :], buf.at[s, pl.ds(r, 1), :],
                                  sems.at[s]).start()

    @pl.when(i == 0)
    def _():
        gather(dest_ref, 0)

    @pl.when(i + 1 < pl.num_programs(0))
    def _():
        gather(dest_next_ref, 1 - slot)

    pltpu.make_async_copy(y_ref.at[pl.ds(0, tm), :], buf.at[slot], sems.at[slot]).wait()
    o_ref[...] = x1_ref[...] + buf[slot]


def _combine(dest, x1, y):
    t = x1.shape[0]
    tm = TM_MOE
    n = t // tm
    dest3 = dest.reshape(n, 1, tm)
    return pl.pallas_call(
        _combine_kernel,
        grid=(n,),
        in_specs=[
            pl.BlockSpec((1, 1, tm), lambda i: (i, 0, 0), memory_space=pltpu.SMEM),
            pl.BlockSpec((1, 1, tm), lambda i: (jnp.minimum(i + 1, n - 1), 0, 0), memory_space=pltpu.SMEM),
            pl.BlockSpec((tm, D_MODEL), lambda i: (i, 0)),
            pl.BlockSpec(memory_space=pl.ANY),
        ],
        out_specs=pl.BlockSpec((tm, D_MODEL), lambda i: (i, 0)),
        out_shape=jax.ShapeDtypeStruct((t, D_MODEL), F32),
        scratch_shapes=[pltpu.VMEM((2, tm, D_MODEL), F32), pltpu.SemaphoreType.DMA((2,))],
        compiler_params=pltpu.CompilerParams(dimension_semantics=("arbitrary",)),
        name="moe_combine",
    )(dest3, dest3, x1, y)


def _rope_angles(pos, dim):
    inv = 1.0 / (ROPE_THETA ** (jnp.arange(0, dim, 2, dtype=F32) / dim))
    return pos.astype(F32)[:, None] * inv[None, :]


def _rope_tables():
    pos = jnp.arange(SEQ, dtype=jnp.int32)
    z = lambda w: jnp.zeros((SEQ, w), F32)
    ang = _rope_angles(pos, MLA_ROPE_DIM)
    c, s = jnp.cos(ang), jnp.sin(ang)
    m_c = jnp.concatenate([jnp.ones((SEQ, 64), F32), c, c, z(32)], axis=1)
    m_sn = jnp.concatenate([z(64), -s, z(48)], axis=1)
    m_sp = jnp.concatenate([z(80), s, z(32)], axis=1)
    row_pos = pos // GRID_W
    col_pos = pos - row_pos * GRID_W
    ra, ca = _rope_angles(row_pos, HEAD_DIM // 2), _rope_angles(col_pos, HEAD_DIM // 2)
    rc, rs, cc, cs = jnp.cos(ra), jnp.sin(ra), jnp.cos(ca), jnp.sin(ca)
    a_c = jnp.concatenate([rc, rc, cc, cc] * 2, axis=1)
    a_sn = jnp.concatenate([-rs, z(16), -cs, z(16)] * 2, axis=1)
    a_sp = jnp.concatenate([z(16), rs, z(16), cs] * 2, axis=1)
    return jnp.concatenate([m_c, m_sn, m_sp, a_c, a_sn, a_sp], axis=1)


def _rel_bucket(rel):
    half = NUM_BUCKETS // 2
    max_exact = half // 2
    n = jnp.abs(rel)
    nf = jnp.maximum(n, 1).astype(F32)
    log_ratio = jnp.log(nf / max_exact) / math.log(REL_MAX_DISTANCE / max_exact)
    large = jnp.minimum(max_exact + (log_ratio * (half - max_exact)).astype(jnp.int32), half - 1)
    return jnp.where(rel > 0, half, 0) + jnp.where(n < max_exact, n, large)


def _bias_line(table, rel):
    return table[_rel_bucket(rel)].T


def _toeplitz(line, rows, cols, first):
    n = line.shape[-1]
    padded = jnp.pad(line, [(0, 0)] * (line.ndim - 1) + [(0, 1)])
    flat = jnp.tile(padded, rows)[..., :rows * n]
    skew = flat.reshape(line.shape[:-1] + (rows, n))
    return skew[..., first:first + cols]


def _dil_bias(table):
    reach = DIL_QBLK + DIL_HALF - 1
    steps = jnp.arange(-reach, reach + 1, dtype=jnp.int32)
    band = jnp.abs(steps) <= DIL_HALF
    lines = jnp.stack([jnp.where(band[None], _bias_line(table, steps * d), NEG_INF) for _, d in DIL_PATTERNS])
    return _toeplitz(lines, DIL_QBLK, 2 * DIL_QBLK, DIL_QBLK - 1)


def _diff_bias_windows(table):
    rel = jnp.arange(2 * SEQ, dtype=jnp.int32) - SEQ
    line = _bias_line(table, rel)
    starts = [SEQ - (i + 1) * TQ for i in range(SEQ // TQ)]
    return jnp.stack([line[:, s0:s0 + SEQ + TQ] for s0 in starts])


def _block_ones(group):
    idx = np.arange(MXU_DIM) // group
    return jnp.asarray(idx[:, None] == idx[None, :], dtype=BF16)


def _pad_row(v):
    return jnp.pad(v.astype(F32), (0, D_MODEL - v.shape[0]))


def _layer_params(layer, norm1_g, w_in, mla_q_norm_g, mla_kv_norm_g, mla_w_uq, mla_w_ukv, mla_qk_g, dil_qk_g,
                  gqa_qk_g, diff_qk_g, diff_lambda, diff_subln_g, mix_beta, w_out, norm2_g, router_group_w,
                  router_group_b, router_expert_w, router_expert_b):
    w = w_in[layer]
    w_p = _bf(jnp.concatenate([w[:, :416], jnp.zeros((D_MODEL, 96), F32), w[:, 416:]], axis=1))
    uq = mla_w_uq[layer].reshape(MLA_Q_RANK, N_HEADS, MLA_QK_DIM)
    wuq = _bf(jnp.pad(uq, ((0, 0), (0, 0), (0, LANES - MLA_QK_DIM))).reshape(MLA_Q_RANK, 512))
    ukv = mla_w_ukv[layer].reshape(MLA_KV_RANK, N_HEADS, 2 * MLA_NOPE_DIM)
    wuk = _bf(jnp.pad(ukv[:, :, :MLA_NOPE_DIM], ((0, 0), (0, 0), (0, LANES - MLA_NOPE_DIM))).reshape(MLA_KV_RANK, 512))
    wuv = _bf(ukv[:, :, MLA_NOPE_DIM:].reshape(MLA_KV_RANK, 256))

    pad96 = lambda g: jnp.tile(jnp.pad(g, (0, LANES - MLA_QK_DIM)), N_HEADS)
    m0 = jnp.tile(jnp.concatenate([jnp.ones(32, F32), jnp.zeros(32, F32)]), N_HEADS)
    dq = jnp.tile(diff_qk_g[layer, 0], 2 * N_HEADS) * (DIFF_QK_DIM ** -0.5 * LOG2E)
    rows = [
        norm1_g[layer], mla_q_norm_g[layer], mla_kv_norm_g[layer],
        pad96(mla_qk_g[layer, 0]) * (MLA_QK_DIM ** -0.5 * LOG2E), pad96(mla_qk_g[layer, 1]),
        jnp.tile(dil_qk_g[layer, 0], N_HEADS) * (HEAD_DIM ** -0.5 * LOG2E), jnp.tile(dil_qk_g[layer, 1], N_HEADS),
        jnp.tile(gqa_qk_g[layer, 0], N_HEADS) * (HEAD_DIM ** -0.5 * LOG2E), jnp.tile(gqa_qk_g[layer, 1], 2),
        dq * m0, dq * (1.0 - m0), jnp.tile(diff_qk_g[layer, 1], 2 * N_HEADS),
    ]
    gv = jnp.stack([_pad_row(r) for r in rows] + [jnp.zeros(D_MODEL, F32)] * (GAIN_ROWS - len(rows)))

    lambda_init = 0.8 - 0.6 * math.exp(-0.3 * layer)
    lv = diff_lambda[layer].astype(F32)
    lam = (jnp.exp(jnp.sum(lv[0] * lv[1])) - jnp.exp(jnp.sum(lv[2] * lv[3])) + lambda_init).reshape(1)
    sub_gain = (jnp.tile(diff_subln_g[layer], N_HEADS) * (1.0 - lambda_init)).reshape(1, 256)

    cv = jnp.stack([mix_beta[layer], norm2_g[layer]] + [jnp.zeros(D_MODEL, F32)] * 6)
    wr = jnp.concatenate([router_group_w[layer], router_expert_w[layer],
                          jnp.zeros((D_MODEL, LANES - 4 - N_EXPERTS), F32)], axis=1)
    wr_hi = _bf(wr)
    wr_lo = _bf(wr - wr_hi.astype(F32))
    rb = jnp.concatenate([router_group_b[layer], router_expert_b[layer],
                          jnp.zeros(LANES - 4 - N_EXPERTS, F32)]).reshape(1, LANES)
    return dict(w_in=w_p, wuq=wuq, wuk=wuk, wuv=wuv, gv=gv, lam=lam, sub_gain=sub_gain, cv=cv,
                w_out=_bf(w_out[layer]), wr_hi=wr_hi, wr_lo=wr_lo, rb=rb)


def kernel(x, rel_bias, norm1_g, w_in, mla_q_norm_g, mla_kv_norm_g, mla_w_uq, mla_w_ukv, mla_qk_g, dil_qk_g, gqa_qk_g, diff_qk_g, diff_lambda, diff_subln_g, mix_beta, w_out, norm2_g, router_group_w, router_group_b, router_expert_w, router_expert_b, expert_w_gate, expert_w_up, expert_w_down):
    batch, seq, d_model = x.shape
    assert seq == SEQ and d_model == D_MODEL
    depth = w_in.shape[0]
    t = batch * seq

    rope = _rope_tables()
    g32, g64, g128 = _block_ones(32), _block_ones(64), _block_ones(128)
    tri = jnp.asarray(np.arange(TM_OUT)[:, None] < np.arange(TM_OUT)[None, :], dtype=BF16)
    dil_bias = _dil_bias(rel_bias[:, :N_HEADS] * LOG2E)
    diff_bias = _diff_bias_windows(rel_bias[:, N_HEADS:] * LOG2E)

    x2d = x.reshape(t, D_MODEL)
    for layer in range(depth):
        p = _layer_params(layer, norm1_g, w_in, mla_q_norm_g, mla_kv_norm_g, mla_w_uq, mla_w_ukv, mla_qk_g,
                          dil_qk_g, gqa_qk_g, diff_qk_g, diff_lambda, diff_subln_g, mix_beta, w_out, norm2_g,
                          router_group_w, router_group_b, router_expert_w, router_expert_b)
        act, act4, act16 = _inproj(x2d, p["gv"], p["w_in"], p["wuq"], p["wuk"], p["wuv"], rope, g32, g64, g128)
        ya = _attn_full(act, batch, _attn_mla_kernel, A_QA, 512, A_KA, 512, A_VA, 256, "attn_mla")
        yb = _attn_dilated(act, act4, act16, batch, dil_bias)
        yc = _attn_full(act, batch, _attn_gqa_kernel, A_QC, 256, A_KC, 128, A_VC, 128, "attn_gqa")
        yd = _attn_diff(act, batch, p["lam"], diff_bias, p["sub_gain"])
        x1, hp, ri, counts = _outproj(ya, yb, yc, yd, x2d, p["cv"], p["w_out"], p["wr_hi"], p["wr_lo"], p["rb"], tri)
        n_tiles = t // TM_MOE + N_CLASSES
        dest, e_lo, e_hi, n_active = _moe_plan(ri, counts, n_tiles)
        xs = _dispatch(dest, hp, n_tiles * TM_MOE)
        y = _moe_group(layer, e_lo, e_hi, n_active, xs, expert_w_gate, expert_w_up, expert_w_down)
        x2d = _combine(dest, x1, y)
    return x2d.reshape(batch, seq, D_MODEL)
```

```python
import functools
import math

import jax
import jax.numpy as jnp
import numpy as np
from jax import lax
from jax.experimental import pallas as pl
from jax.experimental.pallas import tpu as pltpu

F32 = jnp.float32
BF16 = jnp.bfloat16

D_MODEL = 1024
SEQ = 2048
HEAD_DIM = 64
GRID_W = 64
ROPE_THETA = 10000.0
LOG2E = 1.0 / math.log(2.0)
NORM_EPS = 1e-6
NEG_INF = -1e30
NUM_BUCKETS = 32
REL_MAX_DISTANCE = 1024

N_HEADS = 4
MLA_NOPE_DIM = 64
MLA_ROPE_DIM = 32
MLA_QK_DIM = MLA_NOPE_DIM + MLA_ROPE_DIM
MLA_Q_RANK = 256
MLA_KV_RANK = 128
DIL_PATTERNS = ((128, 1), (512, 4), (2048, 16))
DIL_HALF = 64
DIFF_QK_DIM = 32
N_EXPERTS = 16
EXPERTS_PER_GROUP = 4
D_FF = 512

LANES = 128
MXU_DIM = 256
VMEM_LIMIT_BYTES = 56 * 1024 * 1024

P_CQ, P_CKV, P_KR = 0, 256, 384
P_BQ, P_BK, P_BV = 512, 768, 1024
P_CQ2, P_CK2, P_CV2 = 1280, 1536, 1664
P_DQ, P_DK, P_DV = 1792, 2048, 2304
PROJ_COLS = 2560

A_QB, A_KB, A_VB = 0, 256, 512
A_VA, A_QA, A_KA = 768, 1024, 1536
A_QC, A_KC, A_VC = 2048, 2304, 2432
A_QD0, A_QD1, A_KD, A_VD = 2560, 2816, 3072, 3328
ACT_COLS = 3584
DIL_COLS = 768

(G_NORM1, G_MLA_QN, G_MLA_KVN, G_MLA_Q, G_MLA_K, G_DIL_Q, G_DIL_K, G_GQA_Q, G_GQA_K,
 G_DIFF_Q0, G_DIFF_Q1, G_DIFF_K) = range(12)
GAIN_ROWS = 16

TM_PROJ = 512
TQ = 256
TQ_FULL = 512
TM_OUT = 512
TM_MOE = 256
TM_ROWS_DMA = 512
DIL_QBLK = 128

PAIRS_PER_GROUP = 6
N_CLASSES = 4 * PAIRS_PER_GROUP
CLS_ROWS = 32
XS_COLS = D_MODEL + LANES


def _bf(x):
    return x.astype(BF16)


def _dot(a, b):
    return jnp.dot(a, b, preferred_element_type=F32)


def _dot_nt(a, b):
    return lax.dot_general(a, b, (((1,), (1,)), ((), ())), preferred_element_type=F32)


def _rms(x, width):
    return x * lax.rsqrt(jnp.sum(x * x, axis=-1, keepdims=True) * (1.0 / width) + NORM_EPS)


def _group_sumsq(x, g):
    x2 = x * x
    hi = _bf(x2)
    lo = _bf(x2 - hi.astype(F32))
    w = g.shape[0]
    outs = []
    for c in range(x.shape[1] // w):
        sl = slice(w * c, w * (c + 1))
        outs.append(_dot(hi[:, sl], g) + _dot(lo[:, sl], g))
    return outs[0] if len(outs) == 1 else jnp.concatenate(outs, axis=1)


def _group_rms(x, g, group):
    return x * lax.rsqrt(_group_sumsq(x, g) * (1.0 / group) + NORM_EPS)


def _rope(x, c, s_next, s_prev, half):
    n = x.shape[1]
    return x * c + pltpu.roll(x, n - half, axis=1) * s_next + pltpu.roll(x, half, axis=1) * s_prev


def _inproj_kernel(x_ref, gv_ref, w_ref, wuq_ref, wuk_ref, wuv_ref, rope_ref, g32_ref, g64_ref, g128_ref,
                   o_ref, o4_ref, o16_ref, stage):
    def gain(row, width):
        return gv_ref[row:row + 1, 0:width]

    x = x_ref[...]
    h = _rms(x, D_MODEL) * gain(G_NORM1, D_MODEL)
    proj = _dot(_bf(h), w_ref[...])

    g32 = g32_ref[...]
    g64 = g64_ref[...]
    g128 = g128_ref[...]
    m_c = rope_ref[:, 0:128]
    m_sn = rope_ref[:, 128:256]
    m_sp = rope_ref[:, 256:384]
    a_c = rope_ref[:, 384:512]
    a_sn = rope_ref[:, 512:640]
    a_sp = rope_ref[:, 640:768]

    cq = _rms(proj[:, P_CQ:P_CQ + MLA_Q_RANK], MLA_Q_RANK) * gain(G_MLA_QN, MLA_Q_RANK)
    ckv = _rms(proj[:, P_CKV:P_CKV + MLA_KV_RANK], MLA_KV_RANK) * gain(G_MLA_KVN, MLA_KV_RANK)
    q = _dot(_bf(cq), wuq_ref[...])
    k_nope = _dot(_bf(ckv), wuk_ref[...])
    v = _dot(_bf(ckv), wuv_ref[...])
    k_rope = pltpu.roll(proj[:, P_KR:P_KR + LANES], MLA_NOPE_DIM, axis=1)
    k = k_nope + jnp.concatenate([k_rope] * N_HEADS, axis=1)
    qn = _group_rms(q, g128, MLA_QK_DIM) * gain(G_MLA_Q, 512)
    kn = _group_rms(k, g128, MLA_QK_DIM) * gain(G_MLA_K, 512)
    for g in range(N_HEADS):
        sl = slice(LANES * g, LANES * (g + 1))
        o_ref[:, A_QA + LANES * g:A_QA + LANES * (g + 1)] = _bf(_rope(qn[:, sl], m_c, m_sn, m_sp, MLA_ROPE_DIM // 2))
        o_ref[:, A_KA + LANES * g:A_KA + LANES * (g + 1)] = _bf(_rope(kn[:, sl], m_c, m_sn, m_sp, MLA_ROPE_DIM // 2))
    o_ref[:, A_VA:A_VA + 256] = _bf(v)

    qb = _group_rms(proj[:, P_BQ:P_BQ + 256], g64, HEAD_DIM) * gain(G_DIL_Q, 256)
    kb = _group_rms(proj[:, P_BK:P_BK + 256], g64, HEAD_DIM) * gain(G_DIL_K, 256)
    vb = proj[:, P_BV:P_BV + 256]
    o_ref[:, A_QB:A_QB + 256] = _bf(qb)
    o_ref[:, A_KB:A_KB + 256] = _bf(kb)
    o_ref[:, A_VB:A_VB + 256] = _bf(vb)
    for c, val in enumerate((qb, kb, vb)):
        stage[2 * c] = val[:, 0:LANES]
        stage[2 * c + 1] = val[:, LANES:2 * LANES]
    n_chunks = DIL_COLS // LANES
    for d, ref in ((4, o4_ref), (16, o16_ref)):
        n = x.shape[0] // d
        for r in range(d):
            ref[0, r] = _bf(jnp.concatenate(
                [stage[c, pl.ds(r, n, stride=d), :] for c in range(n_chunks)], axis=1))

    qc = _group_rms(proj[:, P_CQ2:P_CQ2 + 256], g64, HEAD_DIM) * gain(G_GQA_Q, 256)
    for g in range(2):
        sl = slice(LANES * g, LANES * (g + 1))
        o_ref[:, A_QC + LANES * g:A_QC + LANES * (g + 1)] = _bf(_rope(qc[:, sl], a_c, a_sn, a_sp, HEAD_DIM // 4))
    kc = _group_rms(proj[:, P_CK2:P_CK2 + 128], g64[0:128, 0:128], HEAD_DIM) * gain(G_GQA_K, 128)
    o_ref[:, A_KC:A_KC + 128] = _bf(_rope(kc, a_c, a_sn, a_sp, HEAD_DIM // 4))
    o_ref[:, A_VC:A_VC + 128] = _bf(proj[:, P_CV2:P_CV2 + 128])

    dqn = _group_rms(proj[:, P_DQ:P_DQ + 256], g32, DIFF_QK_DIM)
    o_ref[:, A_QD0:A_QD0 + 256] = _bf(dqn * gain(G_DIFF_Q0, 256))
    o_ref[:, A_QD1:A_QD1 + 256] = _bf(dqn * gain(G_DIFF_Q1, 256))
    o_ref[:, A_KD:A_KD + 256] = _bf(_group_rms(proj[:, P_DK:P_DK + 256], g32, DIFF_QK_DIM) * gain(G_DIFF_K, 256))
    o_ref[:, A_VD:A_VD + 256] = _bf(proj[:, P_DV:P_DV + 256])


def _inproj(x2d, gv, w_in, wuq, wuk, wuv, rope, g32, g64, g128):
    t = x2d.shape[0]
    tm = TM_PROJ
    n_pos = SEQ // tm
    const = lambda i: (0, 0)
    return pl.pallas_call(
        _inproj_kernel,
        grid=(t // tm,),
        in_specs=[
            pl.BlockSpec((tm, D_MODEL), lambda i: (i, 0)),
            pl.BlockSpec((GAIN_ROWS, D_MODEL), const),
            pl.BlockSpec((D_MODEL, PROJ_COLS), const),
            pl.BlockSpec((MLA_Q_RANK, 512), const),
            pl.BlockSpec((MLA_KV_RANK, 512), const),
            pl.BlockSpec((MLA_KV_RANK, 256), const),
            pl.BlockSpec((tm, 768), lambda i: (i % n_pos, 0)),
            pl.BlockSpec((MXU_DIM, MXU_DIM), const),
            pl.BlockSpec((MXU_DIM, MXU_DIM), const),
            pl.BlockSpec((MXU_DIM, MXU_DIM), const),
        ],
        out_specs=[
            pl.BlockSpec((tm, ACT_COLS), lambda i: (i, 0)),
            pl.BlockSpec((1, 4, tm // 4, DIL_COLS), lambda i: (i // n_pos, 0, i % n_pos, 0)),
            pl.BlockSpec((1, 16, tm // 16, DIL_COLS), lambda i: (i // n_pos, 0, i % n_pos, 0)),
        ],
        out_shape=[
            jax.ShapeDtypeStruct((t, ACT_COLS), BF16),
            jax.ShapeDtypeStruct((t // SEQ, 4, SEQ // 4, DIL_COLS), BF16),
            jax.ShapeDtypeStruct((t // SEQ, 16, SEQ // 16, DIL_COLS), BF16),
        ],
        scratch_shapes=[pltpu.VMEM((DIL_COLS // LANES, tm, LANES), F32)],
        compiler_params=pltpu.CompilerParams(dimension_semantics=("parallel",), vmem_limit_bytes=VMEM_LIMIT_BYTES),
        name="inproj_prep",
    )(x2d, gv, w_in, wuq, wuk, wuv, rope, g32, g64, g128)


def _softmax_pv(s, v):
    m = jnp.max(s, axis=-1, keepdims=True)
    e = jnp.exp2(s - m)
    l = jnp.sum(e, axis=-1, keepdims=True)
    return _dot(_bf(e), v) / l


def _heads_one_ahead(scores_of, finish):
    pending = scores_of(0)
    for h in range(N_HEADS):
        s = pending
        if h + 1 < N_HEADS:
            pending = scores_of(h + 1)
        finish(h, s)


def _attn_mla_kernel(q_ref, k_ref, v_ref, o_ref):
    def scores_of(h):
        sl = slice(LANES * h, LANES * (h + 1))
        return _dot_nt(q_ref[:, sl], k_ref[:, sl])

    def finish(h, s):
        hs = slice(HEAD_DIM * h, HEAD_DIM * (h + 1))
        o_ref[:, hs] = _softmax_pv(s, v_ref[:, hs])
    _heads_one_ahead(scores_of, finish)


def _attn_gqa_kernel(q_ref, k_ref, v_ref, o_ref):
    group = lambda h: slice(HEAD_DIM * (h // 2), HEAD_DIM * (h // 2 + 1))

    def scores_of(h):
        return _dot_nt(q_ref[:, HEAD_DIM * h:HEAD_DIM * (h + 1)], k_ref[:, group(h)])

    def finish(h, s):
        o_ref[:, HEAD_DIM * h:HEAD_DIM * (h + 1)] = _softmax_pv(s, v_ref[:, group(h)])
    _heads_one_ahead(scores_of, finish)


def _attn_full(act, batch, kernel, q_col, q_w, k_col, k_w, v_col, v_w, name):
    t = act.shape[0]
    tq = TQ_FULL
    nq = SEQ // tq
    return pl.pallas_call(
        kernel,
        grid=(batch, nq),
        in_specs=[
            pl.BlockSpec((tq, q_w), lambda b, i: (b * nq + i, q_col // q_w)),
            pl.BlockSpec((SEQ, k_w), lambda b, i: (b, k_col // k_w)),
            pl.BlockSpec((SEQ, v_w), lambda b, i: (b, v_col // v_w)),
        ],
        out_specs=pl.BlockSpec((tq, 256), lambda b, i: (b * nq + i, 0)),
        out_shape=jax.ShapeDtypeStruct((t, 256), F32),
        compiler_params=pltpu.CompilerParams(
            dimension_semantics=("parallel", "parallel"), vmem_limit_bytes=VMEM_LIMIT_BYTES),
        name=name,
    )(act, act, act)


def _attn_diff_kernel(lam_ref, q0_ref, q1_ref, k_ref, v_ref, win_ref, sg_ref, o_ref, bias_ref):
    @pl.when(pl.program_id(1) == 0)
    def _():
        for h in range(N_HEADS):
            w = jnp.broadcast_to(win_ref[0, h:h + 1, :], (TQ, SEQ + TQ))
            bias_ref[h] = pltpu.roll(w, 0, axis=1, stride=1, stride_axis=0)[:, TQ:TQ + SEQ]

    lam = lam_ref[0]

    def scores_of(h):
        hs = slice(HEAD_DIM * h, HEAD_DIM * (h + 1))
        k = k_ref[:, hs]
        return _dot_nt(q0_ref[:, hs], k), _dot_nt(q1_ref[:, hs], k)

    def finish(h, scores):
        hs = slice(HEAD_DIM * h, HEAD_DIM * (h + 1))
        bias = bias_ref[h]
        s0, s1 = scores[0] + bias, scores[1] + bias
        e0 = jnp.exp2(s0 - jnp.max(s0, axis=-1, keepdims=True))
        e1 = jnp.exp2(s1 - jnp.max(s1, axis=-1, keepdims=True))
        r0 = 1.0 / jnp.sum(e0, axis=-1, keepdims=True)
        r1 = lam / jnp.sum(e1, axis=-1, keepdims=True)
        v = v_ref[:, hs]
        o = r0 * _dot(_bf(e0), v) - r1 * _dot(_bf(e1), v)
        o_ref[:, hs] = _rms(o, HEAD_DIM) * sg_ref[:, hs]
    _heads_one_ahead(scores_of, finish)


def _attn_diff(act, batch, lam, bias_win, sub_gain):
    t = act.shape[0]
    nq = SEQ // TQ
    return pl.pallas_call(
        _attn_diff_kernel,
        grid=(nq, batch),
        in_specs=[
            pl.BlockSpec(memory_space=pltpu.SMEM),
            pl.BlockSpec((TQ, 256), lambda i, b: (b * nq + i, A_QD0 // 256)),
            pl.BlockSpec((TQ, 256), lambda i, b: (b * nq + i, A_QD1 // 256)),
            pl.BlockSpec((SEQ, 256), lambda i, b: (b, A_KD // 256)),
            pl.BlockSpec((SEQ, 256), lambda i, b: (b, A_VD // 256)),
            pl.BlockSpec((1, N_HEADS, SEQ + TQ), lambda i, b: (i, 0, 0)),
            pl.BlockSpec((1, 256), lambda i, b: (0, 0)),
        ],
        out_specs=pl.BlockSpec((TQ, 256), lambda i, b: (b * nq + i, 0)),
        out_shape=jax.ShapeDtypeStruct((t, 256), F32),
        scratch_shapes=[pltpu.VMEM((N_HEADS, TQ, SEQ), F32)],
        compiler_params=pltpu.CompilerParams(
            dimension_semantics=("arbitrary", "arbitrary"), vmem_limit_bytes=VMEM_LIMIT_BYTES),
        name="attn_diff",
    )(lam, act, act, act, act, bias_win, sub_gain)


def _dil_branch(q_of, k_of, v_of, bias_of, n_seq):
    outs, lses = [], []
    for h in range(N_HEADS):
        hs = slice(HEAD_DIM * h, HEAD_DIM * (h + 1))
        q, k, v = q_of(hs), k_of(hs), v_of(hs)
        n = q.shape[1]
        if n == DIL_QBLK:
            kw, vw = k, v
            bias = bias_of(h)[:, DIL_HALF:DIL_HALF + DIL_QBLK]
            edge = None
        else:
            n_blk = n // DIL_QBLK

            def windows(x):
                zeros = jnp.zeros((n_seq, DIL_HALF, HEAD_DIM), BF16)
                xp = jnp.concatenate([zeros, x, zeros], axis=1)
                first = xp[:, 0:n].reshape(n_seq * n_blk, DIL_QBLK, HEAD_DIM)
                second = xp[:, DIL_QBLK:DIL_QBLK + n].reshape(n_seq * n_blk, DIL_QBLK, HEAD_DIM)
                return jnp.concatenate([first, second], axis=1)
            kw, vw = windows(k), windows(v)
            q = q.reshape(n_seq * n_blk, DIL_QBLK, HEAD_DIM)
            bias = bias_of(h)
            blk = lax.broadcasted_iota(jnp.int32, (n_seq * n_blk, 1, 2 * DIL_QBLK), 0) % n_blk
            col = lax.broadcasted_iota(jnp.int32, (n_seq * n_blk, 1, 2 * DIL_QBLK), 2)
            key_pos = col - DIL_HALF + blk * DIL_QBLK
            edge = jnp.where(jnp.logical_and(key_pos >= 0, key_pos < n), 0.0, NEG_INF)
        s = jnp.einsum("bqe,bke->bqk", q, kw, preferred_element_type=F32) + bias[None]
        if edge is not None:
            s = s + edge
        m = jnp.max(s, axis=-1, keepdims=True)
        e = jnp.exp2(s - m)
        l = jnp.sum(e, axis=-1, keepdims=True)
        o = jnp.einsum("bqk,bke->bqe", _bf(e), vw, preferred_element_type=F32) / l
        lse = jnp.broadcast_to(m + jnp.log2(l), o.shape)
        outs.append(o.reshape(n_seq, n, HEAD_DIM))
        lses.append(lse.reshape(n_seq, n, HEAD_DIM))
    cat = lambda xs: [jnp.concatenate(xs[2 * c:2 * c + 2], axis=2) for c in range(2)]
    return cat(outs), cat(lses)


def _dil_kernel(n1_ref, r4_ref, r16_ref, bias_ref, o_ref, o1, l1, o4, l4, out):
    col_of = lambda base: (lambda hs: slice(base + hs.start, base + hs.stop))
    q_cols, k_cols, v_cols = col_of(0), col_of(256), col_of(512)

    oc, lc = _dil_branch(lambda hs: n1_ref[:, q_cols(hs)][None], lambda hs: n1_ref[:, k_cols(hs)][None],
                         lambda hs: n1_ref[:, v_cols(hs)][None], lambda h: bias_ref[0, h], 1)
    for c in range(2):
        o1[c] = oc[c][0]
        l1[c] = lc[c][0]

    oc, lc = _dil_branch(lambda hs: r4_ref[0, :, :, q_cols(hs)], lambda hs: r4_ref[0, :, :, k_cols(hs)],
                         lambda hs: r4_ref[0, :, :, v_cols(hs)], lambda h: bias_ref[1, h], 4)
    for c in range(2):
        for r in range(4):
            tok = pl.ds(r, SEQ // 4, stride=4)
            o4[c, tok, :] = oc[c][r]
            l4[c, tok, :] = lc[c][r]

    oc, lc = _dil_branch(lambda hs: r16_ref[0, :, :, q_cols(hs)], lambda hs: r16_ref[0, :, :, k_cols(hs)],
                         lambda hs: r16_ref[0, :, :, v_cols(hs)], lambda h: bias_ref[2, h], 16)
    toks = [pl.ds(r, SEQ // 16, stride=16) for r in range(16)]
    for c in range(2):
        gather = lambda ref: jnp.stack([ref[c, tok, :] for tok in toks])
        la, lb = gather(l1), gather(l4)
        mx = jnp.maximum(jnp.maximum(la, lb), lc[c])
        wa, wb, wc = jnp.exp2(la - mx), jnp.exp2(lb - mx), jnp.exp2(lc[c] - mx)
        den = wa + wb + wc
        res = (wa / den) * gather(o1) + (wb / den) * gather(o4) + (wc / den) * oc[c]
        for r, tok in enumerate(toks):
            out[c, tok, :] = res[r]
    o_ref[:, 0:LANES] = out[0]
    o_ref[:, LANES:2 * LANES] = out[1]


def _attn_dilated(act, act4, act16, batch, bias_tabs):
    t = act.shape[0]
    nat = pltpu.VMEM((2, SEQ, LANES), F32)
    return pl.pallas_call(
        _dil_kernel,
        grid=(batch,),
        in_specs=[
            pl.BlockSpec((SEQ, DIL_COLS), lambda b: (b, 0)),
            pl.BlockSpec((1, 4, SEQ // 4, DIL_COLS), lambda b: (b, 0, 0, 0)),
            pl.BlockSpec((1, 16, SEQ // 16, DIL_COLS), lambda b: (b, 0, 0, 0)),
            pl.BlockSpec((3, N_HEADS, DIL_QBLK, 2 * DIL_QBLK), lambda b: (0, 0, 0, 0)),
        ],
        out_specs=pl.BlockSpec((SEQ, 256), lambda b: (b, 0)),
        out_shape=jax.ShapeDtypeStruct((t, 256), F32),
        scratch_shapes=[nat, nat, nat, nat, nat],
        compiler_params=pltpu.CompilerParams(dimension_semantics=("parallel",), vmem_limit_bytes=VMEM_LIMIT_BYTES),
        name="attn_dilated",
    )(act, act4, act16, bias_tabs)


def _route(lt):
    g = [lt[i:i + 1, :] for i in range(4)]
    gmax = jnp.maximum(jnp.maximum(g[0], g[1]), jnp.maximum(g[2], g[3]))
    gsum = sum(jnp.exp(gi - gmax) for gi in g)
    g_w = 1.0 / gsum
    gidx = jnp.where(g[0] == gmax, 0, jnp.where(g[1] == gmax, 1, jnp.where(g[2] == gmax, 2, 3)))
    el = []
    for j in range(EXPERTS_PER_GROUP):
        acc = jnp.zeros_like(g[0])
        for i in range(4):
            r = 4 + EXPERTS_PER_GROUP * i + j
            acc = jnp.where(gidx == i, lt[r:r + 1, :], acc)
        el.append(acc)
    emax = jnp.maximum(jnp.maximum(el[0], el[1]), jnp.maximum(el[2], el[3]))
    ee = [jnp.exp(e - emax) for e in el]
    esum = ee[0] + ee[1] + ee[2] + ee[3]
    p = [e / esum for e in ee]
    p1 = jnp.maximum(jnp.maximum(p[0], p[1]), jnp.maximum(p[2], p[3]))
    i1 = jnp.where(p[0] == p1, 0, jnp.where(p[1] == p1, 1, jnp.where(p[2] == p1, 2, 3)))
    pm = [jnp.where(i1 == j, -1.0, p[j]) for j in range(4)]
    p2 = jnp.maximum(jnp.maximum(pm[0], pm[1]), jnp.maximum(pm[2], pm[3]))
    i2 = jnp.where(pm[0] == p2, 0, jnp.where(pm[1] == p2, 1, jnp.where(pm[2] == p2, 2, 3)))
    den = p1 + p2
    return gidx, i1, i2, g_w * (p1 / den), g_w * (p2 / den)


def _pair_offset(lo):
    return jnp.where(lo == 0, 0, jnp.where(lo == 1, 3, 5))


def _outproj_kernel(ya_ref, yb_ref, yc_ref, yd_ref, x_ref, cv_ref, w_ref, wrh_ref, wrl_ref, rb_ref, tri_ref,
                    x1_ref, hp_ref, ri_ref, cnt_out_ref, cnt_ref):
    tm = x_ref.shape[0]

    @pl.when(pl.program_id(0) == 0)
    def _():
        cnt_ref[...] = jnp.zeros_like(cnt_ref)

    mixed = jnp.concatenate(
        [_rms(ya_ref[...], 256), _rms(yb_ref[...], 256), _rms(yc_ref[...], 256), yd_ref[...]], axis=1)
    mixed = mixed * cv_ref[0:1, :]
    x1 = x_ref[...] + _dot(_bf(mixed), w_ref[...])
    x1_ref[...] = x1
    h2 = _rms(x1, D_MODEL) * cv_ref[1:2, :]
    h_hi = _bf(h2)
    h_rt = h_hi.astype(F32)
    h_lo = _bf(h2 - h_rt)
    logits = _dot(h_hi, wrh_ref[...]) + _dot(h_hi, wrl_ref[...]) + _dot(h_lo, wrh_ref[...]) + rb_ref[...]
    gidx, i1, i2, wa, wb = _route(logits.T)

    lo, hi = jnp.minimum(i1, i2), jnp.maximum(i1, i2)
    first_is_lo = i1 < i2
    w_lo, w_hi = jnp.where(first_is_lo, wa, wb), jnp.where(first_is_lo, wb, wa)
    cls = gidx * PAIRS_PER_GROUP + _pair_offset(lo) + (hi - lo - 1)

    rows = lax.broadcasted_iota(jnp.int32, (CLS_ROWS, tm), 0)
    onehot = rows == cls
    oh = jnp.where(onehot, 1.0, 0.0)
    prefix = _dot(_bf(oh), tri_ref[...])
    before = cnt_ref[:, 0:1] + prefix
    rank = jnp.sum(jnp.where(onehot, before, 0.0), axis=0, keepdims=True)
    cnt_ref[...] = cnt_ref[...] + jnp.sum(oh, axis=1, keepdims=True)
    cnt_out_ref[...] = cnt_ref[...]
    ri_ref[...] = jnp.concatenate([cls, rank.astype(jnp.int32), jnp.zeros((6, tm), jnp.int32)], axis=0)

    hp_ref[:, 0:D_MODEL] = h_rt
    rows_w = lax.broadcasted_iota(jnp.int32, (LANES, tm), 0)
    w_t = jnp.where(rows_w == 0, w_lo, jnp.where(rows_w == 1, w_hi, 0.0))
    hp_ref[:, D_MODEL:XS_COLS] = w_t.T


def _outproj(ya, yb, yc, yd, x2d, cv, w_out, wr_hi, wr_lo, rb, tri):
    t = x2d.shape[0]
    tm = TM_OUT
    row = lambda w: pl.BlockSpec((tm, w), lambda i: (i, 0))
    const = lambda shape: pl.BlockSpec(shape, lambda i: (0, 0))
    return pl.pallas_call(
        _outproj_kernel,
        grid=(t // tm,),
        in_specs=[row(256), row(256), row(256), row(256), row(D_MODEL), const((8, D_MODEL)),
                  const((D_MODEL, D_MODEL)), const((D_MODEL, LANES)), const((D_MODEL, LANES)), const((1, LANES)),
                  const((tm, tm))],
        out_specs=[row(D_MODEL), row(XS_COLS), pl.BlockSpec((8, tm), lambda i: (0, i)), const((CLS_ROWS, LANES))],
        out_shape=[jax.ShapeDtypeStruct((t, D_MODEL), F32), jax.ShapeDtypeStruct((t, XS_COLS), F32),
                   jax.ShapeDtypeStruct((8, t), jnp.int32), jax.ShapeDtypeStruct((CLS_ROWS, LANES), F32)],
        scratch_shapes=[pltpu.VMEM((CLS_ROWS, LANES), F32)],
        compiler_params=pltpu.CompilerParams(dimension_semantics=("arbitrary",), vmem_limit_bytes=VMEM_LIMIT_BYTES),
        name="outproj_router",
    )(ya, yb, yc, yd, x2d, cv, w_out, wr_hi, wr_lo, rb, tri)


def _moe_plan(ri, counts, n_tiles):
    cls, rank = ri[0], ri[1]
    cnt = counts[:N_CLASSES, 0].astype(jnp.int32)
    padded = ((cnt + TM_MOE - 1) // TM_MOE) * TM_MOE
    ends = jnp.cumsum(padded)
    offs = ends - padded
    classes = jnp.arange(N_CLASSES, dtype=jnp.int32)
    dest = jnp.sum(jnp.where(cls[:, None] == classes[None, :], offs[None, :], 0), axis=1) + rank
    tile_start = jnp.arange(n_tiles, dtype=jnp.int32) * TM_MOE
    tile_cls = jnp.sum((tile_start[:, None] >= ends[None, :]).astype(jnp.int32), axis=1)
    tile_cls = jnp.minimum(tile_cls, N_CLASSES - 1)
    group, pair = tile_cls // PAIRS_PER_GROUP, tile_cls % PAIRS_PER_GROUP
    lo = (pair >= 3).astype(jnp.int32) + (pair >= 5).astype(jnp.int32)
    hi = pair - _pair_offset(lo) + lo + 1
    n_active = (ends[-1] // TM_MOE).reshape(1)
    return dest, group * EXPERTS_PER_GROUP + lo, group * EXPERTS_PER_GROUP + hi, n_active


def _dispatch_kernel(dest_ref, h_ref, xs_in_ref, xs_ref, stage, sems):
    del xs_in_ref
    tm = h_ref.shape[0]
    i = pl.program_id(0)
    slot = i % 2

    def wait_slot(s):
        pltpu.make_async_copy(stage.at[s], xs_ref.at[pl.ds(0, tm), :], sems.at[s]).wait()

    @pl.when(i >= 2)
    def _():
        wait_slot(slot)

    stage[slot] = h_ref[...]
    for r in range(tm):
        pltpu.make_async_copy(stage.at[slot, pl.ds(r, 1), :], xs_ref.at[pl.ds(dest_ref[0, 0, r], 1), :],
                              sems.at[slot]).start()

    @pl.when(i == pl.num_programs(0) - 1)
    def _():
        wait_slot(1 - slot)
        wait_slot(slot)


def _dispatch(dest, hp, n_rows):
    t = hp.shape[0]
    tm = TM_ROWS_DMA
    assert t // tm >= 2
    return pl.pallas_call(
        _dispatch_kernel,
        grid=(t // tm,),
        in_specs=[
            pl.BlockSpec((1, 1, tm), lambda i: (i, 0, 0), memory_space=pltpu.SMEM),
            pl.BlockSpec((tm, XS_COLS), lambda i: (i, 0)),
            pl.BlockSpec(memory_space=pl.ANY),
        ],
        out_specs=pl.BlockSpec(memory_space=pl.ANY),
        out_shape=jax.ShapeDtypeStruct((n_rows, XS_COLS), F32),
        scratch_shapes=[pltpu.VMEM((2, tm, XS_COLS), F32), pltpu.SemaphoreType.DMA((2,))],
        input_output_aliases={2: 0},
        compiler_params=pltpu.CompilerParams(dimension_semantics=("arbitrary",)),
        name="moe_dispatch",
    )(dest.reshape(t // tm, 1, tm), hp, jnp.zeros((n_rows, XS_COLS), F32))


def _moe_group_kernel(elo_ref, ehi_ref, nact_ref, xs_ref, wg0, wu0, wd0, wg1, wu1, wd1, o_ref, wg_s, wu_s, wd_s):
    i = pl.program_id(0)
    active = i < nact_ref[0]
    prev = jnp.maximum(i - 1, 0)
    changed = jnp.logical_or(i == 0, jnp.logical_or(elo_ref[i] != elo_ref[prev], ehi_ref[i] != ehi_ref[prev]))

    @pl.when(jnp.logical_and(active, changed))
    def _():
        for slot, (wg, wu, wd) in enumerate(((wg0, wu0, wd0), (wg1, wu1, wd1))):
            wg_s[slot] = _bf(wg[0, 0])
            wu_s[slot] = _bf(wu[0, 0])
            wd_s[slot] = _bf(wd[0, 0])

    @pl.when(active)
    def _():
        x = _bf(xs_ref[:, 0:D_MODEL])
        gate = xs_ref[:, D_MODEL:XS_COLS]

        def expert(slot):
            a = _dot(x, wg_s[slot])
            hid = a * (1.0 / (1.0 + jnp.exp(-a))) * _dot(x, wu_s[slot])
            return _dot(_bf(hid), wd_s[slot])
        o_ref[...] = gate[:, 0:1] * expert(0) + gate[:, 1:2] * expert(1)

    @pl.when(jnp.logical_not(active))
    def _():
        o_ref[...] = jnp.zeros_like(o_ref)


def _moe_group(layer, e_lo, e_hi, n_active, xs, wg, wu, wd):
    n_rows = xs.shape[0]
    tm = TM_MOE
    w_in = lambda which: pl.BlockSpec((1, 1, D_MODEL, D_FF), lambda i, lo, hi, na: (layer, (lo, hi)[which][i], 0, 0))
    w_dn = lambda which: pl.BlockSpec((1, 1, D_FF, D_MODEL), lambda i, lo, hi, na: (layer, (lo, hi)[which][i], 0, 0))
    return pl.pallas_call(
        _moe_group_kernel,
        grid_spec=pltpu.PrefetchScalarGridSpec(
            num_scalar_prefetch=3,
            grid=(n_rows // tm,),
            in_specs=[pl.BlockSpec((tm, XS_COLS), lambda i, lo, hi, na: (i, 0)),
                      w_in(0), w_in(0), w_dn(0), w_in(1), w_in(1), w_dn(1)],
            out_specs=pl.BlockSpec((tm, D_MODEL), lambda i, lo, hi, na: (i, 0)),
            scratch_shapes=[pltpu.VMEM((2, D_MODEL, D_FF), BF16), pltpu.VMEM((2, D_MODEL, D_FF), BF16),
                            pltpu.VMEM((2, D_FF, D_MODEL), BF16)],
        ),
        out_shape=jax.ShapeDtypeStruct((n_rows, D_MODEL), F32),
        compiler_params=pltpu.CompilerParams(dimension_semantics=("arbitrary",), vmem_limit_bytes=VMEM_LIMIT_BYTES),
        name="moe_group",
    )(e_lo, e_hi, n_active, xs, wg, wu, wd, wg, wu, wd)


def _combine_kernel(dest_ref, dest_next_ref, x1_ref, y_ref, o_ref, buf, sems):
    tm = x1_ref.shape[0]
    i = pl.program_id(0)
    slot = i % 2

    def gather(idx_ref, s):
        for r in range(tm):
            pltpu.make_async_copy(y_ref.at[pl.ds(idx_ref[0, 0, r], 1), :], buf.at[s, pl.ds(r, 1), :],
                                  sems.at[s]).start()

    @pl.when(i == 0)
    def _():
        gather(dest_ref, 0)

    @pl.when(i + 1 < pl.num_programs(0))
    def _():
        gather(dest_next_ref, 1 - slot)

    pltpu.make_async_copy(y_ref.at[pl.ds(0, tm), :], buf.at[slot], sems.at[slot]).wait()
    o_ref[...] = x1_ref[...] + buf[slot]


def _combine(dest, x1, y):
    t = x1.shape[0]
    tm = TM_ROWS_DMA
    n = t // tm
    dest3 = dest.reshape(n, 1, tm)
    return pl.pallas_call(
        _combine_kernel,
        grid=(n,),
        in_specs=[
            pl.BlockSpec((1, 1, tm), lambda i: (i, 0, 0), memory_space=pltpu.SMEM),
            pl.BlockSpec((1, 1, tm), lambda i: (jnp.minimum(i + 1, n - 1), 0, 0), memory_space=pltpu.SMEM),
            pl.BlockSpec((tm, D_MODEL), lambda i: (i, 0)),
            pl.BlockSpec(memory_space=pl.ANY),
        ],
        out_specs=pl.BlockSpec((tm, D_MODEL), lambda i: (i, 0)),
        out_shape=jax.ShapeDtypeStruct((t, D_MODEL), F32),
        scratch_shapes=[pltpu.VMEM((2, tm, D_MODEL), F32), pltpu.SemaphoreType.DMA((2,))],
        compiler_params=pltpu.CompilerParams(dimension_semantics=("arbitrary",)),
        name="moe_combine",
    )(dest3, dest3, x1, y)


def _rope_angles(pos, dim):
    inv = 1.0 / (ROPE_THETA ** (jnp.arange(0, dim, 2, dtype=F32) / dim))
    return pos.astype(F32)[:, None] * inv[None, :]


def _rope_tables():
    pos = jnp.arange(SEQ, dtype=jnp.int32)
    z = lambda w: jnp.zeros((SEQ, w), F32)
    ang = _rope_angles(pos, MLA_ROPE_DIM)
    c, s = jnp.cos(ang), jnp.sin(ang)
    m_c = jnp.concatenate([jnp.ones((SEQ, 64), F32), c, c, z(32)], axis=1)
    m_sn = jnp.concatenate([z(64), -s, z(48)], axis=1)
    m_sp = jnp.concatenate([z(80), s, z(32)], axis=1)
    row_pos = pos // GRID_W
    col_pos = pos - row_pos * GRID_W
    ra, ca = _rope_angles(row_pos, HEAD_DIM // 2), _rope_angles(col_pos, HEAD_DIM // 2)
    rc, rs, cc, cs = jnp.cos(ra), jnp.sin(ra), jnp.cos(ca), jnp.sin(ca)
    a_c = jnp.concatenate([rc, rc, cc, cc] * 2, axis=1)
    a_sn = jnp.concatenate([-rs, z(16), -cs, z(16)] * 2, axis=1)
    a_sp = jnp.concatenate([z(16), rs, z(16), cs] * 2, axis=1)
    return jnp.concatenate([m_c, m_sn, m_sp, a_c, a_sn, a_sp], axis=1)


def _rel_bucket(rel):
    half = NUM_BUCKETS // 2
    max_exact = half // 2
    n = jnp.abs(rel)
    nf = jnp.maximum(n, 1).astype(F32)
    log_ratio = jnp.log(nf / max_exact) / math.log(REL_MAX_DISTANCE / max_exact)
    large = jnp.minimum(max_exact + (log_ratio * (half - max_exact)).astype(jnp.int32), half - 1)
    return jnp.where(rel > 0, half, 0) + jnp.where(n < max_exact, n, large)


def _bias_line(table, rel):
    return table[_rel_bucket(rel)].T


def _toeplitz(line, rows, cols, first):
    n = line.shape[-1]
    padded = jnp.pad(line, [(0, 0)] * (line.ndim - 1) + [(0, 1)])
    flat = jnp.tile(padded, rows)[..., :rows * n]
    skew = flat.reshape(line.shape[:-1] + (rows, n))
    return skew[..., first:first + cols]


def _dil_bias(table):
    reach = DIL_QBLK + DIL_HALF - 1
    steps = jnp.arange(-reach, reach + 1, dtype=jnp.int32)
    band = jnp.abs(steps) <= DIL_HALF
    lines = jnp.stack([jnp.where(band[None], _bias_line(table, steps * d), NEG_INF) for _, d in DIL_PATTERNS])
    return _toeplitz(lines, DIL_QBLK, 2 * DIL_QBLK, DIL_QBLK - 1)


def _diff_bias_windows(table):
    rel = jnp.arange(2 * SEQ, dtype=jnp.int32) - SEQ
    line = _bias_line(table, rel)
    starts = [SEQ - (i + 1) * TQ for i in range(SEQ // TQ)]
    return jnp.stack([line[:, s0:s0 + SEQ + TQ] for s0 in starts])


def _block_ones(group):
    idx = np.arange(MXU_DIM) // group
    return jnp.asarray(idx[:, None] == idx[None, :], dtype=BF16)


def _pad_row(v):
    return jnp.pad(v.astype(F32), (0, D_MODEL - v.shape[0]))


def _layer_params(layer, norm1_g, w_in, mla_q_norm_g, mla_kv_norm_g, mla_w_uq, mla_w_ukv, mla_qk_g, dil_qk_g,
                  gqa_qk_g, diff_qk_g, diff_lambda, diff_subln_g, mix_beta, w_out, norm2_g, router_group_w,
                  router_group_b, router_expert_w, router_expert_b):
    w = w_in[layer]
    w_p = _bf(jnp.concatenate([w[:, :416], jnp.zeros((D_MODEL, 96), F32), w[:, 416:]], axis=1))
    uq = mla_w_uq[layer].reshape(MLA_Q_RANK, N_HEADS, MLA_QK_DIM)
    wuq = _bf(jnp.pad(uq, ((0, 0), (0, 0), (0, LANES - MLA_QK_DIM))).reshape(MLA_Q_RANK, 512))
    ukv = mla_w_ukv[layer].reshape(MLA_KV_RANK, N_HEADS, 2 * MLA_NOPE_DIM)
    wuk = _bf(jnp.pad(ukv[:, :, :MLA_NOPE_DIM], ((0, 0), (0, 0), (0, LANES - MLA_NOPE_DIM))).reshape(MLA_KV_RANK, 512))
    wuv = _bf(ukv[:, :, MLA_NOPE_DIM:].reshape(MLA_KV_RANK, 256))

    pad96 = lambda g: jnp.tile(jnp.pad(g, (0, LANES - MLA_QK_DIM)), N_HEADS)
    m0 = jnp.tile(jnp.concatenate([jnp.ones(32, F32), jnp.zeros(32, F32)]), N_HEADS)
    dq = jnp.tile(diff_qk_g[layer, 0], 2 * N_HEADS) * (DIFF_QK_DIM ** -0.5 * LOG2E)
    rows = [
        norm1_g[layer], mla_q_norm_g[layer], mla_kv_norm_g[layer],
        pad96(mla_qk_g[layer, 0]) * (MLA_QK_DIM ** -0.5 * LOG2E), pad96(mla_qk_g[layer, 1]),
        jnp.tile(dil_qk_g[layer, 0], N_HEADS) * (HEAD_DIM ** -0.5 * LOG2E), jnp.tile(dil_qk_g[layer, 1], N_HEADS),
        jnp.tile(gqa_qk_g[layer, 0], N_HEADS) * (HEAD_DIM ** -0.5 * LOG2E), jnp.tile(gqa_qk_g[layer, 1], 2),
        dq * m0, dq * (1.0 - m0), jnp.tile(diff_qk_g[layer, 1], 2 * N_HEADS),
    ]
    gv = jnp.stack([_pad_row(r) for r in rows] + [jnp.zeros(D_MODEL, F32)] * (GAIN_ROWS - len(rows)))

    lambda_init = 0.8 - 0.6 * math.exp(-0.3 * layer)
    lv = diff_lambda[layer].astype(F32)
    lam = (jnp.exp(jnp.sum(lv[0] * lv[1])) - jnp.exp(jnp.sum(lv[2] * lv[3])) + lambda_init).reshape(1)
    sub_gain = (jnp.tile(diff_subln_g[layer], N_HEADS) * (1.0 - lambda_init)).reshape(1, 256)

    cv = jnp.stack([mix_beta[layer], norm2_g[layer]] + [jnp.zeros(D_MODEL, F32)] * 6)
    wr = jnp.concatenate([router_group_w[layer], router_expert_w[layer],
                          jnp.zeros((D_MODEL, LANES - 4 - N_EXPERTS), F32)], axis=1)
    wr_hi = _bf(wr)
    wr_lo = _bf(wr - wr_hi.astype(F32))
    rb = jnp.concatenate([router_group_b[layer], router_expert_b[layer],
                          jnp.zeros(LANES - 4 - N_EXPERTS, F32)]).reshape(1, LANES)
    return dict(w_in=w_p, wuq=wuq, wuk=wuk, wuv=wuv, gv=gv, lam=lam, sub_gain=sub_gain, cv=cv,
                w_out=_bf(w_out[layer]), wr_hi=wr_hi, wr_lo=wr_lo, rb=rb)


def kernel(x, rel_bias, norm1_g, w_in, mla_q_norm_g, mla_kv_norm_g, mla_w_uq, mla_w_ukv, mla_qk_g, dil_qk_g, gqa_qk_g, diff_qk_g, diff_lambda, diff_subln_g, mix_beta, w_out, norm2_g, router_group_w, router_group_b, router_expert_w, router_expert_b, expert_w_gate, expert_w_up, expert_w_down):
    batch, seq, d_model = x.shape
    assert seq == SEQ and d_model == D_MODEL
    depth = w_in.shape[0]
    t = batch * seq

    rope = _rope_tables()
    g32, g64, g128 = _block_ones(32), _block_ones(64), _block_ones(128)
    tri = jnp.asarray(np.arange(TM_OUT)[:, None] < np.arange(TM_OUT)[None, :], dtype=BF16)
    dil_bias = _dil_bias(rel_bias[:, :N_HEADS] * LOG2E)
    diff_bias = _diff_bias_windows(rel_bias[:, N_HEADS:] * LOG2E)

    x2d = x.reshape(t, D_MODEL)
    for layer in range(depth):
        p = _layer_params(layer, norm1_g, w_in, mla_q_norm_g, mla_kv_norm_g, mla_w_uq, mla_w_ukv, mla_qk_g,
                          dil_qk_g, gqa_qk_g, diff_qk_g, diff_lambda, diff_subln_g, mix_beta, w_out, norm2_g,
                          router_group_w, router_group_b, router_expert_w, router_expert_b)
        act, act4, act16 = _inproj(x2d, p["gv"], p["w_in"], p["wuq"], p["wuk"], p["wuv"], rope, g32, g64, g128)
        ya = _attn_full(act, batch, _attn_mla_kernel, A_QA, 512, A_KA, 512, A_VA, 256, "attn_mla")
        yb = _attn_dilated(act, act4, act16, batch, dil_bias)
        yc = _attn_full(act, batch, _attn_gqa_kernel, A_QC, 256, A_KC, 128, A_VC, 128, "attn_gqa")
        yd = _attn_diff(act, batch, p["lam"], diff_bias, p["sub_gain"])
        x1, hp, ri, counts = _outproj(ya, yb, yc, yd, x2d, p["cv"], p["w_out"], p["wr_hi"], p["wr_lo"], p["rb"], tri)
        n_tiles = t // TM_MOE + N_CLASSES
        dest, e_lo, e_hi, n_active = _moe_plan(ri, counts, n_tiles)
        xs = _dispatch(dest, hp, n_tiles * TM_MOE)
        y = _moe_group(layer, e_lo, e_hi, n_active, xs, expert_w_gate, expert_w_up, expert_w_down)
        x2d = _combine(dest, x1, y)
    return x2d.reshape(batch, seq, D_MODEL)
```

```python
import functools
import math

import jax
import jax.numpy as jnp
import numpy as np
from jax import lax
from jax.experimental import pallas as pl
from jax.experimental.pallas import tpu as pltpu

F32 = jnp.float32
BF16 = jnp.bfloat16

D_MODEL = 1024
SEQ = 2048
HEAD_DIM = 64
GRID_W = 64
ROPE_THETA = 10000.0
LOG2E = 1.0 / math.log(2.0)
NORM_EPS = 1e-6
NEG_INF = -1e30
NUM_BUCKETS = 32
REL_MAX_DISTANCE = 1024

N_HEADS = 4
MLA_NOPE_DIM = 64
MLA_ROPE_DIM = 32
MLA_QK_DIM = MLA_NOPE_DIM + MLA_ROPE_DIM
MLA_Q_RANK = 256
MLA_KV_RANK = 128
DIL_PATTERNS = ((128, 1), (512, 4), (2048, 16))
DIL_HALF = 64
DIFF_QK_DIM = 32
N_EXPERTS = 16
EXPERTS_PER_GROUP = 4
D_FF = 512

LANES = 128
MXU_DIM = 256
VMEM_LIMIT_BYTES = 56 * 1024 * 1024

P_CQ, P_CKV, P_KR = 0, 256, 384
P_BQ, P_BK, P_BV = 512, 768, 1024
P_CQ2, P_CK2, P_CV2 = 1280, 1536, 1664
P_DQ, P_DK, P_DV = 1792, 2048, 2304
PROJ_COLS = 2560

A_QB, A_KB, A_VB = 0, 256, 512
A_VA, A_QA, A_KA = 768, 1024, 1536
A_QC, A_KC, A_VC = 2048, 2304, 2432
A_QD0, A_QD1, A_KD, A_VD = 2560, 2816, 3072, 3328
ACT_COLS = 3584
DIL_COLS = 768

(G_NORM1, G_MLA_QN, G_MLA_KVN, G_MLA_Q, G_MLA_K, G_DIL_Q, G_DIL_K, G_GQA_Q, G_GQA_K,
 G_DIFF_Q0, G_DIFF_Q1, G_DIFF_K) = range(12)
GAIN_ROWS = 16

TM_PROJ = 512
TQ = 256
TQ_FULL = 512
TM_OUT = 512
TM_MOE = 256
TM_ROWS_DMA = 1024
DIL_QBLK = 128

PAIRS_PER_GROUP = 6
N_CLASSES = 4 * PAIRS_PER_GROUP
CLS_ROWS = 32
XS_COLS = D_MODEL + LANES


def _bf(x):
    return x.astype(BF16)


def _dot(a, b):
    return jnp.dot(a, b, preferred_element_type=F32)


def _dot_nt(a, b):
    return lax.dot_general(a, b, (((1,), (1,)), ((), ())), preferred_element_type=F32)


def _rms(x, width):
    return x * lax.rsqrt(jnp.sum(x * x, axis=-1, keepdims=True) * (1.0 / width) + NORM_EPS)


def _group_sumsq(x, g):
    x2 = x * x
    hi = _bf(x2)
    lo = _bf(x2 - hi.astype(F32))
    w = g.shape[0]
    outs = []
    for c in range(x.shape[1] // w):
        sl = slice(w * c, w * (c + 1))
        outs.append(_dot(hi[:, sl], g) + _dot(lo[:, sl], g))
    return outs[0] if len(outs) == 1 else jnp.concatenate(outs, axis=1)


def _group_rms(x, g, group):
    return x * lax.rsqrt(_group_sumsq(x, g) * (1.0 / group) + NORM_EPS)


def _rope(x, c, s_next, s_prev, half):
    n = x.shape[1]
    return x * c + pltpu.roll(x, n - half, axis=1) * s_next + pltpu.roll(x, half, axis=1) * s_prev


def _inproj_kernel(x_ref, gv_ref, w_ref, wuq_ref, wuk_ref, wuv_ref, rope_ref, g32_ref, g64_ref, g128_ref,
                   o_ref, o4_ref, o16_ref, stage):
    def gain(row, width):
        return gv_ref[row:row + 1, 0:width]

    x = x_ref[...]
    h = _rms(x, D_MODEL) * gain(G_NORM1, D_MODEL)
    proj = _dot(_bf(h), w_ref[...])

    g32 = g32_ref[...]
    g64 = g64_ref[...]
    g128 = g128_ref[...]
    m_c = rope_ref[:, 0:128]
    m_sn = rope_ref[:, 128:256]
    m_sp = rope_ref[:, 256:384]
    a_c = rope_ref[:, 384:512]
    a_sn = rope_ref[:, 512:640]
    a_sp = rope_ref[:, 640:768]

    cq = _rms(proj[:, P_CQ:P_CQ + MLA_Q_RANK], MLA_Q_RANK) * gain(G_MLA_QN, MLA_Q_RANK)
    ckv = _rms(proj[:, P_CKV:P_CKV + MLA_KV_RANK], MLA_KV_RANK) * gain(G_MLA_KVN, MLA_KV_RANK)
    q = _dot(_bf(cq), wuq_ref[...])
    k_nope = _dot(_bf(ckv), wuk_ref[...])
    v = _dot(_bf(ckv), wuv_ref[...])
    k_rope = pltpu.roll(proj[:, P_KR:P_KR + LANES], MLA_NOPE_DIM, axis=1)
    k = k_nope + jnp.concatenate([k_rope] * N_HEADS, axis=1)
    qn = _group_rms(q, g128, MLA_QK_DIM) * gain(G_MLA_Q, 512)
    kn = _group_rms(k, g128, MLA_QK_DIM) * gain(G_MLA_K, 512)
    for g in range(N_HEADS):
        sl = slice(LANES * g, LANES * (g + 1))
        o_ref[:, A_QA + LANES * g:A_QA + LANES * (g + 1)] = _bf(_rope(qn[:, sl], m_c, m_sn, m_sp, MLA_ROPE_DIM // 2))
        o_ref[:, A_KA + LANES * g:A_KA + LANES * (g + 1)] = _bf(_rope(kn[:, sl], m_c, m_sn, m_sp, MLA_ROPE_DIM // 2))
    o_ref[:, A_VA:A_VA + 256] = _bf(v)

    qb = _group_rms(proj[:, P_BQ:P_BQ + 256], g64, HEAD_DIM) * gain(G_DIL_Q, 256)
    kb = _group_rms(proj[:, P_BK:P_BK + 256], g64, HEAD_DIM) * gain(G_DIL_K, 256)
    vb = proj[:, P_BV:P_BV + 256]
    o_ref[:, A_QB:A_QB + 256] = _bf(qb)
    o_ref[:, A_KB:A_KB + 256] = _bf(kb)
    o_ref[:, A_VB:A_VB + 256] = _bf(vb)
    for c, val in enumerate((qb, kb, vb)):
        stage[2 * c] = val[:, 0:LANES]
        stage[2 * c + 1] = val[:, LANES:2 * LANES]
    n_chunks = DIL_COLS // LANES
    for d, ref in ((4, o4_ref), (16, o16_ref)):
        n = x.shape[0] // d
        for r in range(d):
            ref[0, r] = _bf(jnp.concatenate(
                [stage[c, pl.ds(r, n, stride=d), :] for c in range(n_chunks)], axis=1))

    qc = _group_rms(proj[:, P_CQ2:P_CQ2 + 256], g64, HEAD_DIM) * gain(G_GQA_Q, 256)
    for g in range(2):
        sl = slice(LANES * g, LANES * (g + 1))
        o_ref[:, A_QC + LANES * g:A_QC + LANES * (g + 1)] = _bf(_rope(qc[:, sl], a_c, a_sn, a_sp, HEAD_DIM // 4))
    kc = _group_rms(proj[:, P_CK2:P_CK2 + 128], g64[0:128, 0:128], HEAD_DIM) * gain(G_GQA_K, 128)
    o_ref[:, A_KC:A_KC + 128] = _bf(_rope(kc, a_c, a_sn, a_sp, HEAD_DIM // 4))
    o_ref[:, A_VC:A_VC + 128] = _bf(proj[:, P_CV2:P_CV2 + 128])

    dqn = _group_rms(proj[:, P_DQ:P_DQ + 256], g32, DIFF_QK_DIM)
    o_ref[:, A_QD0:A_QD0 + 256] = _bf(dqn * gain(G_DIFF_Q0, 256))
    o_ref[:, A_QD1:A_QD1 + 256] = _bf(dqn * gain(G_DIFF_Q1, 256))
    o_ref[:, A_KD:A_KD + 256] = _bf(_group_rms(proj[:, P_DK:P_DK + 256], g32, DIFF_QK_DIM) * gain(G_DIFF_K, 256))
    o_ref[:, A_VD:A_VD + 256] = _bf(proj[:, P_DV:P_DV + 256])


def _inproj(x2d, gv, w_in, wuq, wuk, wuv, rope, g32, g64, g128):
    t = x2d.shape[0]
    tm = TM_PROJ
    n_pos = SEQ // tm
    const = lambda i: (0, 0)
    return pl.pallas_call(
        _inproj_kernel,
        grid=(t // tm,),
        in_specs=[
            pl.BlockSpec((tm, D_MODEL), lambda i: (i, 0)),
            pl.BlockSpec((GAIN_ROWS, D_MODEL), const),
            pl.BlockSpec((D_MODEL, PROJ_COLS), const),
            pl.BlockSpec((MLA_Q_RANK, 512), const),
            pl.BlockSpec((MLA_KV_RANK, 512), const),
            pl.BlockSpec((MLA_KV_RANK, 256), const),
            pl.BlockSpec((tm, 768), lambda i: (i % n_pos, 0)),
            pl.BlockSpec((MXU_DIM, MXU_DIM), const),
            pl.BlockSpec((MXU_DIM, MXU_DIM), const),
            pl.BlockSpec((MXU_DIM, MXU_DIM), const),
        ],
        out_specs=[
            pl.BlockSpec((tm, ACT_COLS), lambda i: (i, 0)),
            pl.BlockSpec((1, 4, tm // 4, DIL_COLS), lambda i: (i // n_pos, 0, i % n_pos, 0)),
            pl.BlockSpec((1, 16, tm // 16, DIL_COLS), lambda i: (i // n_pos, 0, i % n_pos, 0)),
        ],
        out_shape=[
            jax.ShapeDtypeStruct((t, ACT_COLS), BF16),
            jax.ShapeDtypeStruct((t // SEQ, 4, SEQ // 4, DIL_COLS), BF16),
            jax.ShapeDtypeStruct((t // SEQ, 16, SEQ // 16, DIL_COLS), BF16),
        ],
        scratch_shapes=[pltpu.VMEM((DIL_COLS // LANES, tm, LANES), F32)],
        compiler_params=pltpu.CompilerParams(dimension_semantics=("parallel",), vmem_limit_bytes=VMEM_LIMIT_BYTES),
        name="inproj_prep",
    )(x2d, gv, w_in, wuq, wuk, wuv, rope, g32, g64, g128)


def _softmax_pv(s, v):
    m = jnp.max(s, axis=-1, keepdims=True)
    e = jnp.exp2(s - m)
    l = jnp.sum(e, axis=-1, keepdims=True)
    return _dot(_bf(e), v) / l


def _heads_one_ahead(scores_of, finish):
    pending = scores_of(0)
    for h in range(N_HEADS):
        s = pending
        if h + 1 < N_HEADS:
            pending = scores_of(h + 1)
        finish(h, s)


def _attn_mla_kernel(q_ref, k_ref, v_ref, o_ref):
    def scores_of(h):
        sl = slice(LANES * h, LANES * (h + 1))
        return _dot_nt(q_ref[:, sl], k_ref[:, sl])

    def finish(h, s):
        hs = slice(HEAD_DIM * h, HEAD_DIM * (h + 1))
        o_ref[:, hs] = _softmax_pv(s, v_ref[:, hs])
    _heads_one_ahead(scores_of, finish)


def _attn_gqa_kernel(q_ref, k_ref, v_ref, o_ref):
    group = lambda h: slice(HEAD_DIM * (h // 2), HEAD_DIM * (h // 2 + 1))

    def scores_of(h):
        return _dot_nt(q_ref[:, HEAD_DIM * h:HEAD_DIM * (h + 1)], k_ref[:, group(h)])

    def finish(h, s):
        o_ref[:, HEAD_DIM * h:HEAD_DIM * (h + 1)] = _softmax_pv(s, v_ref[:, group(h)])
    _heads_one_ahead(scores_of, finish)


def _attn_full(act, batch, kernel, q_col, q_w, k_col, k_w, v_col, v_w, name):
    t = act.shape[0]
    tq = TQ_FULL
    nq = SEQ // tq
    return pl.pallas_call(
        kernel,
        grid=(batch, nq),
        in_specs=[
            pl.BlockSpec((tq, q_w), lambda b, i: (b * nq + i, q_col // q_w)),
            pl.BlockSpec((SEQ, k_w), lambda b, i: (b, k_col // k_w)),
            pl.BlockSpec((SEQ, v_w), lambda b, i: (b, v_col // v_w)),
        ],
        out_specs=pl.BlockSpec((tq, 256), lambda b, i: (b * nq + i, 0)),
        out_shape=jax.ShapeDtypeStruct((t, 256), F32),
        compiler_params=pltpu.CompilerParams(
            dimension_semantics=("parallel", "parallel"), vmem_limit_bytes=VMEM_LIMIT_BYTES),
        name=name,
    )(act, act, act)


def _attn_diff_kernel(lam_ref, q0_ref, q1_ref, k_ref, v_ref, win_ref, sg_ref, o_ref, bias_ref):
    @pl.when(pl.program_id(1) == 0)
    def _():
        for h in range(N_HEADS):
            w = jnp.broadcast_to(win_ref[0, h:h + 1, :], (TQ, SEQ + TQ))
            bias_ref[h] = pltpu.roll(w, 0, axis=1, stride=1, stride_axis=0)[:, TQ:TQ + SEQ]

    lam = lam_ref[0]

    def scores_of(h):
        hs = slice(HEAD_DIM * h, HEAD_DIM * (h + 1))
        k = k_ref[:, hs]
        return _dot_nt(q0_ref[:, hs], k), _dot_nt(q1_ref[:, hs], k)

    def finish(h, scores):
        hs = slice(HEAD_DIM * h, HEAD_DIM * (h + 1))
        bias = bias_ref[h]
        s0, s1 = scores[0] + bias, scores[1] + bias
        e0 = jnp.exp2(s0 - jnp.max(s0, axis=-1, keepdims=True))
        e1 = jnp.exp2(s1 - jnp.max(s1, axis=-1, keepdims=True))
        r0 = 1.0 / jnp.sum(e0, axis=-1, keepdims=True)
        r1 = lam / jnp.sum(e1, axis=-1, keepdims=True)
        v = v_ref[:, hs]
        o = r0 * _dot(_bf(e0), v) - r1 * _dot(_bf(e1), v)
        o_ref[:, hs] = _rms(o, HEAD_DIM) * sg_ref[:, hs]
    _heads_one_ahead(scores_of, finish)


def _attn_diff(act, batch, lam, bias_win, sub_gain):
    t = act.shape[0]
    nq = SEQ // TQ
    return pl.pallas_call(
        _attn_diff_kernel,
        grid=(nq, batch),
        in_specs=[
            pl.BlockSpec(memory_space=pltpu.SMEM),
            pl.BlockSpec((TQ, 256), lambda i, b: (b * nq + i, A_QD0 // 256)),
            pl.BlockSpec((TQ, 256), lambda i, b: (b * nq + i, A_QD1 // 256)),
            pl.BlockSpec((SEQ, 256), lambda i, b: (b, A_KD // 256)),
            pl.BlockSpec((SEQ, 256), lambda i, b: (b, A_VD // 256)),
            pl.BlockSpec((1, N_HEADS, SEQ + TQ), lambda i, b: (i, 0, 0)),
            pl.BlockSpec((1, 256), lambda i, b: (0, 0)),
        ],
        out_specs=pl.BlockSpec((TQ, 256), lambda i, b: (b * nq + i, 0)),
        out_shape=jax.ShapeDtypeStruct((t, 256), F32),
        scratch_shapes=[pltpu.VMEM((N_HEADS, TQ, SEQ), F32)],
        compiler_params=pltpu.CompilerParams(
            dimension_semantics=("arbitrary", "arbitrary"), vmem_limit_bytes=VMEM_LIMIT_BYTES),
        name="attn_diff",
    )(lam, act, act, act, act, bias_win, sub_gain)


def _dil_branch(q_of, k_of, v_of, bias_of, n_seq):
    outs, lses = [], []
    for h in range(N_HEADS):
        hs = slice(HEAD_DIM * h, HEAD_DIM * (h + 1))
        q, k, v = q_of(hs), k_of(hs), v_of(hs)
        n = q.shape[1]
        if n == DIL_QBLK:
            kw, vw = k, v
            bias = bias_of(h)[:, DIL_HALF:DIL_HALF + DIL_QBLK]
            edge = None
        else:
            n_blk = n // DIL_QBLK

            def windows(x):
                zeros = jnp.zeros((n_seq, DIL_HALF, HEAD_DIM), BF16)
                xp = jnp.concatenate([zeros, x, zeros], axis=1)
                first = xp[:, 0:n].reshape(n_seq * n_blk, DIL_QBLK, HEAD_DIM)
                second = xp[:, DIL_QBLK:DIL_QBLK + n].reshape(n_seq * n_blk, DIL_QBLK, HEAD_DIM)
                return jnp.concatenate([first, second], axis=1)
            kw, vw = windows(k), windows(v)
            q = q.reshape(n_seq * n_blk, DIL_QBLK, HEAD_DIM)
            bias = bias_of(h)
            blk = lax.broadcasted_iota(jnp.int32, (n_seq * n_blk, 1, 2 * DIL_QBLK), 0) % n_blk
            col = lax.broadcasted_iota(jnp.int32, (n_seq * n_blk, 1, 2 * DIL_QBLK), 2)
            key_pos = col - DIL_HALF + blk * DIL_QBLK
            edge = jnp.where(jnp.logical_and(key_pos >= 0, key_pos < n), 0.0, NEG_INF)
        s = jnp.einsum("bqe,bke->bqk", q, kw, preferred_element_type=F32) + bias[None]
        if edge is not None:
            s = s + edge
        m = jnp.max(s, axis=-1, keepdims=True)
        e = jnp.exp2(s - m)
        l = jnp.sum(e, axis=-1, keepdims=True)
        o = jnp.einsum("bqk,bke->bqe", _bf(e), vw, preferred_element_type=F32) / l
        lse = jnp.broadcast_to(m + jnp.log2(l), o.shape)
        outs.append(o.reshape(n_seq, n, HEAD_DIM))
        lses.append(lse.reshape(n_seq, n, HEAD_DIM))
    cat = lambda xs: [jnp.concatenate(xs[2 * c:2 * c + 2], axis=2) for c in range(2)]
    return cat(outs), cat(lses)


def _dil_kernel(n1_ref, r4_ref, r16_ref, bias_ref, o_ref, o1, l1, o4, l4, out):
    col_of = lambda base: (lambda hs: slice(base + hs.start, base + hs.stop))
    q_cols, k_cols, v_cols = col_of(0), col_of(256), col_of(512)

    oc, lc = _dil_branch(lambda hs: n1_ref[:, q_cols(hs)][None], lambda hs: n1_ref[:, k_cols(hs)][None],
                         lambda hs: n1_ref[:, v_cols(hs)][None], lambda h: bias_ref[0, h], 1)
    for c in range(2):
        o1[c] = oc[c][0]
        l1[c] = lc[c][0]

    oc, lc = _dil_branch(lambda hs: r4_ref[0, :, :, q_cols(hs)], lambda hs: r4_ref[0, :, :, k_cols(hs)],
                         lambda hs: r4_ref[0, :, :, v_cols(hs)], lambda h: bias_ref[1, h], 4)
    for c in range(2):
        for r in range(4):
            tok = pl.ds(r, SEQ // 4, stride=4)
            o4[c, tok, :] = oc[c][r]
            l4[c, tok, :] = lc[c][r]

    oc, lc = _dil_branch(lambda hs: r16_ref[0, :, :, q_cols(hs)], lambda hs: r16_ref[0, :, :, k_cols(hs)],
                         lambda hs: r16_ref[0, :, :, v_cols(hs)], lambda h: bias_ref[2, h], 16)
    toks = [pl.ds(r, SEQ // 16, stride=16) for r in range(16)]
    for c in range(2):
        gather = lambda ref: jnp.stack([ref[c, tok, :] for tok in toks])
        la, lb = gather(l1), gather(l4)
        mx = jnp.maximum(jnp.maximum(la, lb), lc[c])
        wa, wb, wc = jnp.exp2(la - mx), jnp.exp2(lb - mx), jnp.exp2(lc[c] - mx)
        den = wa + wb + wc
        res = (wa / den) * gather(o1) + (wb / den) * gather(o4) + (wc / den) * oc[c]
        for r, tok in enumerate(toks):
            out[c, tok, :] = res[r]
    o_ref[:, 0:LANES] = out[0]
    o_ref[:, LANES:2 * LANES] = out[1]


def _attn_dilated(act, act4, act16, batch, bias_tabs):
    t = act.shape[0]
    nat = pltpu.VMEM((2, SEQ, LANES), F32)
    return pl.pallas_call(
        _dil_kernel,
        grid=(batch,),
        in_specs=[
            pl.BlockSpec((SEQ, DIL_COLS), lambda b: (b, 0)),
            pl.BlockSpec((1, 4, SEQ // 4, DIL_COLS), lambda b: (b, 0, 0, 0)),
            pl.BlockSpec((1, 16, SEQ // 16, DIL_COLS), lambda b: (b, 0, 0, 0)),
            pl.BlockSpec((3, N_HEADS, DIL_QBLK, 2 * DIL_QBLK), lambda b: (0, 0, 0, 0)),
        ],
        out_specs=pl.BlockSpec((SEQ, 256), lambda b: (b, 0)),
        out_shape=jax.ShapeDtypeStruct((t, 256), F32),
        scratch_shapes=[nat, nat, nat, nat, nat],
        compiler_params=pltpu.CompilerParams(dimension_semantics=("parallel",), vmem_limit_bytes=VMEM_LIMIT_BYTES),
        name="attn_dilated",
    )(act, act4, act16, bias_tabs)


def _route(lt):
    g = [lt[i:i + 1, :] for i in range(4)]
    gmax = jnp.maximum(jnp.maximum(g[0], g[1]), jnp.maximum(g[2], g[3]))
    gsum = sum(jnp.exp(gi - gmax) for gi in g)
    g_w = 1.0 / gsum
    gidx = jnp.where(g[0] == gmax, 0, jnp.where(g[1] == gmax, 1, jnp.where(g[2] == gmax, 2, 3)))
    el = []
    for j in range(EXPERTS_PER_GROUP):
        acc = jnp.zeros_like(g[0])
        for i in range(4):
            r = 4 + EXPERTS_PER_GROUP * i + j
            acc = jnp.where(gidx == i, lt[r:r + 1, :], acc)
        el.append(acc)
    emax = jnp.maximum(jnp.maximum(el[0], el[1]), jnp.maximum(el[2], el[3]))
    ee = [jnp.exp(e - emax) for e in el]
    esum = ee[0] + ee[1] + ee[2] + ee[3]
    p = [e / esum for e in ee]
    p1 = jnp.maximum(jnp.maximum(p[0], p[1]), jnp.maximum(p[2], p[3]))
    i1 = jnp.where(p[0] == p1, 0, jnp.where(p[1] == p1, 1, jnp.where(p[2] == p1, 2, 3)))
    pm = [jnp.where(i1 == j, -1.0, p[j]) for j in range(4)]
    p2 = jnp.maximum(jnp.maximum(pm[0], pm[1]), jnp.maximum(pm[2], pm[3]))
    i2 = jnp.where(pm[0] == p2, 0, jnp.where(pm[1] == p2, 1, jnp.where(pm[2] == p2, 2, 3)))
    den = p1 + p2
    return gidx, i1, i2, g_w * (p1 / den), g_w * (p2 / den)


def _pair_offset(lo):
    return jnp.where(lo == 0, 0, jnp.where(lo == 1, 3, 5))


def _outproj_kernel(ya_ref, yb_ref, yc_ref, yd_ref, x_ref, cv_ref, w_ref, wrh_ref, wrl_ref, rb_ref, tri_ref,
                    x1_ref, hp_ref, ri_ref, cnt_out_ref, cnt_ref):
    tm = x_ref.shape[0]

    @pl.when(pl.program_id(0) == 0)
    def _():
        cnt_ref[...] = jnp.zeros_like(cnt_ref)

    mixed = jnp.concatenate(
        [_rms(ya_ref[...], 256), _rms(yb_ref[...], 256), _rms(yc_ref[...], 256), yd_ref[...]], axis=1)
    mixed = mixed * cv_ref[0:1, :]
    x1 = x_ref[...] + _dot(_bf(mixed), w_ref[...])
    x1_ref[...] = x1
    h2 = _rms(x1, D_MODEL) * cv_ref[1:2, :]
    h_hi = _bf(h2)
    h_rt = h_hi.astype(F32)
    h_lo = _bf(h2 - h_rt)
    logits = _dot(h_hi, wrh_ref[...]) + _dot(h_hi, wrl_ref[...]) + _dot(h_lo, wrh_ref[...]) + rb_ref[...]
    gidx, i1, i2, wa, wb = _route(logits.T)

    lo, hi = jnp.minimum(i1, i2), jnp.maximum(i1, i2)
    first_is_lo = i1 < i2
    w_lo, w_hi = jnp.where(first_is_lo, wa, wb), jnp.where(first_is_lo, wb, wa)
    cls = gidx * PAIRS_PER_GROUP + _pair_offset(lo) + (hi - lo - 1)

    rows = lax.broadcasted_iota(jnp.int32, (CLS_ROWS, tm), 0)
    onehot = rows == cls
    oh = jnp.where(onehot, 1.0, 0.0)
    prefix = _dot(_bf(oh), tri_ref[...])
    before = cnt_ref[:, 0:1] + prefix
    rank = jnp.sum(jnp.where(onehot, before, 0.0), axis=0, keepdims=True)
    cnt_ref[...] = cnt_ref[...] + jnp.sum(oh, axis=1, keepdims=True)
    cnt_out_ref[...] = cnt_ref[...]
    ri_ref[...] = jnp.concatenate([cls, rank.astype(jnp.int32), jnp.zeros((6, tm), jnp.int32)], axis=0)

    hp_ref[:, 0:D_MODEL] = h_rt
    rows_w = lax.broadcasted_iota(jnp.int32, (LANES, tm), 0)
    w_t = jnp.where(rows_w == 0, w_lo, jnp.where(rows_w == 1, w_hi, 0.0))
    hp_ref[:, D_MODEL:XS_COLS] = w_t.T


def _outproj(ya, yb, yc, yd, x2d, cv, w_out, wr_hi, wr_lo, rb, tri):
    t = x2d.shape[0]
    tm = TM_OUT
    row = lambda w: pl.BlockSpec((tm, w), lambda i: (i, 0))
    const = lambda shape: pl.BlockSpec(shape, lambda i: (0, 0))
    return pl.pallas_call(
        _outproj_kernel,
        grid=(t // tm,),
        in_specs=[row(256), row(256), row(256), row(256), row(D_MODEL), const((8, D_MODEL)),
                  const((D_MODEL, D_MODEL)), const((D_MODEL, LANES)), const((D_MODEL, LANES)), const((1, LANES)),
                  const((tm, tm))],
        out_specs=[row(D_MODEL), row(XS_COLS), pl.BlockSpec((8, tm), lambda i: (0, i)), const((CLS_ROWS, LANES))],
        out_shape=[jax.ShapeDtypeStruct((t, D_MODEL), F32), jax.ShapeDtypeStruct((t, XS_COLS), F32),
                   jax.ShapeDtypeStruct((8, t), jnp.int32), jax.ShapeDtypeStruct((CLS_ROWS, LANES), F32)],
        scratch_shapes=[pltpu.VMEM((CLS_ROWS, LANES), F32)],
        compiler_params=pltpu.CompilerParams(dimension_semantics=("arbitrary",), vmem_limit_bytes=VMEM_LIMIT_BYTES),
        name="outproj_router",
    )(ya, yb, yc, yd, x2d, cv, w_out, wr_hi, wr_lo, rb, tri)


def _moe_plan(ri, counts, n_tiles):
    cls, rank = ri[0], ri[1]
    cnt = counts[:N_CLASSES, 0].astype(jnp.int32)
    padded = ((cnt + TM_MOE - 1) // TM_MOE) * TM_MOE
    ends = jnp.cumsum(padded)
    offs = ends - padded
    classes = jnp.arange(N_CLASSES, dtype=jnp.int32)
    dest = jnp.sum(jnp.where(cls[:, None] == classes[None, :], offs[None, :], 0), axis=1) + rank
    tile_start = jnp.arange(n_tiles, dtype=jnp.int32) * TM_MOE
    tile_cls = jnp.sum((tile_start[:, None] >= ends[None, :]).astype(jnp.int32), axis=1)
    tile_cls = jnp.minimum(tile_cls, N_CLASSES - 1)
    group, pair = tile_cls // PAIRS_PER_GROUP, tile_cls % PAIRS_PER_GROUP
    lo = (pair >= 3).astype(jnp.int32) + (pair >= 5).astype(jnp.int32)
    hi = pair - _pair_offset(lo) + lo + 1
    n_active = (ends[-1] // TM_MOE).reshape(1)
    return dest, group * EXPERTS_PER_GROUP + lo, group * EXPERTS_PER_GROUP + hi, n_active


def _dispatch_kernel(dest_ref, h_ref, xs_in_ref, xs_ref, stage, sems):
    del xs_in_ref
    tm = h_ref.shape[0]
    i = pl.program_id(0)
    slot = i % 2

    def wait_slot(s):
        pltpu.make_async_copy(stage.at[s], xs_ref.at[pl.ds(0, tm), :], sems.at[s]).wait()

    @pl.when(i >= 2)
    def _():
        wait_slot(slot)

    stage[slot] = h_ref[...]
    for r in range(tm):
        pltpu.make_async_copy(stage.at[slot, pl.ds(r, 1), :], xs_ref.at[pl.ds(dest_ref[0, 0, r], 1), :],
                              sems.at[slot]).start()

    @pl.when(i == pl.num_programs(0) - 1)
    def _():
        wait_slot(1 - slot)
        wait_slot(slot)


def _dispatch(dest, hp, n_rows):
    t = hp.shape[0]
    tm = TM_ROWS_DMA
    assert t // tm >= 2
    return pl.pallas_call(
        _dispatch_kernel,
        grid=(t // tm,),
        in_specs=[
            pl.BlockSpec((1, 1, tm), lambda i: (i, 0, 0), memory_space=pltpu.SMEM),
            pl.BlockSpec((tm, XS_COLS), lambda i: (i, 0)),
            pl.BlockSpec(memory_space=pl.ANY),
        ],
        out_specs=pl.BlockSpec(memory_space=pl.ANY),
        out_shape=jax.ShapeDtypeStruct((n_rows, XS_COLS), F32),
        scratch_shapes=[pltpu.VMEM((2, tm, XS_COLS), F32), pltpu.SemaphoreType.DMA((2,))],
        input_output_aliases={2: 0},
        compiler_params=pltpu.CompilerParams(dimension_semantics=("arbitrary",)),
        name="moe_dispatch",
    )(dest.reshape(t // tm, 1, tm), hp, jnp.zeros((n_rows, XS_COLS), F32))


def _moe_group_kernel(elo_ref, ehi_ref, nact_ref, xs_ref, wg0, wu0, wd0, wg1, wu1, wd1, o_ref, wg_s, wu_s, wd_s):
    i = pl.program_id(0)
    active = i < nact_ref[0]
    prev = jnp.maximum(i - 1, 0)
    changed = jnp.logical_or(i == 0, jnp.logical_or(elo_ref[i] != elo_ref[prev], ehi_ref[i] != ehi_ref[prev]))

    @pl.when(jnp.logical_and(active, changed))
    def _():
        for slot, (wg, wu, wd) in enumerate(((wg0, wu0, wd0), (wg1, wu1, wd1))):
            wg_s[slot] = _bf(wg[0, 0])
            wu_s[slot] = _bf(wu[0, 0])
            wd_s[slot] = _bf(wd[0, 0])

    @pl.when(active)
    def _():
        x = _bf(xs_ref[:, 0:D_MODEL])
        gate = xs_ref[:, D_MODEL:XS_COLS]

        def expert(slot):
            a = _dot(x, wg_s[slot])
            hid = a * (1.0 / (1.0 + jnp.exp(-a))) * _dot(x, wu_s[slot])
            return _dot(_bf(hid), wd_s[slot])
        o_ref[...] = gate[:, 0:1] * expert(0) + gate[:, 1:2] * expert(1)

    @pl.when(jnp.logical_not(active))
    def _():
        o_ref[...] = jnp.zeros_like(o_ref)


def _moe_group(layer, e_lo, e_hi, n_active, xs, wg, wu, wd):
    n_rows = xs.shape[0]
    tm = TM_MOE
    w_in = lambda which: pl.BlockSpec((1, 1, D_MODEL, D_FF), lambda i, lo, hi, na: (layer, (lo, hi)[which][i], 0, 0))
    w_dn = lambda which: pl.BlockSpec((1, 1, D_FF, D_MODEL), lambda i, lo, hi, na: (layer, (lo, hi)[which][i], 0, 0))
    return pl.pallas_call(
        _moe_group_kernel,
        grid_spec=pltpu.PrefetchScalarGridSpec(
            num_scalar_prefetch=3,
            grid=(n_rows // tm,),
            in_specs=[pl.BlockSpec((tm, XS_COLS), lambda i, lo, hi, na: (i, 0)),
                      w_in(0), w_in(0), w_dn(0), w_in(1), w_in(1), w_dn(1)],
            out_specs=pl.BlockSpec((tm, D_MODEL), lambda i, lo, hi, na: (i, 0)),
            scratch_shapes=[pltpu.VMEM((2, D_MODEL, D_FF), BF16), pltpu.VMEM((2, D_MODEL, D_FF), BF16),
                            pltpu.VMEM((2, D_FF, D_MODEL), BF16)],
        ),
        out_shape=jax.ShapeDtypeStruct((n_rows, D_MODEL), F32),
        compiler_params=pltpu.CompilerParams(dimension_semantics=("arbitrary",), vmem_limit_bytes=VMEM_LIMIT_BYTES),
        name="moe_group",
    )(e_lo, e_hi, n_active, xs, wg, wu, wd, wg, wu, wd)


def _combine_kernel(dest_ref, dest_next_ref, x1_ref, y_ref, o_ref, buf, sems):
    tm = x1_ref.shape[0]
    i = pl.program_id(0)
    slot = i % 2

    def gather(idx_ref, s):
        for r in range(tm):
            pltpu.make_async_copy(y_ref.at[pl.ds(idx_ref[0, 0, r], 1), :], buf.at[s, pl.ds(r, 1), :],
                                  sems.at[s]).start()

    @pl.when(i == 0)
    def _():
        gather(dest_ref, 0)

    @pl.when(i + 1 < pl.num_programs(0))
    def _():
        gather(dest_next_ref, 1 - slot)

    pltpu.make_async_copy(y_ref.at[pl.ds(0, tm), :], buf.at[slot], sems.at[slot]).wait()
    o_ref[...] = x1_ref[...] + buf[slot]


def _combine(dest, x1, y):
    t = x1.shape[0]
    tm = TM_ROWS_DMA
    n = t // tm
    dest3 = dest.reshape(n, 1, tm)
    return pl.pallas_call(
        _combine_kernel,
        grid=(n,),
        in_specs=[
            pl.BlockSpec((1, 1, tm), lambda i: (i, 0, 0), memory_space=pltpu.SMEM),
            pl.BlockSpec((1, 1, tm), lambda i: (jnp.minimum(i + 1, n - 1), 0, 0), memory_space=pltpu.SMEM),
            pl.BlockSpec((tm, D_MODEL), lambda i: (i, 0)),
            pl.BlockSpec(memory_space=pl.ANY),
        ],
        out_specs=pl.BlockSpec((tm, D_MODEL), lambda i: (i, 0)),
        out_shape=jax.ShapeDtypeStruct((t, D_MODEL), F32),
        scratch_shapes=[pltpu.VMEM((2, tm, D_MODEL), F32), pltpu.SemaphoreType.DMA((2,))],
        compiler_params=pltpu.CompilerParams(dimension_semantics=("arbitrary",)),
        name="moe_combine",
    )(dest3, dest3, x1, y)


def _rope_angles(pos, dim):
    inv = 1.0 / (ROPE_THETA ** (jnp.arange(0, dim, 2, dtype=F32) / dim))
    return pos.astype(F32)[:, None] * inv[None, :]


def _rope_tables():
    pos = jnp.arange(SEQ, dtype=jnp.int32)
    z = lambda w: jnp.zeros((SEQ, w), F32)
    ang = _rope_angles(pos, MLA_ROPE_DIM)
    c, s = jnp.cos(ang), jnp.sin(ang)
    m_c = jnp.concatenate([jnp.ones((SEQ, 64), F32), c, c, z(32)], axis=1)
    m_sn = jnp.concatenate([z(64), -s, z(48)], axis=1)
    m_sp = jnp.concatenate([z(80), s, z(32)], axis=1)
    row_pos = pos // GRID_W
    col_pos = pos - row_pos * GRID_W
    ra, ca = _rope_angles(row_pos, HEAD_DIM // 2), _rope_angles(col_pos, HEAD_DIM // 2)
    rc, rs, cc, cs = jnp.cos(ra), jnp.sin(ra), jnp.cos(ca), jnp.sin(ca)
    a_c = jnp.concatenate([rc, rc, cc, cc] * 2, axis=1)
    a_sn = jnp.concatenate([-rs, z(16), -cs, z(16)] * 2, axis=1)
    a_sp = jnp.concatenate([z(16), rs, z(16), cs] * 2, axis=1)
    return jnp.concatenate([m_c, m_sn, m_sp, a_c, a_sn, a_sp], axis=1)


def _rel_bucket(rel):
    half = NUM_BUCKETS // 2
    max_exact = half // 2
    n = jnp.abs(rel)
    nf = jnp.maximum(n, 1).astype(F32)
    log_ratio = jnp.log(nf / max_exact) / math.log(REL_MAX_DISTANCE / max_exact)
    large = jnp.minimum(max_exact + (log_ratio * (half - max_exact)).astype(jnp.int32), half - 1)
    return jnp.where(rel > 0, half, 0) + jnp.where(n < max_exact, n, large)


def _bias_line(table, rel):
    return table[_rel_bucket(rel)].T


def _toeplitz(line, rows, cols, first):
    n = line.shape[-1]
    padded = jnp.pad(line, [(0, 0)] * (line.ndim - 1) + [(0, 1)])
    flat = jnp.tile(padded, rows)[..., :rows * n]
    skew = flat.reshape(line.shape[:-1] + (rows, n))
    return skew[..., first:first + cols]


def _dil_bias(table):
    reach = DIL_QBLK + DIL_HALF - 1
    steps = jnp.arange(-reach, reach + 1, dtype=jnp.int32)
    band = jnp.abs(steps) <= DIL_HALF
    lines = jnp.stack([jnp.where(band[None], _bias_line(table, steps * d), NEG_INF) for _, d in DIL_PATTERNS])
    return _toeplitz(lines, DIL_QBLK, 2 * DIL_QBLK, DIL_QBLK - 1)


def _diff_bias_windows(table):
    rel = jnp.arange(2 * SEQ, dtype=jnp.int32) - SEQ
    line = _bias_line(table, rel)
    starts = [SEQ - (i + 1) * TQ for i in range(SEQ // TQ)]
    return jnp.stack([line[:, s0:s0 + SEQ + TQ] for s0 in starts])


def _block_ones(group):
    idx = np.arange(MXU_DIM) // group
    return jnp.asarray(idx[:, None] == idx[None, :], dtype=BF16)


def _pad_row(v):
    return jnp.pad(v.astype(F32), (0, D_MODEL - v.shape[0]))


def _layer_params(layer, norm1_g, w_in, mla_q_norm_g, mla_kv_norm_g, mla_w_uq, mla_w_ukv, mla_qk_g, dil_qk_g,
                  gqa_qk_g, diff_qk_g, diff_lambda, diff_subln_g, mix_beta, w_out, norm2_g, router_group_w,
                  router_group_b, router_expert_w, router_expert_b):
    w = w_in[layer]
    w_p = _bf(jnp.concatenate([w[:, :416], jnp.zeros((D_MODEL, 96), F32), w[:, 416:]], axis=1))
    uq = mla_w_uq[layer].reshape(MLA_Q_RANK, N_HEADS, MLA_QK_DIM)
    wuq = _bf(jnp.pad(uq, ((0, 0), (0, 0), (0, LANES - MLA_QK_DIM))).reshape(MLA_Q_RANK, 512))
    ukv = mla_w_ukv[layer].reshape(MLA_KV_RANK, N_HEADS, 2 * MLA_NOPE_DIM)
    wuk = _bf(jnp.pad(ukv[:, :, :MLA_NOPE_DIM], ((0, 0), (0, 0), (0, LANES - MLA_NOPE_DIM))).reshape(MLA_KV_RANK, 512))
    wuv = _bf(ukv[:, :, MLA_NOPE_DIM:].reshape(MLA_KV_RANK, 256))

    pad96 = lambda g: jnp.tile(jnp.pad(g, (0, LANES - MLA_QK_DIM)), N_HEADS)
    m0 = jnp.tile(jnp.concatenate([jnp.ones(32, F32), jnp.zeros(32, F32)]), N_HEADS)
    dq = jnp.tile(diff_qk_g[layer, 0], 2 * N_HEADS) * (DIFF_QK_DIM ** -0.5 * LOG2E)
    rows = [
        norm1_g[layer], mla_q_norm_g[layer], mla_kv_norm_g[layer],
        pad96(mla_qk_g[layer, 0]) * (MLA_QK_DIM ** -0.5 * LOG2E), pad96(mla_qk_g[layer, 1]),
        jnp.tile(dil_qk_g[layer, 0], N_HEADS) * (HEAD_DIM ** -0.5 * LOG2E), jnp.tile(dil_qk_g[layer, 1], N_HEADS),
        jnp.tile(gqa_qk_g[layer, 0], N_HEADS) * (HEAD_DIM ** -0.5 * LOG2E), jnp.tile(gqa_qk_g[layer, 1], 2),
        dq * m0, dq * (1.0 - m0), jnp.tile(diff_qk_g[layer, 1], 2 * N_HEADS),
    ]
    gv = jnp.stack([_pad_row(r) for r in rows] + [jnp.zeros(D_MODEL, F32)] * (GAIN_ROWS - len(rows)))

    lambda_init = 0.8 - 0.6 * math.exp(-0.3 * layer)
    lv = diff_lambda[layer].astype(F32)
    lam = (jnp.exp(jnp.sum(lv[0] * lv[1])) - jnp.exp(jnp.sum(lv[2] * lv[3])) + lambda_init).reshape(1)
    sub_gain = (jnp.tile(diff_subln_g[layer], N_HEADS) * (1.0 - lambda_init)).reshape(1, 256)

    cv = jnp.stack([mix_beta[layer], norm2_g[layer]] + [jnp.zeros(D_MODEL, F32)] * 6)
    wr = jnp.concatenate([router_group_w[layer], router_expert_w[layer],
                          jnp.zeros((D_MODEL, LANES - 4 - N_EXPERTS), F32)], axis=1)
    wr_hi = _bf(wr)
    wr_lo = _bf(wr - wr_hi.astype(F32))
    rb = jnp.concatenate([router_group_b[layer], router_expert_b[layer],
                          jnp.zeros(LANES - 4 - N_EXPERTS, F32)]).reshape(1, LANES)
    return dict(w_in=w_p, wuq=wuq, wuk=wuk, wuv=wuv, gv=gv, lam=lam, sub_gain=sub_gain, cv=cv,
                w_out=_bf(w_out[layer]), wr_hi=wr_hi, wr_lo=wr_lo, rb=rb)


def kernel(x, rel_bias, norm1_g, w_in, mla_q_norm_g, mla_kv_norm_g, mla_w_uq, mla_w_ukv, mla_qk_g, dil_qk_g, gqa_qk_g, diff_qk_g, diff_lambda, diff_subln_g, mix_beta, w_out, norm2_g, router_group_w, router_group_b, router_expert_w, router_expert_b, expert_w_gate, expert_w_up, expert_w_down):
    batch, seq, d_model = x.shape
    assert seq == SEQ and d_model == D_MODEL
    depth = w_in.shape[0]
    t = batch * seq

    rope = _rope_tables()
    g32, g64, g128 = _block_ones(32), _block_ones(64), _block_ones(128)
    tri = jnp.asarray(np.arange(TM_OUT)[:, None] < np.arange(TM_OUT)[None, :], dtype=BF16)
    dil_bias = _dil_bias(rel_bias[:, :N_HEADS] * LOG2E)
    diff_bias = _diff_bias_windows(rel_bias[:, N_HEADS:] * LOG2E)

    x2d = x.reshape(t, D_MODEL)
    for layer in range(depth):
        p = _layer_params(layer, norm1_g, w_in, mla_q_norm_g, mla_kv_norm_g, mla_w_uq, mla_w_ukv, mla_qk_g,
                          dil_qk_g, gqa_qk_g, diff_qk_g, diff_lambda, diff_subln_g, mix_beta, w_out, norm2_g,
                          router_group_w, router_group_b, router_expert_w, router_expert_b)
        act, act4, act16 = _inproj(x2d, p["gv"], p["w_in"], p["wuq"], p["wuk"], p["wuv"], rope, g32, g64, g128)
        ya = _attn_full(act, batch, _attn_mla_kernel, A_QA, 512, A_KA, 512, A_VA, 256, "attn_mla")
        yb = _attn_dilated(act, act4, act16, batch, dil_bias)
        yc = _attn_full(act, batch, _attn_gqa_kernel, A_QC, 256, A_KC, 128, A_VC, 128, "attn_gqa")
        yd = _attn_diff(act, batch, p["lam"], diff_bias, p["sub_gain"])
        x1, hp, ri, counts = _outproj(ya, yb, yc, yd, x2d, p["cv"], p["w_out"], p["wr_hi"], p["wr_lo"], p["rb"], tri)
        n_tiles = t // TM_MOE + N_CLASSES
        dest, e_lo, e_hi, n_active = _moe_plan(ri, counts, n_tiles)
        xs = _dispatch(dest, hp, n_tiles * TM_MOE)
        y = _moe_group(layer, e_lo, e_hi, n_active, xs, expert_w_gate, expert_w_up, expert_w_down)
        x2d = _combine(dest, x1, y)
    return x2d.reshape(batch, seq, D_MODEL)
```
